```python
import math
import jax, jax.numpy as jnp
from jax import lax
import numpy as np

D_MODEL = 1024
BATCH = 8
SEQ = 2048
DEPTH = 1
DEC_BATCH = 128
DEC_SEQ = 1
PAST_LEN = 16384
PAGE_SIZE = 128

D_MIX = 2 * D_MODEL
W_A = D_MIX // 2
H_A = 8
DH_A = W_A // H_A
W_B = D_MIX - W_A
H_B = 4
DH_B = W_B // H_B
CHUNK = 128
CONV_W = 4
D_PLE = 256
N_IN = 3 * W_A + 3 * W_B
EPS = 1e-6

kernel_name = 'hymba_gmlp_mlstm_decode_step'


def rmsnorm(x, g):
    xf = x.astype(jnp.float32)
    y = xf * lax.rsqrt(jnp.mean(xf * xf, axis=-1, keepdims=True) + EPS)
    return (y * g.astype(jnp.float32)).astype(x.dtype)


def head_norm(x, g, b=None):
    xf = x.astype(jnp.float32)
    mu = jnp.mean(xf, axis=-1, keepdims=True)
    var = jnp.mean(jnp.square(xf - mu), axis=-1, keepdims=True)
    y = (xf - mu) * lax.rsqrt(var + EPS) * g.astype(jnp.float32)
    if b is not None:
        y = y + b.astype(jnp.float32)
    return y


def causal_conv(x, buf, w, b):
    T = x.shape[1]
    xpad = jnp.concatenate([buf.astype(x.dtype), x], axis=1)
    y = xpad[:, 0:T] * w[0]
    for j in range(1, CONV_W):
        y = y + xpad[:, j:j + T] * w[j]
    return y + b, xpad[:, -(CONV_W - 1):]


def chunk_spatial(v, w_s, b_s):
    B, T = v.shape[0], v.shape[1]
    L = min(CHUNK, T)
    nc = T // L
    w = w_s[:, :L, :L] * jnp.tril(jnp.ones((L, L), w_s.dtype))
    vc = v.reshape(B, nc, L, H_A, DH_A)
    out = jnp.einsum('hts,bcshd->bcthd', w, vc) + b_s[:, :L].T[None, None, :, :, None]
    return out.reshape(B, T, H_A, DH_A)


def mlstm_chunk(carry, inp):
    C, n, m = carry
    q, k, v, li, lf = inp
    L = q.shape[1]
    b = jnp.cumsum(lf, axis=1)
    causal = jnp.tril(jnp.ones((L, L), bool))
    dmat = b[:, :, None, :] - b[:, None, :, :] + li[:, None, :, :]
    dmat = jnp.where(causal[None, :, :, None], dmat, -jnp.inf)
    inter = b + m[:, None, :]
    m_t = jnp.maximum(inter, jnp.max(dmat, axis=2))
    s = jnp.exp(dmat - m_t[:, :, None, :]) * jnp.einsum('bthd,bshd->btsh', q, k)
    g = jnp.exp(inter - m_t)
    num = jnp.einsum('btsh,bshe->bthe', s, v) + g[..., None] * jnp.einsum('bhed,bthd->bthe', C, q)
    den = jnp.sum(s, axis=2) + g * jnp.einsum('bhd,bthd->bth', n, q)
    h = num / jnp.maximum(jnp.abs(den), jnp.exp(-m_t))[..., None]
    m_new = m_t[:, -1]
    w_end = jnp.exp(b[:, -1:, :] - b + li - m_new[:, None, :])
    g_end = jnp.exp(b[:, -1] + m - m_new)
    C_new = g_end[..., None, None] * C + jnp.einsum('bsh,bshe,bshd->bhed', w_end, v, k)
    n_new = g_end[..., None] * n + jnp.einsum('bsh,bshd->bhd', w_end, k)
    return (C_new, n_new, m_new), h


def mlstm_seq(q, k, v, li, lf, C, n, m):
    B, T = q.shape[0], q.shape[1]
    L = min(CHUNK, T)
    nc = T // L

    def chunks(a):
        return jnp.moveaxis(a.reshape((B, nc, L) + a.shape[2:]), 1, 0)

    (C, n, m), h = lax.scan(mlstm_chunk, (C, n, m), (chunks(q), chunks(k), chunks(v), chunks(li), chunks(lf)))
    h = jnp.moveaxis(h, 0, 1).reshape((B, T) + h.shape[3:])
    return h, C, n, m


def mixer_layer(x, p, norm_g, w_in, ln_v_g, ln_v_b, w_sp, b_sp, conv_w, conv_b, w_q, w_k, w_v,
                w_if, b_if, gn_g, skip, w_out, w_pg, b_pg, w_pp, ple_g, C0, n0, m0, conv0):
    B, T = x.shape[0], x.shape[1]
    h = rmsnorm(x, norm_g)
    proj = h @ w_in
    u, v, z_a, x_b, o_b, z_b = jnp.split(
        proj, [W_A, 2 * W_A, 3 * W_A, 3 * W_A + W_B, 3 * W_A + 2 * W_B], axis=-1)
    u = jax.nn.gelu(u, approximate=False)
    v = jax.nn.gelu(v, approximate=False)
    v_n = head_norm(v.reshape(B, T, H_A, DH_A), ln_v_g, ln_v_b).astype(x.dtype)
    sp = chunk_spatial(v_n, w_sp, b_sp).reshape(B, T, W_A)
    y_a = u * sp * jax.nn.silu(z_a)
    xc, conv_tail = causal_conv(x_b, conv0, conv_w, conv_b)
    xc = jax.nn.silu(xc)
    xch = xc.reshape(B, T, H_B, DH_B)
    xbh = x_b.reshape(B, T, H_B, DH_B)
    q = jnp.einsum('bthd,hde->bthe', xch, w_q)
    k = jnp.einsum('bthd,hde->bthe', xch, w_k)
    vv = jnp.einsum('bthd,hde->bthe', xbh, w_v)
    gin = jnp.concatenate([q.reshape(B, T, W_B), k.reshape(B, T, W_B), vv.reshape(B, T, W_B)], axis=-1)
    gates = (gin @ w_if + b_if).astype(jnp.float32)
    li = gates[..., :H_B]
    lf = jax.nn.log_sigmoid(gates[..., H_B:])
    hb, C, n, m = mlstm_seq(q.astype(jnp.float32), (k * (DH_B ** -0.5)).astype(jnp.float32),
                            vv.astype(jnp.float32), li, lf, C0.astype(jnp.float32),
                            n0.astype(jnp.float32), m0.astype(jnp.float32))
    hb = hb * jax.nn.sigmoid(o_b.reshape(B, T, H_B, DH_B).astype(jnp.float32))
    hb = head_norm(hb, gn_g).reshape(B, T, W_B).astype(x.dtype)
    y_b = (hb + skip * xc) * jax.nn.silu(z_b)
    x = x + jnp.concatenate([y_a, y_b], axis=-1) @ w_out
    e = rmsnorm(p @ w_pp, ple_g)
    x = x + jax.nn.sigmoid(x @ w_pg + b_pg) * e
    return x, C.astype(x.dtype), n.astype(x.dtype), m.astype(x.dtype), conv_tail, v_n.reshape(B, T, W_A)


def setup_inputs(seed: int = 0) -> dict:
    key = jax.random.key(seed)
    ks = jax.random.split(key, 32)

    def nrm(k, shape, s):
        return s * jax.random.normal(k, shape, jnp.float32)

    b_if = jnp.concatenate([
        nrm(ks[20], (DEPTH, H_B), 0.1),
        jnp.linspace(3.0, 6.0, H_B, dtype=jnp.float32)[None, :] + nrm(ks[21], (DEPTH, H_B), 0.01)], axis=-1)
    return {
        'x_prompt': nrm(ks[0], (BATCH, SEQ, D_MODEL), 1.0),
        'x_sample': nrm(ks[1], (DEC_BATCH, DEC_SEQ, D_MODEL), 1.0),
        'state_mlstm_C': nrm(ks[2], (DEPTH, DEC_BATCH, H_B, DH_B, DH_B), 0.1),
        'state_mlstm_n': nrm(ks[3], (DEPTH, DEC_BATCH, H_B, DH_B), 0.1),
        'state_mlstm_m': nrm(ks[4], (DEPTH, DEC_BATCH, H_B), 0.5),
        'state_conv': nrm(ks[5], (DEPTH, DEC_BATCH, CONV_W - 1, W_B), 1.0),
        'p_prompt': nrm(ks[6], (DEPTH, BATCH, SEQ, D_PLE), 1.0),
        'p_sample': nrm(ks[7], (DEPTH, DEC_BATCH, DEC_SEQ, D_PLE), 1.0),
        'norm_in_g': 1.0 + nrm(ks[8], (DEPTH, D_MODEL), 0.02),
        'w_in': nrm(ks[9], (DEPTH, D_MODEL, N_IN), D_MODEL ** -0.5),
        'ln_v_g': 1.0 + nrm(ks[10], (DEPTH, H_A, DH_A), 0.02),
        'ln_v_b': nrm(ks[11], (DEPTH, H_A, DH_A), 0.02),
        'w_spatial': nrm(ks[12], (DEPTH, H_A, CHUNK, CHUNK), CHUNK ** -0.5),
        'b_spatial': 1.0 + nrm(ks[13], (DEPTH, H_A, CHUNK), 0.1),
        'conv_w': nrm(ks[14], (DEPTH, CONV_W, W_B), CONV_W ** -0.5),
        'conv_b': nrm(ks[15], (DEPTH, W_B), 0.02),
        'w_q': nrm(ks[16], (DEPTH, H_B, DH_B, DH_B), DH_B ** -0.5),
        'w_k': nrm(ks[17], (DEPTH, H_B, DH_B, DH_B), DH_B ** -0.5),
        'w_v': nrm(ks[18], (DEPTH, H_B, DH_B, DH_B), DH_B ** -0.5),
        'w_if': nrm(ks[19], (DEPTH, 3 * W_B, 2 * H_B), (3 * W_B) ** -0.5),
        'b_if': b_if,
        'gn_g': 1.0 + nrm(ks[22], (DEPTH, H_B, DH_B), 0.02),
        'skip': 1.0 + nrm(ks[23], (DEPTH, W_B), 0.02),
        'w_out': nrm(ks[24], (DEPTH, D_MIX, D_MODEL), D_MIX ** -0.5),
        'w_ple_gate': nrm(ks[25], (DEPTH, D_MODEL, D_MODEL), D_MODEL ** -0.5),
        'b_ple_gate': nrm(ks[26], (DEPTH, D_MODEL), 0.01),
        'w_ple_proj': nrm(ks[27], (DEPTH, D_PLE, D_MODEL), D_PLE ** -0.5),
        'ple_norm_g': 1.0 + nrm(ks[28], (DEPTH, D_MODEL), 0.02),
        'final_norm_g': 1.0 + nrm(ks[29], (D_MODEL,), 0.02),
    }


def reference(x_prompt, x_sample, state_mlstm_C, state_mlstm_n, state_mlstm_m, state_conv,
              p_prompt, p_sample, norm_in_g, w_in, ln_v_g, ln_v_b, w_spatial, b_spatial,
              conv_w, conv_b, w_q, w_k, w_v, w_if, b_if, gn_g, skip, w_out, w_ple_gate,
              b_ple_gate, w_ple_proj, ple_norm_g, final_norm_g):
    xp, xs = x_prompt, x_sample
    bp = x_prompt.shape[0]
    Cp_l, np_l, mp_l, cp_l = [], [], [], []
    Cs_l, ns_l, ms_l, cs_l, vs_l = [], [], [], [], []
    for i in range(DEPTH):
        lw = (norm_in_g[i], w_in[i], ln_v_g[i], ln_v_b[i], w_spatial[i], b_spatial[i], conv_w[i],
              conv_b[i], w_q[i], w_k[i], w_v[i], w_if[i], b_if[i], gn_g[i], skip[i], w_out[i],
              w_ple_gate[i], b_ple_gate[i], w_ple_proj[i], ple_norm_g[i])
        C0 = jnp.zeros((bp, H_B, DH_B, DH_B), jnp.float32)
        n0 = jnp.zeros((bp, H_B, DH_B), jnp.float32)
        m0 = jnp.zeros((bp, H_B), jnp.float32)
        cb0 = jnp.zeros((bp, CONV_W - 1, W_B), xp.dtype)
        xp, Cp, np_, mp, cp, _ = mixer_layer(xp, p_prompt[i], *lw, C0, n0, m0, cb0)
        xs, Cs, ns, ms, cs, vs = mixer_layer(xs, p_sample[i], *lw, state_mlstm_C[i], state_mlstm_n[i],
                                             state_mlstm_m[i], state_conv[i])
        Cp_l.append(Cp); np_l.append(np_); mp_l.append(mp); cp_l.append(cp)
        Cs_l.append(Cs); ns_l.append(ns); ms_l.append(ms); cs_l.append(cs); vs_l.append(vs)
    y_prompt = rmsnorm(xp, final_norm_g)
    y_sample = rmsnorm(xs, final_norm_g)
    return (y_prompt, y_sample,
            jnp.stack(Cp_l), jnp.stack(np_l), jnp.stack(mp_l), jnp.stack(cp_l),
            jnp.stack(Cs_l), jnp.stack(ns_l), jnp.stack(ms_l), jnp.stack(cs_l), jnp.stack(vs_l))
```

```python
import functools

import jax
import jax.numpy as jnp
from jax import lax
from jax.experimental import pallas as pl
from jax.experimental.pallas import tpu as pltpu

D_MODEL = 1024
W_A = 1024
H_A = 8
DH_A = 128
W_B = 1024
H_B = 4
DH_B = 256
CHUNK = 128
CONV_W = 4
D_PLE = 256
N_IN = 3 * W_A + 3 * W_B
EPS = 1e-6

LANES = 128
SUBLANES = 8
PROMPT_TILE = 256
STATE_ROWS = 8
PROMPT_VMEM_BYTES = 56 * 1024 * 1024
SAMPLE_VMEM_BYTES = 48 * 1024 * 1024

F32 = jnp.float32
BF16 = jnp.bfloat16
NT_DIMS = (((1,), (1,)), ((), ()))
TN_DIMS = (((0,), (0,)), ((), ()))


def _rmsnorm(x, g):
    return x * lax.rsqrt(jnp.mean(x * x, axis=-1, keepdims=True) + EPS) * g


def _layernorm(x, g, b=None):
    mu = jnp.mean(x, axis=-1, keepdims=True)
    d = x - mu
    var = jnp.mean(d * d, axis=-1, keepdims=True)
    y = d * lax.rsqrt(var + EPS) * g
    return y if b is None else y + b


def _gelu(x):
    return 0.5 * x * (1.0 + lax.erf(x * (0.5 ** 0.5)))


def _silu(x):
    return x * jax.nn.sigmoid(x)


def _dot(a, b):
    return jnp.dot(a, b, preferred_element_type=F32)


def _prompt_kernel(x_ref, p_ref, norm_g_ref, w_in_ref, lnv_g_ref, lnv_b_ref, w_sp_ref, b_sp_ref,
                   conv_w_ref, conv_b_ref, w_q_ref, w_k_ref, w_v_ref, w_if_ref, b_if_ref,
                   gn_g_ref, skip_ref, w_out_ref, w_pg_ref, b_pg_ref, w_pp_ref, ple_g_ref, fin_g_ref,
                   y_ref, c_ref, n_ref, m_ref, conv_ref,
                   xpad_ref, qkv_ref, ycat_ref):
    tt = x_ref.shape[0]
    n_chunks = tt // CHUNK

    @pl.when(pl.program_id(1) == 0)
    def _():
        c_ref[...] = jnp.zeros_like(c_ref)
        n_ref[...] = jnp.zeros_like(n_ref)
        m_ref[...] = jnp.zeros_like(m_ref)
        conv_ref[...] = jnp.zeros_like(conv_ref)

    x = x_ref[...]
    hb = _rmsnorm(x, norm_g_ref[...]).astype(BF16)

    def proj(i):
        return _dot(hb, w_in_ref[:, i * 1024:(i + 1) * 1024])

    row = lax.broadcasted_iota(jnp.int32, (CHUNK, CHUNK), 0)
    col = lax.broadcasted_iota(jnp.int32, (CHUNK, CHUNK), 1)
    causal = row >= col
    tril = causal.astype(F32)

    u = _gelu(proj(0))
    v = _gelu(proj(1))
    z_a = proj(2)
    for hh in range(H_A):
        sl = slice(hh * DH_A, (hh + 1) * DH_A)
        vn = _layernorm(v[:, sl], lnv_g_ref[:, sl], lnv_b_ref[:, sl]).astype(BF16)
        wm = (w_sp_ref[hh] * tril).astype(BF16)
        for c in range(n_chunks):
            rows = slice(c * CHUNK, (c + 1) * CHUNK)
            sp = _dot(wm, vn[rows]) + b_sp_ref[:, sl]
            ya = u[rows, sl] * sp * _silu(z_a[rows, sl])
            ycat_ref[rows, sl] = ya.astype(BF16)

    x_b = proj(3)
    o_b = proj(4)
    z_b = proj(5)
    xpad_ref[SUBLANES - 3:SUBLANES, :] = conv_ref[...]
    xpad_ref[SUBLANES:SUBLANES + tt, :] = x_b
    conv_ref[...] = x_b[tt - 3:tt, :]
    xc = xpad_ref[SUBLANES - 3:SUBLANES - 3 + tt, :] * conv_w_ref[0:1, :]
    xc = xc + xpad_ref[SUBLANES - 2:SUBLANES - 2 + tt, :] * conv_w_ref[1:2, :]
    xc = xc + xpad_ref[SUBLANES - 1:SUBLANES - 1 + tt, :] * conv_w_ref[2:3, :]
    xc = xc + x_b * conv_w_ref[3:4, :]
    xc = _silu(xc + conv_b_ref[...])
    xcb = xc.astype(BF16)
    xbb = x_b.astype(BF16)
    for hh in range(H_B):
        sl = slice(hh * DH_B, (hh + 1) * DH_B)
        qkv_ref[:, hh * DH_B:(hh + 1) * DH_B] = _dot(xcb[:, sl], w_q_ref[hh]).astype(BF16)
        qkv_ref[:, W_B + hh * DH_B:W_B + (hh + 1) * DH_B] = _dot(xcb[:, sl], w_k_ref[hh]).astype(BF16)
        qkv_ref[:, 2 * W_B + hh * DH_B:2 * W_B + (hh + 1) * DH_B] = _dot(xbb[:, sl], w_v_ref[hh]).astype(BF16)
    gates = _dot(qkv_ref[...], w_if_ref[...]) + b_if_ref[...]
    logf = jax.nn.log_sigmoid(gates)

    for c in range(n_chunks):
        rows = slice(c * CHUNK, (c + 1) * CHUNK)
        bcum = jnp.dot(tril, logf[rows], precision=lax.Precision.HIGHEST, preferred_element_type=F32)
        bcum = pltpu.roll(bcum, LANES - H_B, 1)
        a_all = gates[rows] - bcum
        a_t = a_all.T
        for hh in range(H_B):
            sl = slice(hh * DH_B, (hh + 1) * DH_B)
            qh = qkv_ref[rows, hh * DH_B:(hh + 1) * DH_B]
            kh = qkv_ref[rows, W_B + hh * DH_B:W_B + (hh + 1) * DH_B]
            vh = qkv_ref[rows, 2 * W_B + hh * DH_B:2 * W_B + (hh + 1) * DH_B]
            m_prev = m_ref[hh, 0:1, 0:1]
            a_mat = jnp.where(causal, a_t[hh:hh + 1, :], -jnp.inf)
            m_run = jnp.maximum(jnp.max(a_mat, axis=-1, keepdims=True), m_prev)
            s = jnp.exp(a_mat - m_run) * lax.dot_general(qh, kh, NT_DIMS, preferred_element_type=F32)
            g = jnp.exp(m_prev - m_run)
            c_old = c_ref[hh]
            inter = lax.dot_general(qh, c_old.astype(BF16), NT_DIMS, preferred_element_type=F32)
            num = _dot(s.astype(BF16), vh) + g * inter
            n_old = n_ref[hh:hh + 1, :]
            den = jnp.sum(s, axis=-1, keepdims=True) + g * jnp.sum(qh.astype(F32) * n_old, axis=-1, keepdims=True)
            m_t = bcum[:, hh:hh + 1] + m_run
            hcell = num / jnp.maximum(jnp.abs(den), jnp.exp(-m_t))
            m_last = m_run[CHUNK - 1:CHUNK, :]
            w_end = jnp.exp(a_all[:, hh:hh + 1] - m_last)
            g_end = jnp.exp(m_prev - m_last)
            kw = kh.astype(F32) * w_end
            c_ref[hh] = g_end * c_old + lax.dot_general(vh, kw.astype(BF16), TN_DIMS, preferred_element_type=F32)
            n_ref[hh:hh + 1, :] = g_end * n_old + jnp.sum(kw, axis=0, keepdims=True)
            m_ref[hh] = jnp.broadcast_to(m_t[CHUNK - 1:CHUNK, :], (SUBLANES, LANES))
            hn = _layernorm(hcell * jax.nn.sigmoid(o_b[rows, sl]), gn_g_ref[:, sl])
            yb = (hn + skip_ref[:, sl] * xc[rows, sl]) * _silu(z_b[rows, sl])
            ycat_ref[rows, W_A + hh * DH_B:W_A + (hh + 1) * DH_B] = yb.astype(BF16)

    x1 = x + _dot(ycat_ref[...], w_out_ref[...])
    e = _rmsnorm(_dot(p_ref[...].astype(BF16), w_pp_ref[...]), ple_g_ref[...])
    gate = jax.nn.sigmoid(_dot(x1.astype(BF16), w_pg_ref[...]) + b_pg_ref[...])
    x2 = x1 + gate * e
    y_ref[...] = _rmsnorm(x2, fin_g_ref[...])


def _resident(shape):
    nd = len(shape)
    return pl.BlockSpec(shape, lambda *_: (0,) * nd, pipeline_mode=pl.Buffered(1))


def _prompt_call(x, p, weights):
    batch, seq, _ = x.shape
    tt = PROMPT_TILE
    grid = (batch, seq // tt)
    in_specs = [
        pl.BlockSpec((None, tt, D_MODEL), lambda b, t: (b, t, 0)),
        pl.BlockSpec((None, tt, D_PLE), lambda b, t: (b, t, 0)),
    ] + [_resident(w.shape) for w in weights]
    out_shape = (
        jax.ShapeDtypeStruct((batch, seq, D_MODEL), F32),
        jax.ShapeDtypeStruct((batch, H_B, DH_B, DH_B), F32),
        jax.ShapeDtypeStruct((batch, H_B, DH_B), F32),
        jax.ShapeDtypeStruct((batch, H_B, SUBLANES, LANES), F32),
        jax.ShapeDtypeStruct((batch, CONV_W - 1, W_B), F32),
    )
    out_specs = (
        pl.BlockSpec((None, tt, D_MODEL), lambda b, t: (b, t, 0)),
        pl.BlockSpec((None, H_B, DH_B, DH_B), lambda b, t: (b, 0, 0, 0)),
        pl.BlockSpec((None, H_B, DH_B), lambda b, t: (b, 0, 0)),
        pl.BlockSpec((None, H_B, SUBLANES, LANES), lambda b, t: (b, 0, 0, 0)),
        pl.BlockSpec((None, CONV_W - 1, W_B), lambda b, t: (b, 0, 0)),
    )
    scratch = [
        pltpu.VMEM((tt + SUBLANES, W_B), F32),
        pltpu.VMEM((tt, 3 * W_B), BF16),
        pltpu.VMEM((tt, W_A + W_B), BF16),
    ]
    return pl.pallas_call(
        _prompt_kernel,
        grid=grid,
        in_specs=in_specs,
        out_specs=out_specs,
        out_shape=out_shape,
        scratch_shapes=scratch,
        compiler_params=pltpu.CompilerParams(
            dimension_semantics=("arbitrary", "arbitrary"),
            vmem_limit_bytes=PROMPT_VMEM_BYTES),
        name="prompt_fused",
    )(x, p, *weights)


def _sample_pre_kernel(x_ref, conv_ref, n_ref, m_ref, norm_g_ref, w_in_ref, lnv_g_ref, lnv_b_ref,
                       w00_ref, b0_ref, conv_w_ref, conv_b_ref, w_q_ref, w_k_ref, w_v_ref, w_if_ref,
                       b_if_ref, skip_ref,
                       vn_ref, ya_ref, convnew_ref, q_ref, k_ref, vs_ref, gfull_ref, numa_ref,
                       denom_ref, osig_ref, skipxc_ref, zsilu_ref, nnew_ref, mnew_ref,
                       qkv_ref):
    x = x_ref[...]
    hb = _rmsnorm(x, norm_g_ref[...]).astype(BF16)

    def proj(i):
        return _dot(hb, w_in_ref[:, i * 1024:(i + 1) * 1024])

    u = _gelu(proj(0))
    v = _gelu(proj(1))
    z_a = proj(2)
    for hh in range(H_A):
        sl = slice(hh * DH_A, (hh + 1) * DH_A)
        vn = _layernorm(v[:, sl], lnv_g_ref[:, sl], lnv_b_ref[:, sl])
        vn_ref[:, sl] = vn
        sp = vn * w00_ref[:, sl] + b0_ref[:, sl]
        ya_ref[:, sl] = (u[:, sl] * sp * _silu(z_a[:, sl])).astype(BF16)

    x_b = proj(3)
    osig_ref[...] = jax.nn.sigmoid(proj(4))
    zsilu_ref[...] = _silu(proj(5))
    c0 = conv_ref[:, 0:W_B]
    c1 = conv_ref[:, W_B:2 * W_B]
    c2 = conv_ref[:, 2 * W_B:3 * W_B]
    xc = c0 * conv_w_ref[0:1, :]
    xc = xc + c1 * conv_w_ref[1:2, :]
    xc = xc + c2 * conv_w_ref[2:3, :]
    xc = xc + x_b * conv_w_ref[3:4, :]
    xc = _silu(xc + conv_b_ref[...])
    convnew_ref[:, 0:W_B] = c1
    convnew_ref[:, W_B:2 * W_B] = c2
    convnew_ref[:, 2 * W_B:3 * W_B] = x_b
    skipxc_ref[...] = skip_ref[...] * xc
    xcb = xc.astype(BF16)
    xbb = x_b.astype(BF16)
    for hh in range(H_B):
        sl = slice(hh * DH_B, (hh + 1) * DH_B)
        qkv_ref[:, hh * DH_B:(hh + 1) * DH_B] = _dot(xcb[:, sl], w_q_ref[hh]).astype(BF16)
        qkv_ref[:, W_B + hh * DH_B:W_B + (hh + 1) * DH_B] = _dot(xcb[:, sl], w_k_ref[hh]).astype(BF16)
        qkv_ref[:, 2 * W_B + hh * DH_B:2 * W_B + (hh + 1) * DH_B] = _dot(xbb[:, sl], w_v_ref[hh]).astype(BF16)
    gates = _dot(qkv_ref[...], w_if_ref[...]) + b_if_ref[...]
    logf = pltpu.roll(jax.nn.log_sigmoid(gates), LANES - H_B, 1)
    li = gates[:, 0:H_B]
    inter = logf[:, 0:H_B] + m_ref[...]
    m_t = jnp.maximum(inter, li)
    s_w = jnp.exp(li - m_t)
    g = jnp.exp(inter - m_t)
    e_neg = jnp.exp(-m_t)
    mnew_ref[...] = m_t
    for hh in range(H_B):
        sl = slice(hh * DH_B, (hh + 1) * DH_B)
        qf = qkv_ref[:, hh * DH_B:(hh + 1) * DH_B].astype(F32)
        kf = qkv_ref[:, W_B + hh * DH_B:W_B + (hh + 1) * DH_B].astype(F32)
        vf = qkv_ref[:, 2 * W_B + hh * DH_B:2 * W_B + (hh + 1) * DH_B].astype(F32)
        s_h = s_w[:, hh:hh + 1]
        g_h = g[:, hh:hh + 1]
        n_old = n_ref[:, sl]
        s_qk = s_h * jnp.sum(qf * kf, axis=-1, keepdims=True)
        den = s_qk + g_h * jnp.sum(n_old * qf, axis=-1, keepdims=True)
        denom = jnp.maximum(jnp.abs(den), e_neg[:, hh:hh + 1])
        q_ref[:, sl] = qf
        k_ref[:, sl] = kf
        vs_ref[:, sl] = s_h * vf
        gfull_ref[:, sl] = jnp.broadcast_to(g_h, qf.shape)
        numa_ref[:, sl] = s_qk * vf
        denom_ref[:, sl] = jnp.broadcast_to(denom, qf.shape)
        nnew_ref[:, sl] = g_h * n_old + s_h * kf


def _sample_state_kernel(c_ref, q_ref, k_ref, vs_ref, g_ref, cnew_ref, cq_ref):
    rowi = lax.broadcasted_iota(jnp.int32, (STATE_ROWS, DH_B), 0)
    for hh in range(H_B):
        sl = slice(hh * DH_B, (hh + 1) * DH_B)
        qh = q_ref[:, sl].astype(BF16)
        kh = k_ref[:, sl]
        vsh = vs_ref[:, sl].astype(BF16)
        acc = jnp.zeros((STATE_ROWS, DH_B), F32)
        for j in range(STATE_ROWS):
            c_old = c_ref[j, hh]
            r = lax.dot_general(qh, c_old.astype(BF16), NT_DIMS, preferred_element_type=F32)
            acc = jnp.where(rowi == j, r, acc)
            k_j = jnp.where(rowi == j, kh, 0.0).astype(BF16)
            outer = lax.dot_general(vsh, k_j, TN_DIMS, preferred_element_type=F32)
            cnew_ref[j, hh] = g_ref[j:j + 1, sl] * c_old + outer
        cq_ref[:, sl] = acc


def _sample_post_kernel(x_ref, p_ref, ya_ref, cq_ref, gfull_ref, numa_ref, denom_ref, osig_ref,
                        skipxc_ref, zsilu_ref, gn_g_ref, w_out_ref, w_pg_ref, b_pg_ref, w_pp_ref,
                        ple_g_ref, fin_g_ref, y_ref, ycat_ref):
    ycat_ref[:, 0:W_A] = ya_ref[...]
    for hh in range(H_B):
        sl = slice(hh * DH_B, (hh + 1) * DH_B)
        hcell = (numa_ref[:, sl] + gfull_ref[:, sl] * cq_ref[:, sl]) / denom_ref[:, sl]
        hn = _layernorm(hcell * osig_ref[:, sl], gn_g_ref[:, sl])
        ycat_ref[:, W_A + hh * DH_B:W_A + (hh + 1) * DH_B] = ((hn + skipxc_ref[:, sl]) * zsilu_ref[:, sl]).astype(BF16)
    x1 = x_ref[...] + _dot(ycat_ref[...], w_out_ref[...])
    e = _rmsnorm(_dot(p_ref[...].astype(BF16), w_pp_ref[...]), ple_g_ref[...])
    gate = jax.nn.sigmoid(_dot(x1.astype(BF16), w_pg_ref[...]) + b_pg_ref[...])
    x2 = x1 + gate * e
    y_ref[...] = _rmsnorm(x2, fin_g_ref[...])


def _whole(a):
    nd = a.ndim
    return pl.BlockSpec(a.shape, lambda *_: (0,) * nd)


def _single_step_call(body, inputs, out_shape, scratch, name):
    return pl.pallas_call(
        body,
        grid=(1,),
        in_specs=[_resident(a.shape) for a in inputs],
        out_specs=tuple(pl.BlockSpec(s.shape, lambda *_, nd=len(s.shape): (0,) * nd) for s in out_shape),
        out_shape=out_shape,
        scratch_shapes=scratch,
        compiler_params=pltpu.CompilerParams(
            dimension_semantics=("arbitrary",), vmem_limit_bytes=SAMPLE_VMEM_BYTES),
        name=name,
    )(*inputs)


def _sample_state_call(c, q, k, vs, gfull):
    nb = c.shape[0]
    rows = pl.BlockSpec((STATE_ROWS, W_B), lambda i: (i, 0))
    cblk = pl.BlockSpec((STATE_ROWS, H_B, DH_B, DH_B), lambda i: (i, 0, 0, 0))
    return pl.pallas_call(
        _sample_state_kernel,
        grid=(nb // STATE_ROWS,),
        in_specs=[cblk, rows, rows, rows, rows],
        out_specs=(cblk, rows),
        out_shape=(jax.ShapeDtypeStruct(c.shape, F32), jax.ShapeDtypeStruct((nb, W_B), F32)),
        compiler_params=pltpu.CompilerParams(
            dimension_semantics=("arbitrary",), vmem_limit_bytes=SAMPLE_VMEM_BYTES),
        name="sample_state",
    )(c, q, k, vs, gfull)


def kernel(x_prompt, x_sample, state_mlstm_C, state_mlstm_n, state_mlstm_m, state_conv, p_prompt, p_sample,
           norm_in_g, w_in, ln_v_g, ln_v_b, w_spatial, b_spatial, conv_w, conv_b, w_q, w_k, w_v, w_if, b_if,
           gn_g, skip, w_out, w_ple_gate, b_ple_gate, w_ple_proj, ple_norm_g, final_norm_g):
    assert norm_in_g.shape[0] == 1, "single-layer trunk"
    batch, seq, _ = x_prompt.shape
    nb = x_sample.shape[0]
    row = lambda a: a.reshape(1, -1).astype(F32)

    norm_g = row(norm_in_g[0])
    w_in_b = w_in[0].astype(BF16)
    lnv_g = row(ln_v_g[0])
    lnv_b = row(ln_v_b[0])
    w_sp = w_spatial[0]
    b_sp_full = jnp.repeat(b_spatial[0].T, DH_A, axis=1)
    cw = conv_w[0]
    cb = row(conv_b[0])
    wq_b = w_q[0].astype(BF16)
    wk_b = (w_k[0] * (DH_B ** -0.5)).astype(BF16)
    wv_b = w_v[0].astype(BF16)
    k_fix = jnp.concatenate([jnp.ones((W_B, 1), F32), jnp.full((W_B, 1), DH_B ** 0.5, F32), jnp.ones((W_B, 1), F32)], 0)
    w_if_p = jnp.pad(w_if[0] * k_fix, ((0, 0), (0, LANES - 2 * H_B))).astype(BF16)
    b_if_p = jnp.pad(b_if[0], (0, LANES - 2 * H_B)).reshape(1, LANES)
    gn = row(gn_g[0])
    sk = row(skip[0])
    w_out_b = w_out[0].astype(BF16)
    w_pg_b = w_ple_gate[0].astype(BF16)
    b_pg = row(b_ple_gate[0])
    w_pp_b = w_ple_proj[0].astype(BF16)
    ple_g = row(ple_norm_g[0])
    fin_g = row(final_norm_g)

    prompt_weights = [norm_g, w_in_b, lnv_g, lnv_b, w_sp, b_sp_full, cw, cb, wq_b, wk_b, wv_b, w_if_p, b_if_p,
                      gn, sk, w_out_b, w_pg_b, b_pg, w_pp_b, ple_g, fin_g]
    y_p, c_p, n_p, m_p, conv_p = _prompt_call(x_prompt, p_prompt[0], prompt_weights)

    xs = x_sample.reshape(nb, D_MODEL)
    ps = p_sample[0].reshape(nb, D_PLE)
    conv_s = state_conv[0].reshape(nb, (CONV_W - 1) * W_B)
    n_s = state_mlstm_n[0].reshape(nb, W_B)
    m_s = state_mlstm_m[0]
    w00 = jnp.repeat(w_sp[:, 0, 0], DH_A).reshape(1, W_A)
    b0 = jnp.repeat(b_spatial[0][:, 0], DH_A).reshape(1, W_A)
    full = jax.ShapeDtypeStruct((nb, W_B), F32)
    pre_out = (
        jax.ShapeDtypeStruct((nb, W_A), F32),
        jax.ShapeDtypeStruct((nb, W_A), BF16),
        jax.ShapeDtypeStruct((nb, (CONV_W - 1) * W_B), F32),
        full, full, full, full, full, full, full, full, full, full,
        jax.ShapeDtypeStruct((nb, H_B), F32),
    )
    (vn_s, ya_s, conv_new, q_s, k_s, vs_s, gfull, numa, denom, osig, skipxc, zsilu, n_new, m_new) = _single_step_call(
        _sample_pre_kernel,
        [xs, conv_s, n_s, m_s, norm_g, w_in_b, lnv_g, lnv_b, w00, b0, cw, cb, wq_b, wk_b, wv_b, w_if_p, b_if_p, sk],
        pre_out, [pltpu.VMEM((nb, 3 * W_B), BF16)], "sample_pre")
    c_new, cq = _sample_state_call(state_mlstm_C[0], q_s, k_s, vs_s, gfull)
    (y_s,) = _single_step_call(
        _sample_post_kernel,
        [xs, ps, ya_s, cq, gfull, numa, denom, osig, skipxc, zsilu, gn, w_out_b, w_pg_b, b_pg, w_pp_b, ple_g, fin_g],
        (jax.ShapeDtypeStruct((nb, D_MODEL), F32),), [pltpu.VMEM((nb, W_A + W_B), BF16)], "sample_post")

    return (
        y_p,
        y_s.reshape(nb, 1, D_MODEL),
        c_p[None],
        n_p[None],
        m_p[:, :, 0, 0][None],
        conv_p[None],
        c_new[None],
        n_new.reshape(1, nb, H_B, DH_B),
        m_new[None],
        conv_new.reshape(1, nb, CONV_W - 1, W_B),
        vn_s.reshape(1, nb, 1, W_A),
    )
```

```python
import jax
import jax.numpy as jnp
from jax import lax
from jax.experimental import pallas as pl
from jax.experimental.pallas import tpu as pltpu

D_MODEL = 1024
W_A = 1024
H_A = 8
DH_A = 128
W_B = 1024
H_B = 4
DH_B = 256
CHUNK = 128
CONV_W = 4
D_PLE = 256
N_IN = 3 * W_A + 3 * W_B
EPS = 1e-6

LANES = 128
SUBLANES = 8
PROMPT_TILE = 256
STATE_ROWS = 8
PROMPT_VMEM_BYTES = 56 * 1024 * 1024
SAMPLE_VMEM_BYTES = 48 * 1024 * 1024

F32 = jnp.float32
BF16 = jnp.bfloat16
NT_DIMS = (((1,), (1,)), ((), ()))
TN_DIMS = (((0,), (0,)), ((), ()))


def _rmsnorm(x, g):
    return x * lax.rsqrt(jnp.mean(x * x, axis=-1, keepdims=True) + EPS) * g


def _layernorm(x, g, b=None):
    mu = jnp.mean(x, axis=-1, keepdims=True)
    d = x - mu
    var = jnp.mean(d * d, axis=-1, keepdims=True)
    y = d * lax.rsqrt(var + EPS) * g
    return y if b is None else y + b


def _gelu(x):
    return 0.5 * x * (1.0 + lax.erf(x * (0.5 ** 0.5)))


def _silu(x):
    return x * jax.nn.sigmoid(x)


def _dot(a, b):
    return jnp.dot(a, b, preferred_element_type=F32)


def _cumsum_rows(tril_b, x):
    x1 = x.astype(BF16)
    r1 = x - x1.astype(F32)
    x2 = r1.astype(BF16)
    x3 = (r1 - x2.astype(F32)).astype(BF16)
    return _dot(tril_b, x1) + _dot(tril_b, x2) + _dot(tril_b, x3)


def _prompt_kernel(x_ref, p_ref, norm_g_ref, w_in_ref, lnv_g_ref, lnv_b_ref, w_sp_ref, b_sp_ref,
                   conv_w_ref, conv_b_ref, w_q_ref, w_k_ref, w_v_ref, w_if_ref, b_if_ref,
                   gn_g_ref, skip_ref, w_out_ref, w_pg_ref, b_pg_ref, w_pp_ref, ple_g_ref, fin_g_ref,
                   y_ref, c_ref, n_ref, m_ref, conv_ref,
                   xpad_ref, qkv_ref, ycat_ref):
    tt = x_ref.shape[0]
    n_chunks = tt // CHUNK

    @pl.when(pl.program_id(1) == 0)
    def _():
        c_ref[...] = jnp.zeros_like(c_ref)
        n_ref[...] = jnp.zeros_like(n_ref)
        m_ref[...] = jnp.zeros_like(m_ref)
        conv_ref[...] = jnp.zeros_like(conv_ref)

    x = x_ref[...]
    hb = _rmsnorm(x, norm_g_ref[...]).astype(BF16)

    def proj(i):
        return _dot(hb, w_in_ref[i])

    row = lax.broadcasted_iota(jnp.int32, (CHUNK, CHUNK), 0)
    col = lax.broadcasted_iota(jnp.int32, (CHUNK, CHUNK), 1)
    causal = row >= col
    tril = causal.astype(F32)
    tril_b = causal.astype(BF16)

    u = _gelu(proj(0))
    v = _gelu(proj(1))
    z_a = proj(2)
    for hh in range(H_A):
        sl = slice(hh * DH_A, (hh + 1) * DH_A)
        vn = _layernorm(v[:, sl], lnv_g_ref[:, sl], lnv_b_ref[:, sl]).astype(BF16)
        wm = (w_sp_ref[hh] * tril).astype(BF16)
        vn_wide = jnp.concatenate([vn[c * CHUNK:(c + 1) * CHUNK] for c in range(n_chunks)], axis=1)
        sp_wide = _dot(wm, vn_wide)
        for c in range(n_chunks):
            rows = slice(c * CHUNK, (c + 1) * CHUNK)
            sp = sp_wide[:, c * DH_A:(c + 1) * DH_A] + b_sp_ref[:, sl]
            ycat_ref[rows, sl] = (u[rows, sl] * sp * _silu(z_a[rows, sl])).astype(BF16)

    x_b = proj(3)
    xpad_ref[SUBLANES - 3:SUBLANES, :] = conv_ref[...]
    xpad_ref[SUBLANES:SUBLANES + tt, :] = x_b
    conv_ref[...] = x_b[tt - 3:tt, :]
    xc = xpad_ref[SUBLANES - 3:SUBLANES - 3 + tt, :] * conv_w_ref[0:1, :]
    xc = xc + xpad_ref[SUBLANES - 2:SUBLANES - 2 + tt, :] * conv_w_ref[1:2, :]
    xc = xc + xpad_ref[SUBLANES - 1:SUBLANES - 1 + tt, :] * conv_w_ref[2:3, :]
    xc = xc + x_b * conv_w_ref[3:4, :]
    xc = _silu(xc + conv_b_ref[...])
    o_sig = jax.nn.sigmoid(proj(4))
    z_silu = _silu(proj(5))
    xcb = xc.astype(BF16)
    xbb = x_b.astype(BF16)
    for hh in range(H_B):
        sl = slice(hh * DH_B, (hh + 1) * DH_B)
        qkv_ref[:, hh * DH_B:(hh + 1) * DH_B] = _dot(xcb[:, sl], w_q_ref[hh]).astype(BF16)
        qkv_ref[:, W_B + hh * DH_B:W_B + (hh + 1) * DH_B] = _dot(xcb[:, sl], w_k_ref[hh]).astype(BF16)
        qkv_ref[:, 2 * W_B + hh * DH_B:2 * W_B + (hh + 1) * DH_B] = _dot(xbb[:, sl], w_v_ref[hh]).astype(BF16)
    gates = _dot(qkv_ref[...], w_if_ref[...]) + b_if_ref[...]
    logf = jax.nn.log_sigmoid(gates)
    decay = []
    for c in range(n_chunks):
        rows = slice(c * CHUNK, (c + 1) * CHUNK)
        bcum = pltpu.roll(_cumsum_rows(tril_b, logf[rows]), LANES - H_B, 1)
        a_all = gates[rows] - bcum
        decay.append((bcum, a_all, a_all.T))

    def mlstm_unit(c, hh):
        rows = slice(c * CHUNK, (c + 1) * CHUNK)
        sl = slice(hh * DH_B, (hh + 1) * DH_B)
        bcum, a_all, a_t = decay[c]
        qh = qkv_ref[rows, hh * DH_B:(hh + 1) * DH_B]
        kh = qkv_ref[rows, W_B + hh * DH_B:W_B + (hh + 1) * DH_B]
        vh = qkv_ref[rows, 2 * W_B + hh * DH_B:2 * W_B + (hh + 1) * DH_B]
        m_prev = m_ref[hh, 0:1, 0:1]
        a_mat = jnp.where(causal, a_t[hh:hh + 1, :], -jnp.inf)
        m_run = jnp.maximum(jnp.max(a_mat, axis=-1, keepdims=True), m_prev)
        s = jnp.exp(a_mat - m_run) * lax.dot_general(qh, kh, NT_DIMS, preferred_element_type=F32)
        g = jnp.exp(m_prev - m_run)
        c_old = c_ref[hh]
        inter = lax.dot_general(qh, c_old.astype(BF16), NT_DIMS, preferred_element_type=F32)
        num = _dot(s.astype(BF16), vh) + g * inter
        n_old = n_ref[hh:hh + 1, :]
        den = jnp.sum(s, axis=-1, keepdims=True) + g * jnp.sum(qh.astype(F32) * n_old, axis=-1, keepdims=True)
        m_t = bcum[:, hh:hh + 1] + m_run
        hcell = num / jnp.maximum(jnp.abs(den), jnp.exp(-m_t))
        m_last = m_run[CHUNK - 1:CHUNK, :]
        w_end = jnp.exp(a_all[:, hh:hh + 1] - m_last)
        g_end = jnp.exp(m_prev - m_last)
        kw = kh.astype(F32) * w_end
        c_ref[hh] = g_end * c_old + lax.dot_general(vh, kw.astype(BF16), TN_DIMS, preferred_element_type=F32)
        n_ref[hh:hh + 1, :] = g_end * n_old + jnp.sum(kw, axis=0, keepdims=True)
        m_ref[hh] = jnp.broadcast_to(m_t[CHUNK - 1:CHUNK, :], (SUBLANES, LANES))
        hn = _layernorm(hcell * o_sig[rows, sl], gn_g_ref[:, sl])
        yb = (hn + skip_ref[:, sl] * xc[rows, sl]) * z_silu[rows, sl]
        ycat_ref[rows, W_A + hh * DH_B:W_A + (hh + 1) * DH_B] = yb.astype(BF16)

    e = _rmsnorm(_dot(p_ref[...].astype(BF16), w_pp_ref[...]), ple_g_ref[...])
    for c in range(n_chunks):
        for hh in range(H_B):
            mlstm_unit(c, hh)

    x1 = x + _dot(ycat_ref[...], w_out_ref[...])
    gate = jax.nn.sigmoid(_dot(x1.astype(BF16), w_pg_ref[...]) + b_pg_ref[...])
    x2 = x1 + gate * e
    y_ref[...] = _rmsnorm(x2, fin_g_ref[...])


def _resident(shape):
    nd = len(shape)
    return pl.BlockSpec(shape, lambda *_: (0,) * nd, pipeline_mode=pl.Buffered(1))


def _prompt_call(x, p, weights):
    batch, seq, _ = x.shape
    tt = PROMPT_TILE
    grid = (batch, seq // tt)
    in_specs = [
        pl.BlockSpec((None, tt, D_MODEL), lambda b, t: (b, t, 0)),
        pl.BlockSpec((None, tt, D_PLE), lambda b, t: (b, t, 0)),
    ] + [_resident(w.shape) for w in weights]
    out_shape = (
        jax.ShapeDtypeStruct((batch, seq, D_MODEL), F32),
        jax.ShapeDtypeStruct((batch, H_B, DH_B, DH_B), F32),
        jax.ShapeDtypeStruct((batch, H_B, DH_B), F32),
        jax.ShapeDtypeStruct((batch, H_B, SUBLANES, LANES), F32),
        jax.ShapeDtypeStruct((batch, CONV_W - 1, W_B), F32),
    )
    out_specs = (
        pl.BlockSpec((None, tt, D_MODEL), lambda b, t: (b, t, 0)),
        pl.BlockSpec((None, H_B, DH_B, DH_B), lambda b, t: (b, 0, 0, 0)),
        pl.BlockSpec((None, H_B, DH_B), lambda b, t: (b, 0, 0)),
        pl.BlockSpec((None, H_B, SUBLANES, LANES), lambda b, t: (b, 0, 0, 0)),
        pl.BlockSpec((None, CONV_W - 1, W_B), lambda b, t: (b, 0, 0)),
    )
    scratch = [
        pltpu.VMEM((tt + SUBLANES, W_B), F32),
        pltpu.VMEM((tt, 3 * W_B), BF16),
        pltpu.VMEM((tt, W_A + W_B), BF16),
    ]
    return pl.pallas_call(
        _prompt_kernel,
        grid=grid,
        in_specs=in_specs,
        out_specs=out_specs,
        out_shape=out_shape,
        scratch_shapes=scratch,
        compiler_params=pltpu.CompilerParams(
            dimension_semantics=("arbitrary", "arbitrary"),
            vmem_limit_bytes=PROMPT_VMEM_BYTES),
        name="prompt_fused",
    )(x, p, *weights)


def _sample_pre_kernel(x_ref, conv_ref, n_ref, m_ref, norm_g_ref, w_in_ref, lnv_g_ref, lnv_b_ref,
                       w00_ref, b0_ref, conv_w_ref, conv_b_ref, w_q_ref, w_k_ref, w_v_ref, w_if_ref,
                       b_if_ref, skip_ref,
                       vn_ref, ya_ref, convnew_ref, q_ref, k_ref, vs_ref, gfull_ref, numa_ref,
                       denom_ref, osig_ref, skipxc_ref, zsilu_ref, nnew_ref, mnew_ref,
                       qkv_ref):
    x = x_ref[:, 0, :]
    hb = _rmsnorm(x, norm_g_ref[...]).astype(BF16)

    def proj(i):
        return _dot(hb, w_in_ref[i])

    u = _gelu(proj(0))
    v = _gelu(proj(1))
    z_a = proj(2)
    for hh in range(H_A):
        sl = slice(hh * DH_A, (hh + 1) * DH_A)
        vn = _layernorm(v[:, sl], lnv_g_ref[:, sl], lnv_b_ref[:, sl])
        vn_ref[:, 0, sl] = vn
        sp = vn * w00_ref[:, sl] + b0_ref[:, sl]
        ya_ref[:, sl] = (u[:, sl] * sp * _silu(z_a[:, sl])).astype(BF16)

    x_b = proj(3)
    osig_ref[...] = jax.nn.sigmoid(proj(4))
    zsilu_ref[...] = _silu(proj(5))
    c0 = conv_ref[:, 0, :]
    c1 = conv_ref[:, 1, :]
    c2 = conv_ref[:, 2, :]
    xc = c0 * conv_w_ref[0:1, :]
    xc = xc + c1 * conv_w_ref[1:2, :]
    xc = xc + c2 * conv_w_ref[2:3, :]
    xc = xc + x_b * conv_w_ref[3:4, :]
    xc = _silu(xc + conv_b_ref[...])
    convnew_ref[:, 0, :] = c1
    convnew_ref[:, 1, :] = c2
    convnew_ref[:, 2, :] = x_b
    skipxc_ref[...] = skip_ref[...] * xc
    xcb = xc.astype(BF16)
    xbb = x_b.astype(BF16)
    for hh in range(H_B):
        sl = slice(hh * DH_B, (hh + 1) * DH_B)
        qkv_ref[:, hh * DH_B:(hh + 1) * DH_B] = _dot(xcb[:, sl], w_q_ref[hh]).astype(BF16)
        qkv_ref[:, W_B + hh * DH_B:W_B + (hh + 1) * DH_B] = _dot(xcb[:, sl], w_k_ref[hh]).astype(BF16)
        qkv_ref[:, 2 * W_B + hh * DH_B:2 * W_B + (hh + 1) * DH_B] = _dot(xbb[:, sl], w_v_ref[hh]).astype(BF16)
    gates = _dot(qkv_ref[...], w_if_ref[...]) + b_if_ref[...]
    logf = pltpu.roll(jax.nn.log_sigmoid(gates), LANES - H_B, 1)
    li = gates[:, 0:H_B]
    inter = logf[:, 0:H_B] + m_ref[...]
    m_t = jnp.maximum(inter, li)
    s_w = jnp.exp(li - m_t)
    g = jnp.exp(inter - m_t)
    e_neg = jnp.exp(-m_t)
    mnew_ref[...] = m_t
    for hh in range(H_B):
        sl = slice(hh * DH_B, (hh + 1) * DH_B)
        qf = qkv_ref[:, hh * DH_B:(hh + 1) * DH_B].astype(F32)
        kf = qkv_ref[:, W_B + hh * DH_B:W_B + (hh + 1) * DH_B].astype(F32)
        vf = qkv_ref[:, 2 * W_B + hh * DH_B:2 * W_B + (hh + 1) * DH_B].astype(F32)
        s_h = s_w[:, hh:hh + 1]
        g_h = g[:, hh:hh + 1]
        n_old = n_ref[:, hh, :]
        s_qk = s_h * jnp.sum(qf * kf, axis=-1, keepdims=True)
        den = s_qk + g_h * jnp.sum(n_old * qf, axis=-1, keepdims=True)
        denom = jnp.maximum(jnp.abs(den), e_neg[:, hh:hh + 1])
        q_ref[:, sl] = qf
        k_ref[:, sl] = kf
        vs_ref[:, sl] = s_h * vf
        gfull_ref[:, sl] = jnp.broadcast_to(g_h, qf.shape)
        numa_ref[:, sl] = s_qk * vf
        denom_ref[:, sl] = jnp.broadcast_to(denom, qf.shape)
        nnew_ref[:, hh, :] = g_h * n_old + s_h * kf


def _sample_state_kernel(c_ref, q_ref, k_ref, vs_ref, g_ref, cnew_ref, cq_ref):
    rowi = lax.broadcasted_iota(jnp.int32, (STATE_ROWS, DH_B), 0)
    for hh in range(H_B):
        sl = slice(hh * DH_B, (hh + 1) * DH_B)
        qh = q_ref[:, sl].astype(BF16)
        kh = k_ref[:, sl]
        vsh = vs_ref[:, sl].astype(BF16)
        acc = jnp.zeros((STATE_ROWS, DH_B), F32)
        for j in range(STATE_ROWS):
            c_old = c_ref[j, hh]
            r = lax.dot_general(qh, c_old.astype(BF16), NT_DIMS, preferred_element_type=F32)
            acc = jnp.where(rowi == j, r, acc)
            k_j = jnp.where(rowi == j, kh, 0.0).astype(BF16)
            outer = lax.dot_general(vsh, k_j, TN_DIMS, preferred_element_type=F32)
            cnew_ref[j, hh] = g_ref[j:j + 1, sl] * c_old + outer
        cq_ref[:, sl] = acc


def _sample_post_kernel(x_ref, p_ref, ya_ref, cq_ref, gfull_ref, numa_ref, denom_ref, osig_ref,
                        skipxc_ref, zsilu_ref, gn_g_ref, w_out_ref, w_pg_ref, b_pg_ref, w_pp_ref,
                        ple_g_ref, fin_g_ref, y_ref, ycat_ref):
    ycat_ref[:, 0:W_A] = ya_ref[...]
    for hh in range(H_B):
        sl = slice(hh * DH_B, (hh + 1) * DH_B)
        hcell = (numa_ref[:, sl] + gfull_ref[:, sl] * cq_ref[:, sl]) / denom_ref[:, sl]
        hn = _layernorm(hcell * osig_ref[:, sl], gn_g_ref[:, sl])
        ycat_ref[:, W_A + hh * DH_B:W_A + (hh + 1) * DH_B] = ((hn + skipxc_ref[:, sl]) * zsilu_ref[:, sl]).astype(BF16)
    x1 = x_ref[:, 0, :] + _dot(ycat_ref[...], w_out_ref[...])
    e = _rmsnorm(_dot(p_ref[:, 0, :].astype(BF16), w_pp_ref[...]), ple_g_ref[...])
    gate = jax.nn.sigmoid(_dot(x1.astype(BF16), w_pg_ref[...]) + b_pg_ref[...])
    x2 = x1 + gate * e
    y_ref[:, 0, :] = _rmsnorm(x2, fin_g_ref[...])


def _single_step_call(body, inputs, out_shape, scratch, name):
    return pl.pallas_call(
        body,
        grid=(1,),
        in_specs=[_resident(a.shape) for a in inputs],
        out_specs=tuple(pl.BlockSpec(s.shape, lambda *_, nd=len(s.shape): (0,) * nd) for s in out_shape),
        out_shape=out_shape,
        scratch_shapes=scratch,
        compiler_params=pltpu.CompilerParams(
            dimension_semantics=("arbitrary",), vmem_limit_bytes=SAMPLE_VMEM_BYTES),
        name=name,
    )(*inputs)


def _sample_state_call(c, q, k, vs, gfull):
    nb = c.shape[0]
    rows = pl.BlockSpec((STATE_ROWS, W_B), lambda i: (i, 0))
    cblk = pl.BlockSpec((STATE_ROWS, H_B, DH_B, DH_B), lambda i: (i, 0, 0, 0))
    return pl.pallas_call(
        _sample_state_kernel,
        grid=(nb // STATE_ROWS,),
        in_specs=[cblk, rows, rows, rows, rows],
        out_specs=(cblk, rows),
        out_shape=(jax.ShapeDtypeStruct(c.shape, F32), jax.ShapeDtypeStruct((nb, W_B), F32)),
        compiler_params=pltpu.CompilerParams(
            dimension_semantics=("arbitrary",), vmem_limit_bytes=SAMPLE_VMEM_BYTES),
        name="sample_state",
    )(c, q, k, vs, gfull)


def kernel(x_prompt, x_sample, state_mlstm_C, state_mlstm_n, state_mlstm_m, state_conv, p_prompt, p_sample,
           norm_in_g, w_in, ln_v_g, ln_v_b, w_spatial, b_spatial, conv_w, conv_b, w_q, w_k, w_v, w_if, b_if,
           gn_g, skip, w_out, w_ple_gate, b_ple_gate, w_ple_proj, ple_norm_g, final_norm_g):
    assert norm_in_g.shape[0] == 1, "single-layer trunk"
    batch, seq, _ = x_prompt.shape
    nb = x_sample.shape[0]
    row = lambda a: a.reshape(1, -1).astype(F32)

    norm_g = row(norm_in_g[0])
    w_in_b = w_in[0].reshape(D_MODEL, N_IN // 1024, 1024).transpose(1, 0, 2).astype(BF16)
    lnv_g = row(ln_v_g[0])
    lnv_b = row(ln_v_b[0])
    w_sp = w_spatial[0]
    b_sp_full = jnp.repeat(b_spatial[0].T, DH_A, axis=1)
    cw = conv_w[0]
    cb = row(conv_b[0])
    wq_b = w_q[0].astype(BF16)
    wk_b = (w_k[0] * (DH_B ** -0.5)).astype(BF16)
    wv_b = w_v[0].astype(BF16)
    k_fix = jnp.concatenate([jnp.ones((W_B, 1), F32), jnp.full((W_B, 1), DH_B ** 0.5, F32), jnp.ones((W_B, 1), F32)], 0)
    w_if_p = jnp.pad(w_if[0] * k_fix, ((0, 0), (0, LANES - 2 * H_B))).astype(BF16)
    b_if_p = jnp.pad(b_if[0], (0, LANES - 2 * H_B)).reshape(1, LANES)
    gn = row(gn_g[0])
    sk = row(skip[0])
    w_out_b = w_out[0].astype(BF16)
    w_pg_b = w_ple_gate[0].astype(BF16)
    b_pg = row(b_ple_gate[0])
    w_pp_b = w_ple_proj[0].astype(BF16)
    ple_g = row(ple_norm_g[0])
    fin_g = row(final_norm_g)

    prompt_weights = [norm_g, w_in_b, lnv_g, lnv_b, w_sp, b_sp_full, cw, cb, wq_b, wk_b, wv_b, w_if_p, b_if_p,
                      gn, sk, w_out_b, w_pg_b, b_pg, w_pp_b, ple_g, fin_g]
    y_p, c_p, n_p, m_p, conv_p = _prompt_call(x_prompt, p_prompt[0], prompt_weights)

    xs = x_sample
    ps = p_sample[0]
    conv_s = state_conv[0]
    n_s = state_mlstm_n[0]
    m_s = state_mlstm_m[0]
    w00 = jnp.repeat(w_sp[:, 0, 0], DH_A).reshape(1, W_A)
    b0 = jnp.repeat(b_spatial[0][:, 0], DH_A).reshape(1, W_A)
    full = jax.ShapeDtypeStruct((nb, W_B), F32)
    pre_out = (
        jax.ShapeDtypeStruct((nb, 1, W_A), F32),
        jax.ShapeDtypeStruct((nb, W_A), BF16),
        jax.ShapeDtypeStruct((nb, CONV_W - 1, W_B), F32),
        full, full, full, full, full, full, full, full, full,
        jax.ShapeDtypeStruct((nb, H_B, DH_B), F32),
        jax.ShapeDtypeStruct((nb, H_B), F32),
    )
    (vn_s, ya_s, conv_new, q_s, k_s, vs_s, gfull, numa, denom, osig, skipxc, zsilu, n_new, m_new) = _single_step_call(
        _sample_pre_kernel,
        [xs, conv_s, n_s, m_s, norm_g, w_in_b, lnv_g, lnv_b, w00, b0, cw, cb, wq_b, wk_b, wv_b, w_if_p, b_if_p, sk],
        pre_out, [pltpu.VMEM((nb, 3 * W_B), BF16)], "sample_pre")
    c_new, cq = _sample_state_call(state_mlstm_C[0], q_s, k_s, vs_s, gfull)
    (y_s,) = _single_step_call(
        _sample_post_kernel,
        [xs, ps, ya_s, cq, gfull, numa, denom, osig, skipxc, zsilu, gn, w_out_b, w_pg_b, b_pg, w_pp_b, ple_g, fin_g],
        (jax.ShapeDtypeStruct((nb, 1, D_MODEL), F32),), [pltpu.VMEM((nb, W_A + W_B), BF16)], "sample_post")

    return (y_p, y_s, c_p[None], n_p[None], m_p[:, :, 0, 0][None], conv_p[None],
            c_new[None], n_new[None], m_new[None], conv_new[None], vn_s[None])
```

```python
import jax
import jax.numpy as jnp
from jax import lax
from jax.experimental import pallas as pl
from jax.experimental.pallas import tpu as pltpu

D_MODEL = 1024
W_A = 1024
H_A = 8
DH_A = 128
W_B = 1024
H_B = 4
DH_B = 256
CHUNK = 128
CONV_W = 4
D_PLE = 256
N_IN = 3 * W_A + 3 * W_B
EPS = 1e-6

LANES = 128
SUBLANES = 8
PROMPT_TILE = 256
SEQS_PER_STEP = 1
PROMPT_VMEM_BYTES = 56 * 1024 * 1024
SAMPLE_VMEM_BYTES = 48 * 1024 * 1024

F32 = jnp.float32
BF16 = jnp.bfloat16
NT_DIMS = (((1,), (1,)), ((), ()))
TN_DIMS = (((0,), (0,)), ((), ()))


def _rmsnorm(x, g):
    return x * lax.rsqrt(jnp.mean(x * x, axis=-1, keepdims=True) + EPS) * g


def _layernorm(x, g, b=None):
    mu = jnp.mean(x, axis=-1, keepdims=True)
    d = x - mu
    var = jnp.mean(d * d, axis=-1, keepdims=True)
    y = d * lax.rsqrt(var + EPS) * g
    return y if b is None else y + b


def _gelu(x):
    return 0.5 * x * (1.0 + lax.erf(x * (0.5 ** 0.5)))


def _silu(x):
    return x * jax.nn.sigmoid(x)


def _dot(a, b):
    return jnp.dot(a, b, preferred_element_type=F32)


def _cumsum_rows(tril_b, x):
    x1 = x.astype(BF16)
    r1 = x - x1.astype(F32)
    x2 = r1.astype(BF16)
    x3 = (r1 - x2.astype(F32)).astype(BF16)
    return _dot(tril_b, x1) + _dot(tril_b, x2) + _dot(tril_b, x3)


def _prompt_kernel(x_ref, p_ref, norm_g_ref, w_in_ref, lnv_g_ref, lnv_b_ref, w_sp_ref, b_sp_ref,
                   conv_w_ref, conv_b_ref, w_q_ref, w_k_ref, w_v_ref, w_if_ref, b_if_ref,
                   gn_g_ref, skip_ref, w_out_ref, w_pg_ref, b_pg_ref, w_pp_ref, ple_g_ref, fin_g_ref,
                   sc_ref, sq_ref, sk_ref, svs_ref, sg_ref,
                   y_ref, c_ref, n_ref, m_ref, conv_ref, scnew_ref, scq_ref,
                   xpad_ref, qkv_ref, ycat_ref):
    n_seq, tt = x_ref.shape[0], x_ref.shape[1]
    n_chunks = tt // CHUNK
    step = pl.program_id(0) * pl.num_programs(1) + pl.program_id(1)

    @pl.when(pl.program_id(1) == 0)
    def _():
        c_ref[...] = jnp.zeros_like(c_ref)
        n_ref[...] = jnp.zeros_like(n_ref)
        m_ref[...] = jnp.zeros_like(m_ref)
        conv_ref[...] = jnp.zeros_like(conv_ref)

    @pl.when(step == 0)
    def _():
        scq_ref[...] = jnp.zeros_like(scq_ref)

    rows_per_step = sc_ref.shape[0]
    group_row = lax.broadcasted_iota(jnp.int32, (SUBLANES, DH_B), 0)

    def sample_state_pair(j, hh):
        sl = slice(hh * DH_B, (hh + 1) * DH_B)
        r = step * rows_per_step + j
        base = pl.multiple_of((r // SUBLANES) * SUBLANES, SUBLANES)
        is_row = group_row == (r - base)
        group = pl.ds(base, SUBLANES)
        c_old = sc_ref[j, hh]
        cq = lax.dot_general(sq_ref[group, sl].astype(BF16), c_old.astype(BF16), NT_DIMS, preferred_element_type=F32)
        scq_ref[group, sl] = jnp.where(is_row, cq, scq_ref[group, sl])
        k_row = jnp.where(is_row, sk_ref[group, sl], 0.0).astype(BF16)
        outer = lax.dot_general(svs_ref[group, sl].astype(BF16), k_row, TN_DIMS, preferred_element_type=F32)
        scnew_ref[j, hh] = sg_ref[pl.ds(r, 1), sl] * c_old + outer

    row = lax.broadcasted_iota(jnp.int32, (CHUNK, CHUNK), 0)
    col = lax.broadcasted_iota(jnp.int32, (CHUNK, CHUNK), 1)
    causal = row >= col
    tril = causal.astype(F32)
    tril_b = causal.astype(BF16)

    class Seq:
        pass

    seqs = [Seq() for _ in range(n_seq)]

    def proj(sq, i):
        return _dot(sq.hb, w_in_ref[:, i * W_A:(i + 1) * W_A])

    def stage_norm_u(sq, bi):
        sq.hb = _rmsnorm(x_ref[bi], norm_g_ref[...]).astype(BF16)
        sq.u = _gelu(proj(sq, 0))

    def stage_v(sq, bi):
        sq.v = _gelu(proj(sq, 1))

    def stage_za(sq, bi):
        sq.z_a = proj(sq, 2)

    def gmlp_heads(sq, bi, heads):
        for hh in heads:
            sl = slice(hh * DH_A, (hh + 1) * DH_A)
            vn = _layernorm(sq.v[:, sl], lnv_g_ref[:, sl], lnv_b_ref[:, sl]).astype(BF16)
            wm = (w_sp_ref[hh] * tril).astype(BF16)
            vn_wide = jnp.concatenate([vn[c * CHUNK:(c + 1) * CHUNK] for c in range(n_chunks)], axis=1)
            sp_wide = _dot(wm, vn_wide)
            for c in range(n_chunks):
                rows = slice(c * CHUNK, (c + 1) * CHUNK)
                sp = sp_wide[:, c * DH_A:(c + 1) * DH_A] + b_sp_ref[:, sl]
                ycat_ref[bi, rows, sl] = (sq.u[rows, sl] * sp * _silu(sq.z_a[rows, sl])).astype(BF16)

    def stage_conv(sq, bi):
        x_b = proj(sq, 3)
        xpad_ref[bi, SUBLANES - 3:SUBLANES, :] = conv_ref[bi]
        xpad_ref[bi, SUBLANES:SUBLANES + tt, :] = x_b
        conv_ref[bi] = x_b[tt - 3:tt, :]
        xc = xpad_ref[bi, SUBLANES - 3:SUBLANES - 3 + tt, :] * conv_w_ref[0:1, :]
        xc = xc + xpad_ref[bi, SUBLANES - 2:SUBLANES - 2 + tt, :] * conv_w_ref[1:2, :]
        xc = xc + xpad_ref[bi, SUBLANES - 1:SUBLANES - 1 + tt, :] * conv_w_ref[2:3, :]
        xc = xc + x_b * conv_w_ref[3:4, :]
        sq.xc = _silu(xc + conv_b_ref[...])
        sq.xcb = sq.xc.astype(BF16)
        sq.xbb = x_b.astype(BF16)

    def stage_gates_oz(sq, bi):
        sq.o_sig = jax.nn.sigmoid(proj(sq, 4))
        sq.z_silu = _silu(proj(sq, 5))

    def stage_qkv(sq, bi):
        for hh in range(H_B):
            sl = slice(hh * DH_B, (hh + 1) * DH_B)
            qkv_ref[bi, :, hh * DH_B:(hh + 1) * DH_B] = _dot(sq.xcb[:, sl], w_q_ref[hh]).astype(BF16)
            qkv_ref[bi, :, W_B + hh * DH_B:W_B + (hh + 1) * DH_B] = _dot(sq.xcb[:, sl], w_k_ref[hh]).astype(BF16)
            qkv_ref[bi, :, 2 * W_B + hh * DH_B:2 * W_B + (hh + 1) * DH_B] = _dot(sq.xbb[:, sl], w_v_ref[hh]).astype(BF16)
        gates = _dot(qkv_ref[bi], w_if_ref[...]) + b_if_ref[...]
        logf = jax.nn.log_sigmoid(gates)
        sq.decay = []
        for c in range(n_chunks):
            rows = slice(c * CHUNK, (c + 1) * CHUNK)
            bcum = pltpu.roll(_cumsum_rows(tril_b, logf[rows]), LANES - H_B, 1)
            a_all = gates[rows] - bcum
            sq.decay.append((bcum, a_all, a_all.T))

    def mlstm_unit(sq, bi, c, hh):
        rows = slice(c * CHUNK, (c + 1) * CHUNK)
        sl = slice(hh * DH_B, (hh + 1) * DH_B)
        bcum, a_all, a_t = sq.decay[c]
        qh = qkv_ref[bi, rows, hh * DH_B:(hh + 1) * DH_B]
        kh = qkv_ref[bi, rows, W_B + hh * DH_B:W_B + (hh + 1) * DH_B]
        vh = qkv_ref[bi, rows, 2 * W_B + hh * DH_B:2 * W_B + (hh + 1) * DH_B]
        m_prev = m_ref[bi, hh, 0:1, 0:1]
        a_mat = jnp.where(causal, a_t[hh:hh + 1, :], -jnp.inf)
        m_run = jnp.maximum(jnp.max(a_mat, axis=-1, keepdims=True), m_prev)
        s = jnp.exp(a_mat - m_run) * lax.dot_general(qh, kh, NT_DIMS, preferred_element_type=F32)
        g = jnp.exp(m_prev - m_run)
        c_old = c_ref[bi, hh]
        inter = lax.dot_general(qh, c_old.astype(BF16), NT_DIMS, preferred_element_type=F32)
        num = _dot(s.astype(BF16), vh) + g * inter
        n_old = n_ref[bi, hh:hh + 1, :]
        den = jnp.sum(s, axis=-1, keepdims=True) + g * jnp.sum(qh.astype(F32) * n_old, axis=-1, keepdims=True)
        m_t = bcum[:, hh:hh + 1] + m_run
        hcell = num / jnp.maximum(jnp.abs(den), jnp.exp(-m_t))
        m_last = m_run[CHUNK - 1:CHUNK, :]
        w_end = jnp.exp(a_all[:, hh:hh + 1] - m_last)
        g_end = jnp.exp(m_prev - m_last)
        kw = kh.astype(F32) * w_end
        c_ref[bi, hh] = g_end * c_old + lax.dot_general(vh, kw.astype(BF16), TN_DIMS, preferred_element_type=F32)
        n_ref[bi, hh:hh + 1, :] = g_end * n_old + jnp.sum(kw, axis=0, keepdims=True)
        m_ref[bi, hh] = jnp.broadcast_to(m_t[CHUNK - 1:CHUNK, :], (SUBLANES, LANES))
        hn = _layernorm(hcell * sq.o_sig[rows, sl], gn_g_ref[:, sl])
        yb = (hn + skip_ref[:, sl] * sq.xc[rows, sl]) * sq.z_silu[rows, sl]
        ycat_ref[bi, rows, W_A + hh * DH_B:W_A + (hh + 1) * DH_B] = yb.astype(BF16)

    def stage_ple(sq, bi):
        sq.e = _rmsnorm(_dot(p_ref[bi].astype(BF16), w_pp_ref[...]), ple_g_ref[...])

    def stage_merge(sq, bi):
        sq.x1 = x_ref[bi] + _dot(ycat_ref[bi], w_out_ref[...])

    def stage_out(sq, bi):
        gate = jax.nn.sigmoid(_dot(sq.x1.astype(BF16), w_pg_ref[...]) + b_pg_ref[...])
        x2 = sq.x1 + gate * sq.e
        y_ref[bi] = _rmsnorm(x2, fin_g_ref[...])

    half_a = list(range(H_A // 2))
    front = [stage_norm_u, stage_v, stage_za,
             lambda sq, bi: gmlp_heads(sq, bi, half_a),
             lambda sq, bi: gmlp_heads(sq, bi, [h + H_A // 2 for h in half_a]),
             stage_conv, stage_gates_oz, stage_qkv]
    units = [lambda sq, bi, c=c, hh=hh: mlstm_unit(sq, bi, c, hh) for c in range(n_chunks) for hh in range(H_B)]
    tail = [stage_ple, stage_merge, stage_out]

    def run(stages):
        for st, bi in stages:
            st(seqs[bi], bi)

    def interleave(first, second):
        done = 0
        for i, (st, bi) in enumerate(first):
            st(seqs[bi], bi)
            upto = (i + 1) * len(second) // len(first)
            run(second[done:upto])
            done = upto

    def of(stages, bi):
        return [(st, bi) for st in stages]

    pairs = [(lambda sq, bi, j=j, hh=hh: sample_state_pair(j, hh), 0)
             for j in range(rows_per_step) for hh in range(H_B)]

    interleave(of(front, 0), pairs)
    for bi in range(n_seq):
        fill = (of(front, bi + 1) if bi + 1 < n_seq else []) + (of(tail, bi - 1) if bi >= 1 else [])
        interleave(of(units, bi), fill)
    run(of(tail, n_seq - 1))


def _resident(shape):
    nd = len(shape)
    return pl.BlockSpec(shape, lambda *_: (0,) * nd, pipeline_mode=pl.Buffered(1))


def _prompt_call(x, p, weights, sample_c, sample_rows):
    batch, seq, _ = x.shape
    tt = PROMPT_TILE
    ns = SEQS_PER_STEP
    grid = (batch // ns, seq // tt)
    n_steps = grid[0] * grid[1]
    n_rows = sample_c.shape[0]
    rows_per_step = n_rows // n_steps
    assert rows_per_step * n_steps == n_rows, "sample rows are spread evenly over the prompt grid steps"

    def state_block(b, t):
        return (b * grid[1] + t, 0, 0, 0)

    in_specs = [
        pl.BlockSpec((ns, tt, D_MODEL), lambda b, t: (b, t, 0)),
        pl.BlockSpec((ns, tt, D_PLE), lambda b, t: (b, t, 0)),
    ] + [_resident(w.shape) for w in weights] + [
        pl.BlockSpec((rows_per_step, H_B, DH_B, DH_B), state_block),
    ] + [_resident(a.shape) for a in sample_rows]
    out_shape = (
        jax.ShapeDtypeStruct((batch, seq, D_MODEL), F32),
        jax.ShapeDtypeStruct((batch, H_B, DH_B, DH_B), F32),
        jax.ShapeDtypeStruct((batch, H_B, DH_B), F32),
        jax.ShapeDtypeStruct((batch, H_B, SUBLANES, LANES), F32),
        jax.ShapeDtypeStruct((batch, CONV_W - 1, W_B), F32),
        jax.ShapeDtypeStruct(sample_c.shape, F32),
        jax.ShapeDtypeStruct((n_rows, W_B), F32),
    )
    out_specs = (
        pl.BlockSpec((ns, tt, D_MODEL), lambda b, t: (b, t, 0)),
        pl.BlockSpec((ns, H_B, DH_B, DH_B), lambda b, t: (b, 0, 0, 0)),
        pl.BlockSpec((ns, H_B, DH_B), lambda b, t: (b, 0, 0)),
        pl.BlockSpec((ns, H_B, SUBLANES, LANES), lambda b, t: (b, 0, 0, 0)),
        pl.BlockSpec((ns, CONV_W - 1, W_B), lambda b, t: (b, 0, 0)),
        pl.BlockSpec((rows_per_step, H_B, DH_B, DH_B), state_block),
        pl.BlockSpec((n_rows, W_B), lambda b, t: (0, 0)),
    )
    scratch = [
        pltpu.VMEM((ns, tt + SUBLANES, W_B), F32),
        pltpu.VMEM((ns, tt, 3 * W_B), BF16),
        pltpu.VMEM((ns, tt, W_A + W_B), BF16),
    ]
    return pl.pallas_call(
        _prompt_kernel,
        grid=grid,
        in_specs=in_specs,
        out_specs=out_specs,
        out_shape=out_shape,
        scratch_shapes=scratch,
        compiler_params=pltpu.CompilerParams(
            dimension_semantics=("arbitrary", "arbitrary"),
            vmem_limit_bytes=PROMPT_VMEM_BYTES),
        name="prompt_fused",
    )(x, p, *weights, sample_c, *sample_rows)


def _sample_pre_kernel(x_ref, conv_ref, n_ref, m_ref, norm_g_ref, w_in_ref, lnv_g_ref, lnv_b_ref,
                       w00_ref, b0_ref, conv_w_ref, conv_b_ref, w_q_ref, w_k_ref, w_v_ref, w_if_ref,
                       b_if_ref, skip_ref,
                       vn_ref, ya_ref, convnew_ref, q_ref, k_ref, vs_ref, gfull_ref, numa_ref,
                       denom_ref, osig_ref, skipxc_ref, zsilu_ref, nnew_ref, mnew_ref,
                       qkv_ref):
    x = x_ref[:, 0, :]
    hb = _rmsnorm(x, norm_g_ref[...]).astype(BF16)

    def proj(i):
        return _dot(hb, w_in_ref[:, i * W_A:(i + 1) * W_A])

    u = _gelu(proj(0))
    v = _gelu(proj(1))
    z_a = proj(2)
    for hh in range(H_A):
        sl = slice(hh * DH_A, (hh + 1) * DH_A)
        vn = _layernorm(v[:, sl], lnv_g_ref[:, sl], lnv_b_ref[:, sl])
        vn_ref[:, 0, sl] = vn
        sp = vn * w00_ref[:, sl] + b0_ref[:, sl]
        ya_ref[:, sl] = (u[:, sl] * sp * _silu(z_a[:, sl])).astype(BF16)

    x_b = proj(3)
    osig_ref[...] = jax.nn.sigmoid(proj(4))
    zsilu_ref[...] = _silu(proj(5))
    c0 = conv_ref[:, 0, :]
    c1 = conv_ref[:, 1, :]
    c2 = conv_ref[:, 2, :]
    xc = c0 * conv_w_ref[0:1, :]
    xc = xc + c1 * conv_w_ref[1:2, :]
    xc = xc + c2 * conv_w_ref[2:3, :]
    xc = xc + x_b * conv_w_ref[3:4, :]
    xc = _silu(xc + conv_b_ref[...])
    convnew_ref[:, 0, :] = c1
    convnew_ref[:, 1, :] = c2
    convnew_ref[:, 2, :] = x_b
    skipxc_ref[...] = skip_ref[...] * xc
    xcb = xc.astype(BF16)
    xbb = x_b.astype(BF16)
    for hh in range(H_B):
        sl = slice(hh * DH_B, (hh + 1) * DH_B)
        qkv_ref[:, hh * DH_B:(hh + 1) * DH_B] = _dot(xcb[:, sl], w_q_ref[hh]).astype(BF16)
        qkv_ref[:, W_B + hh * DH_B:W_B + (hh + 1) * DH_B] = _dot(xcb[:, sl], w_k_ref[hh]).astype(BF16)
        qkv_ref[:, 2 * W_B + hh * DH_B:2 * W_B + (hh + 1) * DH_B] = _dot(xbb[:, sl], w_v_ref[hh]).astype(BF16)
    gates = _dot(qkv_ref[...], w_if_ref[...]) + b_if_ref[...]
    logf = pltpu.roll(jax.nn.log_sigmoid(gates), LANES - H_B, 1)
    li = gates[:, 0:H_B]
    inter = logf[:, 0:H_B] + m_ref[...]
    m_t = jnp.maximum(inter, li)
    s_w = jnp.exp(li - m_t)
    g = jnp.exp(inter - m_t)
    e_neg = jnp.exp(-m_t)
    mnew_ref[...] = m_t
    for hh in range(H_B):
        sl = slice(hh * DH_B, (hh + 1) * DH_B)
        qf = qkv_ref[:, hh * DH_B:(hh + 1) * DH_B].astype(F32)
        kf = qkv_ref[:, W_B + hh * DH_B:W_B + (hh + 1) * DH_B].astype(F32)
        vf = qkv_ref[:, 2 * W_B + hh * DH_B:2 * W_B + (hh + 1) * DH_B].astype(F32)
        s_h = s_w[:, hh:hh + 1]
        g_h = g[:, hh:hh + 1]
        n_old = n_ref[:, hh, :]
        s_qk = s_h * jnp.sum(qf * kf, axis=-1, keepdims=True)
        den = s_qk + g_h * jnp.sum(n_old * qf, axis=-1, keepdims=True)
        denom = jnp.maximum(jnp.abs(den), e_neg[:, hh:hh + 1])
        q_ref[:, sl] = qf
        k_ref[:, sl] = kf
        vs_ref[:, sl] = s_h * vf
        gfull_ref[:, sl] = jnp.broadcast_to(g_h, qf.shape)
        numa_ref[:, sl] = s_qk * vf
        denom_ref[:, sl] = jnp.broadcast_to(denom, qf.shape)
        nnew_ref[:, hh, :] = g_h * n_old + s_h * kf


def _sample_post_kernel(x_ref, p_ref, ya_ref, cq_ref, gfull_ref, numa_ref, denom_ref, osig_ref,
                        skipxc_ref, zsilu_ref, gn_g_ref, w_out_ref, w_pg_ref, b_pg_ref, w_pp_ref,
                        ple_g_ref, fin_g_ref, y_ref, ycat_ref):
    ycat_ref[:, 0:W_A] = ya_ref[...]
    for hh in range(H_B):
        sl = slice(hh * DH_B, (hh + 1) * DH_B)
        hcell = (numa_ref[:, sl] + gfull_ref[:, sl] * cq_ref[:, sl]) / denom_ref[:, sl]
        hn = _layernorm(hcell * osig_ref[:, sl], gn_g_ref[:, sl])
        ycat_ref[:, W_A + hh * DH_B:W_A + (hh + 1) * DH_B] = ((hn + skipxc_ref[:, sl]) * zsilu_ref[:, sl]).astype(BF16)
    x1 = x_ref[:, 0, :] + _dot(ycat_ref[...], w_out_ref[...])
    e = _rmsnorm(_dot(p_ref[:, 0, :].astype(BF16), w_pp_ref[...]), ple_g_ref[...])
    gate = jax.nn.sigmoid(_dot(x1.astype(BF16), w_pg_ref[...]) + b_pg_ref[...])
    x2 = x1 + gate * e
    y_ref[:, 0, :] = _rmsnorm(x2, fin_g_ref[...])


def _single_step_call(body, inputs, out_shape, scratch, name):
    return pl.pallas_call(
        body,
        grid=(1,),
        in_specs=[_resident(a.shape) for a in inputs],
        out_specs=tuple(pl.BlockSpec(s.shape, lambda *_, nd=len(s.shape): (0,) * nd) for s in out_shape),
        out_shape=out_shape,
        scratch_shapes=scratch,
        compiler_params=pltpu.CompilerParams(
            dimension_semantics=("arbitrary",), vmem_limit_bytes=SAMPLE_VMEM_BYTES),
        name=name,
    )(*inputs)


def kernel(x_prompt, x_sample, state_mlstm_C, state_mlstm_n, state_mlstm_m, state_conv, p_prompt, p_sample,
           norm_in_g, w_in, ln_v_g, ln_v_b, w_spatial, b_spatial, conv_w, conv_b, w_q, w_k, w_v, w_if, b_if,
           gn_g, skip, w_out, w_ple_gate, b_ple_gate, w_ple_proj, ple_norm_g, final_norm_g):
    assert norm_in_g.shape[0] == 1, "single-layer trunk"
    nb = x_sample.shape[0]
    row = lambda a: a.reshape(1, -1).astype(F32)

    norm_g = row(norm_in_g[0])
    w_in_b = w_in[0].astype(BF16)
    lnv_g = row(ln_v_g[0])
    lnv_b = row(ln_v_b[0])
    w_sp = w_spatial[0]
    b_sp_full = jnp.repeat(b_spatial[0].T, DH_A, axis=1)
    cw = conv_w[0]
    cb = row(conv_b[0])
    wq_b = w_q[0].astype(BF16)
    wk_b = (w_k[0] * (DH_B ** -0.5)).astype(BF16)
    wv_b = w_v[0].astype(BF16)
    k_fix = jnp.concatenate([jnp.ones((W_B, 1), F32), jnp.full((W_B, 1), DH_B ** 0.5, F32), jnp.ones((W_B, 1), F32)], 0)
    w_if_p = jnp.pad(w_if[0] * k_fix, ((0, 0), (0, LANES - 2 * H_B))).astype(BF16)
    b_if_p = jnp.pad(b_if[0], (0, LANES - 2 * H_B)).reshape(1, LANES)
    gn = row(gn_g[0])
    sk = row(skip[0])
    w_out_b = w_out[0].astype(BF16)
    w_pg_b = w_ple_gate[0].astype(BF16)
    b_pg = row(b_ple_gate[0])
    w_pp_b = w_ple_proj[0].astype(BF16)
    ple_g = row(ple_norm_g[0])
    fin_g = row(final_norm_g)

    xs = x_sample
    ps = p_sample[0]
    conv_s = state_conv[0]
    n_s = state_mlstm_n[0]
    m_s = state_mlstm_m[0]
    w00 = jnp.repeat(w_sp[:, 0, 0], DH_A).reshape(1, W_A)
    b0 = jnp.repeat(b_spatial[0][:, 0], DH_A).reshape(1, W_A)
    full = jax.ShapeDtypeStruct((nb, W_B), F32)
    pre_out = (
        jax.ShapeDtypeStruct((nb, 1, W_A), F32),
        jax.ShapeDtypeStruct((nb, W_A), BF16),
        jax.ShapeDtypeStruct((nb, CONV_W - 1, W_B), F32),
        full, full, full, full, full, full, full, full, full,
        jax.ShapeDtypeStruct((nb, H_B, DH_B), F32),
        jax.ShapeDtypeStruct((nb, H_B), F32),
    )
    (vn_s, ya_s, conv_new, q_s, k_s, vs_s, gfull, numa, denom, osig, skipxc, zsilu, n_new, m_new) = _single_step_call(
        _sample_pre_kernel,
        [xs, conv_s, n_s, m_s, norm_g, w_in_b, lnv_g, lnv_b, w00, b0, cw, cb, wq_b, wk_b, wv_b, w_if_p, b_if_p, sk],
        pre_out, [pltpu.VMEM((nb, 3 * W_B), BF16)], "sample_pre")

    prompt_weights = [norm_g, w_in_b, lnv_g, lnv_b, w_sp, b_sp_full, cw, cb, wq_b, wk_b, wv_b, w_if_p, b_if_p,
                      gn, sk, w_out_b, w_pg_b, b_pg, w_pp_b, ple_g, fin_g]
    y_p, c_p, n_p, m_p, conv_p, c_new, cq = _prompt_call(
        x_prompt, p_prompt[0], prompt_weights, state_mlstm_C[0], [q_s, k_s, vs_s, gfull])

    (y_s,) = _single_step_call(
        _sample_post_kernel,
        [xs, ps, ya_s, cq, gfull, numa, denom, osig, skipxc, zsilu, gn, w_out_b, w_pg_b, b_pg, w_pp_b, ple_g, fin_g],
        (jax.ShapeDtypeStruct((nb, 1, D_MODEL), F32),), [pltpu.VMEM((nb, W_A + W_B), BF16)], "sample_post")

    return (y_p, y_s, c_p[None], n_p[None], m_p[:, :, 0, 0][None], conv_p[None],
            c_new[None], n_new[None], m_new[None], conv_new[None], vn_s[None])
```

```python
import jax
import jax.numpy as jnp
from jax import lax
from jax.experimental import pallas as pl
from jax.experimental.pallas import tpu as pltpu

D_MODEL = 1024
W_A = 1024
H_A = 8
DH_A = 128
W_B = 1024
H_B = 4
DH_B = 256
CHUNK = 128
CONV_W = 4
D_PLE = 256
N_IN = 3 * W_A + 3 * W_B
EPS = 1e-6

LANES = 128
SUBLANES = 8
PROMPT_TILE = 256
SEQS_PER_STEP = 1
PROMPT_VMEM_BYTES = 56 * 1024 * 1024
SAMPLE_VMEM_BYTES = 48 * 1024 * 1024

F32 = jnp.float32
BF16 = jnp.bfloat16
NT_DIMS = (((1,), (1,)), ((), ()))
TN_DIMS = (((0,), (0,)), ((), ()))


def _rmsnorm(x, g):
    return x * lax.rsqrt(jnp.mean(x * x, axis=-1, keepdims=True) + EPS) * g


def _layernorm(x, g, b=None):
    mu = jnp.mean(x, axis=-1, keepdims=True)
    d = x - mu
    var = jnp.mean(d * d, axis=-1, keepdims=True)
    y = d * lax.rsqrt(var + EPS) * g
    return y if b is None else y + b


def _gelu(x):
    return 0.5 * x * (1.0 + lax.erf(x * (0.5 ** 0.5)))


def _silu(x):
    return x * jax.nn.sigmoid(x)


def _dot(a, b):
    return jnp.dot(a, b, preferred_element_type=F32)


def _cumsum_rows(tril_b, x):
    x1 = x.astype(BF16)
    r1 = x - x1.astype(F32)
    x2 = r1.astype(BF16)
    x3 = (r1 - x2.astype(F32)).astype(BF16)
    return _dot(tril_b, x1) + _dot(tril_b, x2) + _dot(tril_b, x3)


def _prompt_kernel(x_ref, p_ref, norm_g_ref, w_in_ref, lnv_g_ref, lnv_b_ref, w_sp_ref, b_sp_ref,
                   conv_w_ref, conv_b_ref, w_q_ref, w_k_ref, w_v_ref, w_if_ref, b_if_ref,
                   gn_g_ref, skip_ref, w_out_ref, w_pg_ref, b_pg_ref, w_pp_ref, ple_g_ref, fin_g_ref,
                   sc_ref, sq_ref, sk_ref, svs_ref, sg_ref,
                   y_ref, c_ref, n_ref, m_ref, conv_ref, scnew_ref, scq_ref,
                   xpad_ref, qkv_ref, ycat_ref):
    n_seq, tt = x_ref.shape[0], x_ref.shape[1]
    n_chunks = tt // CHUNK
    step = pl.program_id(0) * pl.num_programs(1) + pl.program_id(1)

    @pl.when(pl.program_id(1) == 0)
    def _():
        c_ref[...] = jnp.zeros_like(c_ref)
        n_ref[...] = jnp.zeros_like(n_ref)
        m_ref[...] = jnp.zeros_like(m_ref)
        conv_ref[...] = jnp.zeros_like(conv_ref)

    @pl.when(step == 0)
    def _():
        scq_ref[...] = jnp.zeros_like(scq_ref)

    rows_per_step = sc_ref.shape[0]
    group_row = lax.broadcasted_iota(jnp.int32, (SUBLANES, DH_B), 0)

    def sample_state_pair(j, hh):
        sl = slice(hh * DH_B, (hh + 1) * DH_B)
        r = step * rows_per_step + j
        base = pl.multiple_of((r // SUBLANES) * SUBLANES, SUBLANES)
        is_row = group_row == (r - base)
        group = pl.ds(base, SUBLANES)
        c_old = sc_ref[j, hh]
        cq = lax.dot_general(sq_ref[group, sl].astype(BF16), c_old.astype(BF16), NT_DIMS, preferred_element_type=F32)
        scq_ref[group, sl] = jnp.where(is_row, cq, scq_ref[group, sl])
        k_row = jnp.where(is_row, sk_ref[group, sl], 0.0).astype(BF16)
        outer = lax.dot_general(svs_ref[group, sl].astype(BF16), k_row, TN_DIMS, preferred_element_type=F32)
        scnew_ref[j, hh] = sg_ref[pl.ds(r, 1), sl] * c_old + outer

    row = lax.broadcasted_iota(jnp.int32, (CHUNK, CHUNK), 0)
    col = lax.broadcasted_iota(jnp.int32, (CHUNK, CHUNK), 1)
    causal = row >= col
    tril = causal.astype(F32)
    tril_b = causal.astype(BF16)

    class Seq:
        pass

    seqs = [Seq() for _ in range(n_seq)]

    def proj(sq, i):
        return _dot(sq.hb, w_in_ref[:, i * W_A:(i + 1) * W_A])

    def stage_norm(sq, bi):
        sq.hb = _rmsnorm(x_ref[bi], norm_g_ref[...]).astype(BF16)

    def stage_u(sq, bi):
        sq.u = _gelu(proj(sq, 0))

    def stage_v(sq, bi):
        sq.v = _gelu(proj(sq, 1))

    def stage_za(sq, bi):
        sq.z_a = proj(sq, 2)

    def gmlp_heads(sq, bi, heads):
        for hh in heads:
            sl = slice(hh * DH_A, (hh + 1) * DH_A)
            vn = _layernorm(sq.v[:, sl], lnv_g_ref[:, sl], lnv_b_ref[:, sl]).astype(BF16)
            wm = (w_sp_ref[hh] * tril).astype(BF16)
            vn_wide = jnp.concatenate([vn[c * CHUNK:(c + 1) * CHUNK] for c in range(n_chunks)], axis=1)
            sp_wide = _dot(wm, vn_wide)
            for c in range(n_chunks):
                rows = slice(c * CHUNK, (c + 1) * CHUNK)
                sp = sp_wide[:, c * DH_A:(c + 1) * DH_A] + b_sp_ref[:, sl]
                ycat_ref[bi, rows, sl] = (sq.u[rows, sl] * sp * _silu(sq.z_a[rows, sl])).astype(BF16)

    def stage_conv(sq, bi):
        x_b = proj(sq, 3)
        xpad_ref[bi, SUBLANES - 3:SUBLANES, :] = conv_ref[bi]
        xpad_ref[bi, SUBLANES:SUBLANES + tt, :] = x_b
        conv_ref[bi] = x_b[tt - 3:tt, :]
        xc = xpad_ref[bi, SUBLANES - 3:SUBLANES - 3 + tt, :] * conv_w_ref[0:1, :]
        xc = xc + xpad_ref[bi, SUBLANES - 2:SUBLANES - 2 + tt, :] * conv_w_ref[1:2, :]
        xc = xc + xpad_ref[bi, SUBLANES - 1:SUBLANES - 1 + tt, :] * conv_w_ref[2:3, :]
        xc = xc + x_b * conv_w_ref[3:4, :]
        sq.xc = _silu(xc + conv_b_ref[...])
        sq.xcb = sq.xc.astype(BF16)
        sq.xbb = x_b.astype(BF16)

    def stage_gate_o(sq, bi):
        sq.o_sig = jax.nn.sigmoid(proj(sq, 4))

    def stage_gate_z(sq, bi):
        sq.z_silu = _silu(proj(sq, 5))

    def stage_qkv(sq, bi):
        for hh in range(H_B):
            sl = slice(hh * DH_B, (hh + 1) * DH_B)
            qkv_ref[bi, :, hh * DH_B:(hh + 1) * DH_B] = _dot(sq.xcb[:, sl], w_q_ref[hh]).astype(BF16)
            qkv_ref[bi, :, W_B + hh * DH_B:W_B + (hh + 1) * DH_B] = _dot(sq.xcb[:, sl], w_k_ref[hh]).astype(BF16)
            qkv_ref[bi, :, 2 * W_B + hh * DH_B:2 * W_B + (hh + 1) * DH_B] = _dot(sq.xbb[:, sl], w_v_ref[hh]).astype(BF16)
        gates = _dot(qkv_ref[bi], w_if_ref[...]) + b_if_ref[...]
        logf = jax.nn.log_sigmoid(gates)
        sq.decay = []
        for c in range(n_chunks):
            rows = slice(c * CHUNK, (c + 1) * CHUNK)
            bcum = pltpu.roll(_cumsum_rows(tril_b, logf[rows]), LANES - H_B, 1)
            a_all = gates[rows] - bcum
            sq.decay.append((bcum, a_all, a_all.T))

    def mlstm_unit(sq, bi, c, hh):
        rows = slice(c * CHUNK, (c + 1) * CHUNK)
        sl = slice(hh * DH_B, (hh + 1) * DH_B)
        bcum, a_all, a_t = sq.decay[c]
        qh = qkv_ref[bi, rows, hh * DH_B:(hh + 1) * DH_B]
        kh = qkv_ref[bi, rows, W_B + hh * DH_B:W_B + (hh + 1) * DH_B]
        vh = qkv_ref[bi, rows, 2 * W_B + hh * DH_B:2 * W_B + (hh + 1) * DH_B]
        m_prev = m_ref[bi, hh, 0:1, 0:1]
        a_mat = jnp.where(causal, a_t[hh:hh + 1, :], -jnp.inf)
        m_run = jnp.maximum(jnp.max(a_mat, axis=-1, keepdims=True), m_prev)
        s = jnp.exp(a_mat - m_run) * lax.dot_general(qh, kh, NT_DIMS, preferred_element_type=F32)
        g = jnp.exp(m_prev - m_run)
        c_old = c_ref[bi, hh]
        inter = lax.dot_general(qh, c_old.astype(BF16), NT_DIMS, preferred_element_type=F32)
        num = _dot(s.astype(BF16), vh) + g * inter
        n_old = n_ref[bi, hh:hh + 1, :]
        den = jnp.sum(s, axis=-1, keepdims=True) + g * jnp.sum(qh.astype(F32) * n_old, axis=-1, keepdims=True)
        m_t = bcum[:, hh:hh + 1] + m_run
        hcell = num / jnp.maximum(jnp.abs(den), jnp.exp(-m_t))
        m_last = m_run[CHUNK - 1:CHUNK, :]
        w_end = jnp.exp(a_all[:, hh:hh + 1] - m_last)
        g_end = jnp.exp(m_prev - m_last)
        kw = kh.astype(F32) * w_end
        c_ref[bi, hh] = g_end * c_old + lax.dot_general(vh, kw.astype(BF16), TN_DIMS, preferred_element_type=F32)
        n_ref[bi, hh:hh + 1, :] = g_end * n_old + jnp.sum(kw, axis=0, keepdims=True)
        m_ref[bi, hh] = jnp.broadcast_to(m_t[CHUNK - 1:CHUNK, :], (SUBLANES, LANES))
        hn = _layernorm(hcell * sq.o_sig[rows, sl], gn_g_ref[:, sl])
        yb = (hn + skip_ref[:, sl] * sq.xc[rows, sl]) * sq.z_silu[rows, sl]
        ycat_ref[bi, rows, W_A + hh * DH_B:W_A + (hh + 1) * DH_B] = yb.astype(BF16)

    def stage_ple(sq, bi):
        sq.e = _rmsnorm(_dot(p_ref[bi].astype(BF16), w_pp_ref[...]), ple_g_ref[...])

    def stage_merge_a(sq, bi):
        sq.x1 = x_ref[bi] + _dot(ycat_ref[bi, :, :W_A], w_out_ref[:W_A, :])

    def stage_merge_b(sq, bi):
        sq.x1 = sq.x1 + _dot(ycat_ref[bi, :, W_A:], w_out_ref[W_A:, :])

    def stage_out(sq, bi):
        gate = jax.nn.sigmoid(_dot(sq.x1.astype(BF16), w_pg_ref[...]) + b_pg_ref[...])
        x2 = sq.x1 + gate * sq.e
        y_ref[bi] = _rmsnorm(x2, fin_g_ref[...])

    def heads(lo, hi):
        return lambda sq, bi: gmlp_heads(sq, bi, list(range(lo, hi)))

    front = [stage_norm, stage_conv, stage_u, stage_qkv, stage_v, stage_gate_o, stage_gate_z, stage_za,
             heads(0, 4), heads(4, 8)]
    units = [lambda sq, bi, c=c, hh=hh: mlstm_unit(sq, bi, c, hh) for c in range(n_chunks) for hh in range(H_B)]
    mid = [stage_ple, stage_merge_a]
    tail = [stage_merge_b, stage_out]

    def run(stages):
        for st, bi in stages:
            st(seqs[bi], bi)

    def interleave(first, second):
        done = 0
        for i, (st, bi) in enumerate(first):
            upto = -(-(i * len(second)) // len(first))
            run(second[done:upto])
            done = upto
            st(seqs[bi], bi)
        run(second[done:])

    def of(stages, bi):
        return [(st, bi) for st in stages]

    pairs = [(lambda sq, bi, j=j, hh=hh: sample_state_pair(j, hh), 0)
             for j in range(rows_per_step) for hh in range(H_B)]

    interleave(of(front, 0), pairs)
    for bi in range(n_seq):
        fill = (of(front, bi + 1) if bi + 1 < n_seq else []) + (of(tail, bi - 1) if bi >= 1 else [])
        interleave(of(units, bi), of(mid, bi) + fill)
    run(of(tail, n_seq - 1))


def _resident(shape):
    nd = len(shape)
    return pl.BlockSpec(shape, lambda *_: (0,) * nd, pipeline_mode=pl.Buffered(1))


def _prompt_call(x, p, weights, sample_c, sample_rows):
    batch, seq, _ = x.shape
    tt = PROMPT_TILE
    ns = SEQS_PER_STEP
    grid = (batch // ns, seq // tt)
    n_steps = grid[0] * grid[1]
    n_rows = sample_c.shape[0]
    rows_per_step = n_rows // n_steps
    assert rows_per_step * n_steps == n_rows, "sample rows are spread evenly over the prompt grid steps"

    def state_block(b, t):
        return (b * grid[1] + t, 0, 0, 0)

    in_specs = [
        pl.BlockSpec((ns, tt, D_MODEL), lambda b, t: (b, t, 0)),
        pl.BlockSpec((ns, tt, D_PLE), lambda b, t: (b, t, 0)),
    ] + [_resident(w.shape) for w in weights] + [
        pl.BlockSpec((rows_per_step, H_B, DH_B, DH_B), state_block),
    ] + [_resident(a.shape) for a in sample_rows]
    out_shape = (
        jax.ShapeDtypeStruct((batch, seq, D_MODEL), F32),
        jax.ShapeDtypeStruct((batch, H_B, DH_B, DH_B), F32),
        jax.ShapeDtypeStruct((batch, H_B, DH_B), F32),
        jax.ShapeDtypeStruct((batch, H_B, SUBLANES, LANES), F32),
        jax.ShapeDtypeStruct((batch, CONV_W - 1, W_B), F32),
        jax.ShapeDtypeStruct(sample_c.shape, F32),
        jax.ShapeDtypeStruct((n_rows, W_B), F32),
    )
    out_specs = (
        pl.BlockSpec((ns, tt, D_MODEL), lambda b, t: (b, t, 0)),
        pl.BlockSpec((ns, H_B, DH_B, DH_B), lambda b, t: (b, 0, 0, 0)),
        pl.BlockSpec((ns, H_B, DH_B), lambda b, t: (b, 0, 0)),
        pl.BlockSpec((ns, H_B, SUBLANES, LANES), lambda b, t: (b, 0, 0, 0)),
        pl.BlockSpec((ns, CONV_W - 1, W_B), lambda b, t: (b, 0, 0)),
        pl.BlockSpec((rows_per_step, H_B, DH_B, DH_B), state_block),
        pl.BlockSpec((n_rows, W_B), lambda b, t: (0, 0)),
    )
    scratch = [
        pltpu.VMEM((ns, tt + SUBLANES, W_B), F32),
        pltpu.VMEM((ns, tt, 3 * W_B), BF16),
        pltpu.VMEM((ns, tt, W_A + W_B), BF16),
    ]
    return pl.pallas_call(
        _prompt_kernel,
        grid=grid,
        in_specs=in_specs,
        out_specs=out_specs,
        out_shape=out_shape,
        scratch_shapes=scratch,
        compiler_params=pltpu.CompilerParams(
            dimension_semantics=("arbitrary", "arbitrary"),
            vmem_limit_bytes=PROMPT_VMEM_BYTES),
        name="prompt_fused",
    )(x, p, *weights, sample_c, *sample_rows)


def _sample_pre_kernel(x_ref, conv_ref, n_ref, m_ref, norm_g_ref, w_in_ref, lnv_g_ref, lnv_b_ref,
                       w00_ref, b0_ref, conv_w_ref, conv_b_ref, w_q_ref, w_k_ref, w_v_ref, w_if_ref,
                       b_if_ref, skip_ref,
                       vn_ref, ya_ref, convnew_ref, q_ref, k_ref, vs_ref, gfull_ref, numa_ref,
                       denom_ref, osig_ref, skipxc_ref, zsilu_ref, nnew_ref, mnew_ref,
                       qkv_ref):
    x = x_ref[:, 0, :]
    hb = _rmsnorm(x, norm_g_ref[...]).astype(BF16)

    def proj(i):
        return _dot(hb, w_in_ref[:, i * W_A:(i + 1) * W_A])

    u = _gelu(proj(0))
    v = _gelu(proj(1))
    z_a = proj(2)
    for hh in range(H_A):
        sl = slice(hh * DH_A, (hh + 1) * DH_A)
        vn = _layernorm(v[:, sl], lnv_g_ref[:, sl], lnv_b_ref[:, sl])
        vn_ref[:, 0, sl] = vn
        sp = vn * w00_ref[:, sl] + b0_ref[:, sl]
        ya_ref[:, sl] = (u[:, sl] * sp * _silu(z_a[:, sl])).astype(BF16)

    x_b = proj(3)
    osig_ref[...] = jax.nn.sigmoid(proj(4))
    zsilu_ref[...] = _silu(proj(5))
    c0 = conv_ref[:, 0, :]
    c1 = conv_ref[:, 1, :]
    c2 = conv_ref[:, 2, :]
    xc = c0 * conv_w_ref[0:1, :]
    xc = xc + c1 * conv_w_ref[1:2, :]
    xc = xc + c2 * conv_w_ref[2:3, :]
    xc = xc + x_b * conv_w_ref[3:4, :]
    xc = _silu(xc + conv_b_ref[...])
    convnew_ref[:, 0, :] = c1
    convnew_ref[:, 1, :] = c2
    convnew_ref[:, 2, :] = x_b
    skipxc_ref[...] = skip_ref[...] * xc
    xcb = xc.astype(BF16)
    xbb = x_b.astype(BF16)
    for hh in range(H_B):
        sl = slice(hh * DH_B, (hh + 1) * DH_B)
        qkv_ref[:, hh * DH_B:(hh + 1) * DH_B] = _dot(xcb[:, sl], w_q_ref[hh]).astype(BF16)
        qkv_ref[:, W_B + hh * DH_B:W_B + (hh + 1) * DH_B] = _dot(xcb[:, sl], w_k_ref[hh]).astype(BF16)
        qkv_ref[:, 2 * W_B + hh * DH_B:2 * W_B + (hh + 1) * DH_B] = _dot(xbb[:, sl], w_v_ref[hh]).astype(BF16)
    gates = _dot(qkv_ref[...], w_if_ref[...]) + b_if_ref[...]
    logf = pltpu.roll(jax.nn.log_sigmoid(gates), LANES - H_B, 1)
    li = gates[:, 0:H_B]
    inter = logf[:, 0:H_B] + m_ref[...]
    m_t = jnp.maximum(inter, li)
    s_w = jnp.exp(li - m_t)
    g = jnp.exp(inter - m_t)
    e_neg = jnp.exp(-m_t)
    mnew_ref[...] = m_t
    for hh in range(H_B):
        sl = slice(hh * DH_B, (hh + 1) * DH_B)
        qf = qkv_ref[:, hh * DH_B:(hh + 1) * DH_B].astype(F32)
        kf = qkv_ref[:, W_B + hh * DH_B:W_B + (hh + 1) * DH_B].astype(F32)
        vf = qkv_ref[:, 2 * W_B + hh * DH_B:2 * W_B + (hh + 1) * DH_B].astype(F32)
        s_h = s_w[:, hh:hh + 1]
        g_h = g[:, hh:hh + 1]
        n_old = n_ref[:, hh, :]
        s_qk = s_h * jnp.sum(qf * kf, axis=-1, keepdims=True)
        den = s_qk + g_h * jnp.sum(n_old * qf, axis=-1, keepdims=True)
        denom = jnp.maximum(jnp.abs(den), e_neg[:, hh:hh + 1])
        q_ref[:, sl] = qf
        k_ref[:, sl] = kf
        vs_ref[:, sl] = s_h * vf
        gfull_ref[:, sl] = jnp.broadcast_to(g_h, qf.shape)
        numa_ref[:, sl] = s_qk * vf
        denom_ref[:, sl] = jnp.broadcast_to(denom, qf.shape)
        nnew_ref[:, hh, :] = g_h * n_old + s_h * kf


def _sample_post_kernel(x_ref, p_ref, ya_ref, cq_ref, gfull_ref, numa_ref, denom_ref, osig_ref,
                        skipxc_ref, zsilu_ref, gn_g_ref, w_out_ref, w_pg_ref, b_pg_ref, w_pp_ref,
                        ple_g_ref, fin_g_ref, y_ref, ycat_ref):
    ycat_ref[:, 0:W_A] = ya_ref[...]
    for hh in range(H_B):
        sl = slice(hh * DH_B, (hh + 1) * DH_B)
        hcell = (numa_ref[:, sl] + gfull_ref[:, sl] * cq_ref[:, sl]) / denom_ref[:, sl]
        hn = _layernorm(hcell * osig_ref[:, sl], gn_g_ref[:, sl])
        ycat_ref[:, W_A + hh * DH_B:W_A + (hh + 1) * DH_B] = ((hn + skipxc_ref[:, sl]) * zsilu_ref[:, sl]).astype(BF16)
    x1 = x_ref[:, 0, :] + _dot(ycat_ref[...], w_out_ref[...])
    e = _rmsnorm(_dot(p_ref[:, 0, :].astype(BF16), w_pp_ref[...]), ple_g_ref[...])
    gate = jax.nn.sigmoid(_dot(x1.astype(BF16), w_pg_ref[...]) + b_pg_ref[...])
    x2 = x1 + gate * e
    y_ref[:, 0, :] = _rmsnorm(x2, fin_g_ref[...])


def _single_step_call(body, inputs, out_shape, scratch, name):
    return pl.pallas_call(
        body,
        grid=(1,),
        in_specs=[_resident(a.shape) for a in inputs],
        out_specs=tuple(pl.BlockSpec(s.shape, lambda *_, nd=len(s.shape): (0,) * nd) for s in out_shape),
        out_shape=out_shape,
        scratch_shapes=scratch,
        compiler_params=pltpu.CompilerParams(
            dimension_semantics=("arbitrary",), vmem_limit_bytes=SAMPLE_VMEM_BYTES),
        name=name,
    )(*inputs)


def kernel(x_prompt, x_sample, state_mlstm_C, state_mlstm_n, state_mlstm_m, state_conv, p_prompt, p_sample,
           norm_in_g, w_in, ln_v_g, ln_v_b, w_spatial, b_spatial, conv_w, conv_b, w_q, w_k, w_v, w_if, b_if,
           gn_g, skip, w_out, w_ple_gate, b_ple_gate, w_ple_proj, ple_norm_g, final_norm_g):
    assert norm_in_g.shape[0] == 1, "single-layer trunk"
    nb = x_sample.shape[0]
    row = lambda a: a.reshape(1, -1).astype(F32)

    norm_g = row(norm_in_g[0])
    w_in_b = w_in[0].astype(BF16)
    lnv_g = row(ln_v_g[0])
    lnv_b = row(ln_v_b[0])
    w_sp = w_spatial[0]
    b_sp_full = jnp.repeat(b_spatial[0].T, DH_A, axis=1)
    cw = conv_w[0]
    cb = row(conv_b[0])
    wq_b = w_q[0].astype(BF16)
    wk_b = (w_k[0] * (DH_B ** -0.5)).astype(BF16)
    wv_b = w_v[0].astype(BF16)
    k_fix = jnp.concatenate([jnp.ones((W_B, 1), F32), jnp.full((W_B, 1), DH_B ** 0.5, F32), jnp.ones((W_B, 1), F32)], 0)
    w_if_p = jnp.pad(w_if[0] * k_fix, ((0, 0), (0, LANES - 2 * H_B))).astype(BF16)
    b_if_p = jnp.pad(b_if[0], (0, LANES - 2 * H_B)).reshape(1, LANES)
    gn = row(gn_g[0])
    sk = row(skip[0])
    w_out_b = w_out[0].astype(BF16)
    w_pg_b = w_ple_gate[0].astype(BF16)
    b_pg = row(b_ple_gate[0])
    w_pp_b = w_ple_proj[0].astype(BF16)
    ple_g = row(ple_norm_g[0])
    fin_g = row(final_norm_g)

    xs = x_sample
    ps = p_sample[0]
    conv_s = state_conv[0]
    n_s = state_mlstm_n[0]
    m_s = state_mlstm_m[0]
    w00 = jnp.repeat(w_sp[:, 0, 0], DH_A).reshape(1, W_A)
    b0 = jnp.repeat(b_spatial[0][:, 0], DH_A).reshape(1, W_A)
    full = jax.ShapeDtypeStruct((nb, W_B), F32)
    pre_out = (
        jax.ShapeDtypeStruct((nb, 1, W_A), F32),
        jax.ShapeDtypeStruct((nb, W_A), BF16),
        jax.ShapeDtypeStruct((nb, CONV_W - 1, W_B), F32),
        full, full, full, full, full, full, full, full, full,
        jax.ShapeDtypeStruct((nb, H_B, DH_B), F32),
        jax.ShapeDtypeStruct((nb, H_B), F32),
    )
    (vn_s, ya_s, conv_new, q_s, k_s, vs_s, gfull, numa, denom, osig, skipxc, zsilu, n_new, m_new) = _single_step_call(
        _sample_pre_kernel,
        [xs, conv_s, n_s, m_s, norm_g, w_in_b, lnv_g, lnv_b, w00, b0, cw, cb, wq_b, wk_b, wv_b, w_if_p, b_if_p, sk],
        pre_out, [pltpu.VMEM((nb, 3 * W_B), BF16)], "sample_pre")

    prompt_weights = [norm_g, w_in_b, lnv_g, lnv_b, w_sp, b_sp_full, cw, cb, wq_b, wk_b, wv_b, w_if_p, b_if_p,
                      gn, sk, w_out_b, w_pg_b, b_pg, w_pp_b, ple_g, fin_g]
    y_p, c_p, n_p, m_p, conv_p, c_new, cq = _prompt_call(
        x_prompt, p_prompt[0], prompt_weights, state_mlstm_C[0], [q_s, k_s, vs_s, gfull])

    (y_s,) = _single_step_call(
        _sample_post_kernel,
        [xs, ps, ya_s, cq, gfull, numa, denom, osig, skipxc, zsilu, gn, w_out_b, w_pg_b, b_pg, w_pp_b, ple_g, fin_g],
        (jax.ShapeDtypeStruct((nb, 1, D_MODEL), F32),), [pltpu.VMEM((nb, W_A + W_B), BF16)], "sample_post")

    return (y_p, y_s, c_p[None], n_p[None], m_p[:, :, 0, 0][None], conv_p[None],
            c_new[None], n_new[None], m_new[None], conv_new[None], vn_s[None])
```

```python
import jax
import jax.numpy as jnp
from jax import lax
from jax.experimental import pallas as pl
from jax.experimental.pallas import tpu as pltpu

D_MODEL = 1024
W_A = 1024
H_A = 8
DH_A = 128
W_B = 1024
H_B = 4
DH_B = 256
CHUNK = 128
CONV_W = 4
D_PLE = 256
N_IN = 3 * W_A + 3 * W_B
EPS = 1e-6

LANES = 128
SUBLANES = 8
PROMPT_TILE = 256
SEQS_PER_STEP = 1
STEP_ORDER = (
    "norm", "conv", "u", "v", "za", "qkv", "heads_lo", "gate_o", "heads_hi", "gate_z",
    "unit", "ple", "unit", "unit", "unit", "unit", "merge_a", "unit", "unit", "unit",
    "merge_b", "pair", "pair", "pair", "pair", "out", "pair", "pair", "pair", "pair",
)
PROMPT_VMEM_BYTES = 56 * 1024 * 1024
SAMPLE_VMEM_BYTES = 48 * 1024 * 1024

F32 = jnp.float32
BF16 = jnp.bfloat16
NT_DIMS = (((1,), (1,)), ((), ()))
TN_DIMS = (((0,), (0,)), ((), ()))


def _rmsnorm(x, g):
    return x * lax.rsqrt(jnp.mean(x * x, axis=-1, keepdims=True) + EPS) * g


def _layernorm(x, g, b=None):
    mu = jnp.mean(x, axis=-1, keepdims=True)
    d = x - mu
    var = jnp.mean(d * d, axis=-1, keepdims=True)
    y = d * lax.rsqrt(var + EPS) * g
    return y if b is None else y + b


def _gelu(x):
    return 0.5 * x * (1.0 + lax.erf(x * (0.5 ** 0.5)))


def _silu(x):
    return x * jax.nn.sigmoid(x)


def _dot(a, b):
    return jnp.dot(a, b, preferred_element_type=F32)


def _cumsum_rows(tril_b, x):
    x1 = x.astype(BF16)
    r1 = x - x1.astype(F32)
    x2 = r1.astype(BF16)
    x3 = (r1 - x2.astype(F32)).astype(BF16)
    return _dot(tril_b, x1) + _dot(tril_b, x2) + _dot(tril_b, x3)


def _prompt_kernel(x_ref, p_ref, norm_g_ref, w_in_ref, lnv_g_ref, lnv_b_ref, w_sp_ref, b_sp_ref,
                   conv_w_ref, conv_b_ref, w_q_ref, w_k_ref, w_v_ref, w_if_ref, b_if_ref,
                   gn_g_ref, skip_ref, w_out_ref, w_pg_ref, b_pg_ref, w_pp_ref, ple_g_ref, fin_g_ref,
                   sc_ref, sq_ref, sk_ref, svs_ref, sg_ref,
                   y_ref, c_ref, n_ref, m_ref, conv_ref, scnew_ref, scq_ref,
                   xpad_ref, qkv_ref, ycat_ref):
    n_seq, tt = x_ref.shape[0], x_ref.shape[1]
    n_chunks = tt // CHUNK
    step = pl.program_id(0) * pl.num_programs(1) + pl.program_id(1)

    @pl.when(pl.program_id(1) == 0)
    def _():
        c_ref[...] = jnp.zeros_like(c_ref)
        n_ref[...] = jnp.zeros_like(n_ref)
        m_ref[...] = jnp.zeros_like(m_ref)
        conv_ref[...] = jnp.zeros_like(conv_ref)

    @pl.when(step == 0)
    def _():
        scq_ref[...] = jnp.zeros_like(scq_ref)

    rows_per_step = sc_ref.shape[0]
    group_row = lax.broadcasted_iota(jnp.int32, (SUBLANES, DH_B), 0)

    def sample_state_pair(j, hh):
        sl = slice(hh * DH_B, (hh + 1) * DH_B)
        r = step * rows_per_step + j
        base = pl.multiple_of((r // SUBLANES) * SUBLANES, SUBLANES)
        is_row = group_row == (r - base)
        group = pl.ds(base, SUBLANES)
        c_old = sc_ref[j, hh]
        cq = lax.dot_general(sq_ref[group, sl].astype(BF16), c_old.astype(BF16), NT_DIMS, preferred_element_type=F32)
        scq_ref[group, sl] = jnp.where(is_row, cq, scq_ref[group, sl])
        k_row = jnp.where(is_row, sk_ref[group, sl], 0.0).astype(BF16)
        outer = lax.dot_general(svs_ref[group, sl].astype(BF16), k_row, TN_DIMS, preferred_element_type=F32)
        scnew_ref[j, hh] = sg_ref[pl.ds(r, 1), sl] * c_old + outer

    row = lax.broadcasted_iota(jnp.int32, (CHUNK, CHUNK), 0)
    col = lax.broadcasted_iota(jnp.int32, (CHUNK, CHUNK), 1)
    causal = row >= col
    tril = causal.astype(F32)
    tril_b = causal.astype(BF16)

    class Seq:
        pass

    seqs = [Seq() for _ in range(n_seq)]

    def proj(sq, i):
        return _dot(sq.hb, w_in_ref[:, i * W_A:(i + 1) * W_A])

    def stage_norm(sq, bi):
        sq.hb = _rmsnorm(x_ref[bi], norm_g_ref[...]).astype(BF16)

    def stage_u(sq, bi):
        sq.u = _gelu(proj(sq, 0))

    def stage_v(sq, bi):
        sq.v = _gelu(proj(sq, 1))

    def stage_za(sq, bi):
        sq.z_a = proj(sq, 2)

    def gmlp_heads(sq, bi, heads):
        for hh in heads:
            sl = slice(hh * DH_A, (hh + 1) * DH_A)
            vn = _layernorm(sq.v[:, sl], lnv_g_ref[:, sl], lnv_b_ref[:, sl]).astype(BF16)
            wm = (w_sp_ref[hh] * tril).astype(BF16)
            vn_wide = jnp.concatenate([vn[c * CHUNK:(c + 1) * CHUNK] for c in range(n_chunks)], axis=1)
            sp_wide = _dot(wm, vn_wide)
            for c in range(n_chunks):
                rows = slice(c * CHUNK, (c + 1) * CHUNK)
                sp = sp_wide[:, c * DH_A:(c + 1) * DH_A] + b_sp_ref[:, sl]
                ycat_ref[bi, rows, sl] = (sq.u[rows, sl] * sp * _silu(sq.z_a[rows, sl])).astype(BF16)

    def stage_conv(sq, bi):
        x_b = proj(sq, 3)
        xpad_ref[bi, SUBLANES - 3:SUBLANES, :] = conv_ref[bi]
        xpad_ref[bi, SUBLANES:SUBLANES + tt, :] = x_b
        conv_ref[bi] = x_b[tt - 3:tt, :]
        xc = xpad_ref[bi, SUBLANES - 3:SUBLANES - 3 + tt, :] * conv_w_ref[0:1, :]
        xc = xc + xpad_ref[bi, SUBLANES - 2:SUBLANES - 2 + tt, :] * conv_w_ref[1:2, :]
        xc = xc + xpad_ref[bi, SUBLANES - 1:SUBLANES - 1 + tt, :] * conv_w_ref[2:3, :]
        xc = xc + x_b * conv_w_ref[3:4, :]
        sq.xc = _silu(xc + conv_b_ref[...])
        sq.xcb = sq.xc.astype(BF16)
        sq.xbb = x_b.astype(BF16)

    def stage_gate_o(sq, bi):
        sq.o_sig = jax.nn.sigmoid(proj(sq, 4))

    def stage_gate_z(sq, bi):
        sq.z_silu = _silu(proj(sq, 5))

    def stage_qkv(sq, bi):
        for hh in range(H_B):
            sl = slice(hh * DH_B, (hh + 1) * DH_B)
            qkv_ref[bi, :, hh * DH_B:(hh + 1) * DH_B] = _dot(sq.xcb[:, sl], w_q_ref[hh]).astype(BF16)
            qkv_ref[bi, :, W_B + hh * DH_B:W_B + (hh + 1) * DH_B] = _dot(sq.xcb[:, sl], w_k_ref[hh]).astype(BF16)
            qkv_ref[bi, :, 2 * W_B + hh * DH_B:2 * W_B + (hh + 1) * DH_B] = _dot(sq.xbb[:, sl], w_v_ref[hh]).astype(BF16)
        gates = _dot(qkv_ref[bi], w_if_ref[...]) + b_if_ref[...]
        logf = jax.nn.log_sigmoid(gates)
        sq.decay = []
        for c in range(n_chunks):
            rows = slice(c * CHUNK, (c + 1) * CHUNK)
            bcum = pltpu.roll(_cumsum_rows(tril_b, logf[rows]), LANES - H_B, 1)
            a_all = gates[rows] - bcum
            sq.decay.append((bcum, a_all, a_all.T))

    def mlstm_unit(sq, bi, c, hh):
        rows = slice(c * CHUNK, (c + 1) * CHUNK)
        sl = slice(hh * DH_B, (hh + 1) * DH_B)
        bcum, a_all, a_t = sq.decay[c]
        qh = qkv_ref[bi, rows, hh * DH_B:(hh + 1) * DH_B]
        kh = qkv_ref[bi, rows, W_B + hh * DH_B:W_B + (hh + 1) * DH_B]
        vh = qkv_ref[bi, rows, 2 * W_B + hh * DH_B:2 * W_B + (hh + 1) * DH_B]
        m_prev = m_ref[bi, hh, 0:1, 0:1]
        a_mat = jnp.where(causal, a_t[hh:hh + 1, :], -jnp.inf)
        m_run = jnp.maximum(jnp.max(a_mat, axis=-1, keepdims=True), m_prev)
        s = jnp.exp(a_mat - m_run) * lax.dot_general(qh, kh, NT_DIMS, preferred_element_type=F32)
        g = jnp.exp(m_prev - m_run)
        c_old = c_ref[bi, hh]
        inter = lax.dot_general(qh, c_old.astype(BF16), NT_DIMS, preferred_element_type=F32)
        num = _dot(s.astype(BF16), vh) + g * inter
        n_old = n_ref[bi, hh:hh + 1, :]
        den = jnp.sum(s, axis=-1, keepdims=True) + g * jnp.sum(qh.astype(F32) * n_old, axis=-1, keepdims=True)
        m_t = bcum[:, hh:hh + 1] + m_run
        hcell = num / jnp.maximum(jnp.abs(den), jnp.exp(-m_t))
        m_last = m_run[CHUNK - 1:CHUNK, :]
        w_end = jnp.exp(a_all[:, hh:hh + 1] - m_last)
        g_end = jnp.exp(m_prev - m_last)
        kw = kh.astype(F32) * w_end
        c_ref[bi, hh] = g_end * c_old + lax.dot_general(vh, kw.astype(BF16), TN_DIMS, preferred_element_type=F32)
        n_ref[bi, hh:hh + 1, :] = g_end * n_old + jnp.sum(kw, axis=0, keepdims=True)
        m_ref[bi, hh] = jnp.broadcast_to(m_t[CHUNK - 1:CHUNK, :], (SUBLANES, LANES))
        hn = _layernorm(hcell * sq.o_sig[rows, sl], gn_g_ref[:, sl])
        yb = (hn + skip_ref[:, sl] * sq.xc[rows, sl]) * sq.z_silu[rows, sl]
        ycat_ref[bi, rows, W_A + hh * DH_B:W_A + (hh + 1) * DH_B] = yb.astype(BF16)

    def stage_ple(sq, bi):
        sq.e = _rmsnorm(_dot(p_ref[bi].astype(BF16), w_pp_ref[...]), ple_g_ref[...])

    def stage_merge_a(sq, bi):
        sq.x1 = x_ref[bi] + _dot(ycat_ref[bi, :, :W_A], w_out_ref[:W_A, :])

    def stage_merge_b(sq, bi):
        sq.x1 = sq.x1 + _dot(ycat_ref[bi, :, W_A:], w_out_ref[W_A:, :])

    def stage_out(sq, bi):
        gate = jax.nn.sigmoid(_dot(sq.x1.astype(BF16), w_pg_ref[...]) + b_pg_ref[...])
        x2 = sq.x1 + gate * sq.e
        y_ref[bi] = _rmsnorm(x2, fin_g_ref[...])

    units = iter([(c, hh) for c in range(n_chunks) for hh in range(H_B)])
    pairs = iter([(j, hh) for j in range(rows_per_step) for hh in range(H_B)])
    stages = {
        "norm": stage_norm, "conv": stage_conv, "u": stage_u, "qkv": stage_qkv, "v": stage_v,
        "gate_o": stage_gate_o, "gate_z": stage_gate_z, "za": stage_za,
        "heads_lo": lambda sq, bi: gmlp_heads(sq, bi, list(range(0, H_A // 2))),
        "heads_hi": lambda sq, bi: gmlp_heads(sq, bi, list(range(H_A // 2, H_A))),
        "unit": lambda sq, bi: mlstm_unit(sq, bi, *next(units)),
        "pair": lambda sq, bi: sample_state_pair(*next(pairs)),
        "ple": stage_ple, "merge_a": stage_merge_a, "merge_b": stage_merge_b, "out": stage_out,
    }
    assert STEP_ORDER.count("unit") == n_chunks * H_B and STEP_ORDER.count("pair") == rows_per_step * H_B
    for name in STEP_ORDER:
        stages[name](seqs[0], 0)


def _resident(shape):
    nd = len(shape)
    return pl.BlockSpec(shape, lambda *_: (0,) * nd, pipeline_mode=pl.Buffered(1))


def _prompt_call(x, p, weights, sample_c, sample_rows):
    batch, seq, _ = x.shape
    tt = PROMPT_TILE
    ns = SEQS_PER_STEP
    grid = (batch // ns, seq // tt)
    n_steps = grid[0] * grid[1]
    n_rows = sample_c.shape[0]
    rows_per_step = n_rows // n_steps
    assert rows_per_step * n_steps == n_rows, "sample rows are spread evenly over the prompt grid steps"

    def state_block(b, t):
        return (b * grid[1] + t, 0, 0, 0)

    in_specs = [
        pl.BlockSpec((ns, tt, D_MODEL), lambda b, t: (b, t, 0)),
        pl.BlockSpec((ns, tt, D_PLE), lambda b, t: (b, t, 0)),
    ] + [_resident(w.shape) for w in weights] + [
        pl.BlockSpec((rows_per_step, H_B, DH_B, DH_B), state_block),
    ] + [_resident(a.shape) for a in sample_rows]
    out_shape = (
        jax.ShapeDtypeStruct((batch, seq, D_MODEL), F32),
        jax.ShapeDtypeStruct((batch, H_B, DH_B, DH_B), F32),
        jax.ShapeDtypeStruct((batch, H_B, DH_B), F32),
        jax.ShapeDtypeStruct((batch, H_B, SUBLANES, LANES), F32),
        jax.ShapeDtypeStruct((batch, CONV_W - 1, W_B), F32),
        jax.ShapeDtypeStruct(sample_c.shape, F32),
        jax.ShapeDtypeStruct((n_rows, W_B), F32),
    )
    out_specs = (
        pl.BlockSpec((ns, tt, D_MODEL), lambda b, t: (b, t, 0)),
        pl.BlockSpec((ns, H_B, DH_B, DH_B), lambda b, t: (b, 0, 0, 0)),
        pl.BlockSpec((ns, H_B, DH_B), lambda b, t: (b, 0, 0)),
        pl.BlockSpec((ns, H_B, SUBLANES, LANES), lambda b, t: (b, 0, 0, 0)),
        pl.BlockSpec((ns, CONV_W - 1, W_B), lambda b, t: (b, 0, 0)),
        pl.BlockSpec((rows_per_step, H_B, DH_B, DH_B), state_block),
        pl.BlockSpec((n_rows, W_B), lambda b, t: (0, 0)),
    )
    scratch = [
        pltpu.VMEM((ns, tt + SUBLANES, W_B), F32),
        pltpu.VMEM((ns, tt, 3 * W_B), BF16),
        pltpu.VMEM((ns, tt, W_A + W_B), BF16),
    ]
    return pl.pallas_call(
        _prompt_kernel,
        grid=grid,
        in_specs=in_specs,
        out_specs=out_specs,
        out_shape=out_shape,
        scratch_shapes=scratch,
        compiler_params=pltpu.CompilerParams(
            dimension_semantics=("arbitrary", "arbitrary"),
            vmem_limit_bytes=PROMPT_VMEM_BYTES),
        name="prompt_fused",
    )(x, p, *weights, sample_c, *sample_rows)


def _sample_pre_kernel(x_ref, conv_ref, n_ref, m_ref, norm_g_ref, w_in_ref, lnv_g_ref, lnv_b_ref,
                       w00_ref, b0_ref, conv_w_ref, conv_b_ref, w_q_ref, w_k_ref, w_v_ref, w_if_ref,
                       b_if_ref, skip_ref,
                       vn_ref, ya_ref, convnew_ref, q_ref, k_ref, vs_ref, gfull_ref, numa_ref,
                       denom_ref, osig_ref, skipxc_ref, zsilu_ref, nnew_ref, mnew_ref,
                       qkv_ref):
    x = x_ref[:, 0, :]
    hb = _rmsnorm(x, norm_g_ref[...]).astype(BF16)

    def proj(i):
        return _dot(hb, w_in_ref[:, i * W_A:(i + 1) * W_A])

    u = _gelu(proj(0))
    v = _gelu(proj(1))
    z_a = proj(2)
    for hh in range(H_A):
        sl = slice(hh * DH_A, (hh + 1) * DH_A)
        vn = _layernorm(v[:, sl], lnv_g_ref[:, sl], lnv_b_ref[:, sl])
        vn_ref[:, 0, sl] = vn
        sp = vn * w00_ref[:, sl] + b0_ref[:, sl]
        ya_ref[:, sl] = (u[:, sl] * sp * _silu(z_a[:, sl])).astype(BF16)

    x_b = proj(3)
    osig_ref[...] = jax.nn.sigmoid(proj(4))
    zsilu_ref[...] = _silu(proj(5))
    c0 = conv_ref[:, 0, :]
    c1 = conv_ref[:, 1, :]
    c2 = conv_ref[:, 2, :]
    xc = c0 * conv_w_ref[0:1, :]
    xc = xc + c1 * conv_w_ref[1:2, :]
    xc = xc + c2 * conv_w_ref[2:3, :]
    xc = xc + x_b * conv_w_ref[3:4, :]
    xc = _silu(xc + conv_b_ref[...])
    convnew_ref[:, 0, :] = c1
    convnew_ref[:, 1, :] = c2
    convnew_ref[:, 2, :] = x_b
    skipxc_ref[...] = skip_ref[...] * xc
    xcb = xc.astype(BF16)
    xbb = x_b.astype(BF16)
    for hh in range(H_B):
        sl = slice(hh * DH_B, (hh + 1) * DH_B)
        qkv_ref[:, hh * DH_B:(hh + 1) * DH_B] = _dot(xcb[:, sl], w_q_ref[hh]).astype(BF16)
        qkv_ref[:, W_B + hh * DH_B:W_B + (hh + 1) * DH_B] = _dot(xcb[:, sl], w_k_ref[hh]).astype(BF16)
        qkv_ref[:, 2 * W_B + hh * DH_B:2 * W_B + (hh + 1) * DH_B] = _dot(xbb[:, sl], w_v_ref[hh]).astype(BF16)
    gates = _dot(qkv_ref[...], w_if_ref[...]) + b_if_ref[...]
    logf = pltpu.roll(jax.nn.log_sigmoid(gates), LANES - H_B, 1)
    li = gates[:, 0:H_B]
    inter = logf[:, 0:H_B] + m_ref[...]
    m_t = jnp.maximum(inter, li)
    s_w = jnp.exp(li - m_t)
    g = jnp.exp(inter - m_t)
    e_neg = jnp.exp(-m_t)
    mnew_ref[...] = m_t
    for hh in range(H_B):
        sl = slice(hh * DH_B, (hh + 1) * DH_B)
        qf = qkv_ref[:, hh * DH_B:(hh + 1) * DH_B].astype(F32)
        kf = qkv_ref[:, W_B + hh * DH_B:W_B + (hh + 1) * DH_B].astype(F32)
        vf = qkv_ref[:, 2 * W_B + hh * DH_B:2 * W_B + (hh + 1) * DH_B].astype(F32)
        s_h = s_w[:, hh:hh + 1]
        g_h = g[:, hh:hh + 1]
        n_old = n_ref[:, hh, :]
        s_qk = s_h * jnp.sum(qf * kf, axis=-1, keepdims=True)
        den = s_qk + g_h * jnp.sum(n_old * qf, axis=-1, keepdims=True)
        denom = jnp.maximum(jnp.abs(den), e_neg[:, hh:hh + 1])
        q_ref[:, sl] = qf
        k_ref[:, sl] = kf
        vs_ref[:, sl] = s_h * vf
        gfull_ref[:, sl] = jnp.broadcast_to(g_h, qf.shape)
        numa_ref[:, sl] = s_qk * vf
        denom_ref[:, sl] = jnp.broadcast_to(denom, qf.shape)
        nnew_ref[:, hh, :] = g_h * n_old + s_h * kf


def _sample_post_kernel(x_ref, p_ref, ya_ref, cq_ref, gfull_ref, numa_ref, denom_ref, osig_ref,
                        skipxc_ref, zsilu_ref, gn_g_ref, w_out_ref, w_pg_ref, b_pg_ref, w_pp_ref,
                        ple_g_ref, fin_g_ref, y_ref, ycat_ref):
    ycat_ref[:, 0:W_A] = ya_ref[...]
    for hh in range(H_B):
        sl = slice(hh * DH_B, (hh + 1) * DH_B)
        hcell = (numa_ref[:, sl] + gfull_ref[:, sl] * cq_ref[:, sl]) / denom_ref[:, sl]
        hn = _layernorm(hcell * osig_ref[:, sl], gn_g_ref[:, sl])
        ycat_ref[:, W_A + hh * DH_B:W_A + (hh + 1) * DH_B] = ((hn + skipxc_ref[:, sl]) * zsilu_ref[:, sl]).astype(BF16)
    x1 = x_ref[:, 0, :] + _dot(ycat_ref[...], w_out_ref[...])
    e = _rmsnorm(_dot(p_ref[:, 0, :].astype(BF16), w_pp_ref[...]), ple_g_ref[...])
    gate = jax.nn.sigmoid(_dot(x1.astype(BF16), w_pg_ref[...]) + b_pg_ref[...])
    x2 = x1 + gate * e
    y_ref[:, 0, :] = _rmsnorm(x2, fin_g_ref[...])


def _single_step_call(body, inputs, out_shape, scratch, name):
    return pl.pallas_call(
        body,
        grid=(1,),
        in_specs=[_resident(a.shape) for a in inputs],
        out_specs=tuple(pl.BlockSpec(s.shape, lambda *_, nd=len(s.shape): (0,) * nd) for s in out_shape),
        out_shape=out_shape,
        scratch_shapes=scratch,
        compiler_params=pltpu.CompilerParams(
            dimension_semantics=("arbitrary",), vmem_limit_bytes=SAMPLE_VMEM_BYTES),
        name=name,
    )(*inputs)


def kernel(x_prompt, x_sample, state_mlstm_C, state_mlstm_n, state_mlstm_m, state_conv, p_prompt, p_sample,
           norm_in_g, w_in, ln_v_g, ln_v_b, w_spatial, b_spatial, conv_w, conv_b, w_q, w_k, w_v, w_if, b_if,
           gn_g, skip, w_out, w_ple_gate, b_ple_gate, w_ple_proj, ple_norm_g, final_norm_g):
    assert norm_in_g.shape[0] == 1, "single-layer trunk"
    nb = x_sample.shape[0]
    row = lambda a: a.reshape(1, -1).astype(F32)

    norm_g = row(norm_in_g[0])
    w_in_b = w_in[0].astype(BF16)
    lnv_g = row(ln_v_g[0])
    lnv_b = row(ln_v_b[0])
    w_sp = w_spatial[0]
    b_sp_full = jnp.repeat(b_spatial[0].T, DH_A, axis=1)
    cw = conv_w[0]
    cb = row(conv_b[0])
    wq_b = w_q[0].astype(BF16)
    wk_b = (w_k[0] * (DH_B ** -0.5)).astype(BF16)
    wv_b = w_v[0].astype(BF16)
    k_fix = jnp.concatenate([jnp.ones((W_B, 1), F32), jnp.full((W_B, 1), DH_B ** 0.5, F32), jnp.ones((W_B, 1), F32)], 0)
    w_if_p = jnp.pad(w_if[0] * k_fix, ((0, 0), (0, LANES - 2 * H_B))).astype(BF16)
    b_if_p = jnp.pad(b_if[0], (0, LANES - 2 * H_B)).reshape(1, LANES)
    gn = row(gn_g[0])
    sk = row(skip[0])
    w_out_b = w_out[0].astype(BF16)
    w_pg_b = w_ple_gate[0].astype(BF16)
    b_pg = row(b_ple_gate[0])
    w_pp_b = w_ple_proj[0].astype(BF16)
    ple_g = row(ple_norm_g[0])
    fin_g = row(final_norm_g)

    xs = x_sample
    ps = p_sample[0]
    conv_s = state_conv[0]
    n_s = state_mlstm_n[0]
    m_s = state_mlstm_m[0]
    w00 = jnp.repeat(w_sp[:, 0, 0], DH_A).reshape(1, W_A)
    b0 = jnp.repeat(b_spatial[0][:, 0], DH_A).reshape(1, W_A)
    full = jax.ShapeDtypeStruct((nb, W_B), F32)
    pre_out = (
        jax.ShapeDtypeStruct((nb, 1, W_A), F32),
        jax.ShapeDtypeStruct((nb, W_A), BF16),
        jax.ShapeDtypeStruct((nb, CONV_W - 1, W_B), F32),
        full, full, full, full, full, full, full, full, full,
        jax.ShapeDtypeStruct((nb, H_B, DH_B), F32),
        jax.ShapeDtypeStruct((nb, H_B), F32),
    )
    (vn_s, ya_s, conv_new, q_s, k_s, vs_s, gfull, numa, denom, osig, skipxc, zsilu, n_new, m_new) = _single_step_call(
        _sample_pre_kernel,
        [xs, conv_s, n_s, m_s, norm_g, w_in_b, lnv_g, lnv_b, w00, b0, cw, cb, wq_b, wk_b, wv_b, w_if_p, b_if_p, sk],
        pre_out, [pltpu.VMEM((nb, 3 * W_B), BF16)], "sample_pre")

    prompt_weights = [norm_g, w_in_b, lnv_g, lnv_b, w_sp, b_sp_full, cw, cb, wq_b, wk_b, wv_b, w_if_p, b_if_p,
                      gn, sk, w_out_b, w_pg_b, b_pg, w_pp_b, ple_g, fin_g]
    y_p, c_p, n_p, m_p, conv_p, c_new, cq = _prompt_call(
        x_prompt, p_prompt[0], prompt_weights, state_mlstm_C[0], [q_s, k_s, vs_s, gfull])

    (y_s,) = _single_step_call(
        _sample_post_kernel,
        [xs, ps, ya_s, cq, gfull, numa, denom, osig, skipxc, zsilu, gn, w_out_b, w_pg_b, b_pg, w_pp_b, ple_g, fin_g],
        (jax.ShapeDtypeStruct((nb, 1, D_MODEL), F32),), [pltpu.VMEM((nb, W_A + W_B), BF16)], "sample_post")

    return (y_p, y_s, c_p[None], n_p[None], m_p[:, :, 0, 0][None], conv_p[None],
            c_new[None], n_new[None], m_new[None], conv_new[None], vn_s[None])
```

```python
import jax
import jax.numpy as jnp
from jax import lax
from jax.experimental import pallas as pl
from jax.experimental.pallas import tpu as pltpu

D_MODEL = 1024
W_A = 1024
H_A = 8
DH_A = 128
W_B = 1024
H_B = 4
DH_B = 256
CHUNK = 128
CONV_W = 4
D_PLE = 256
N_IN = 3 * W_A + 3 * W_B
EPS = 1e-6

LANES = 128
SUBLANES = 8
PROMPT_TILE = 256
SEQS_PER_STEP = 1
STEP_ORDER = (
    "norm", "conv", "u", "v", "za", "qkv", "heads_lo", "gate_o", "heads_hi", "gate_z",
    "unit", "ple", "unit", "unit", "unit", "unit", "merge_a", "unit", "unit", "unit",
    "merge_b", "pair", "pair", "pair", "pair", "out", "pair", "pair", "pair", "pair",
)
PROMPT_VMEM_BYTES = 56 * 1024 * 1024
SAMPLE_VMEM_BYTES = 48 * 1024 * 1024

F32 = jnp.float32
BF16 = jnp.bfloat16
NT_DIMS = (((1,), (1,)), ((), ()))
TN_DIMS = (((0,), (0,)), ((), ()))


def _rmsnorm(x, g):
    return x * lax.rsqrt(jnp.mean(x * x, axis=-1, keepdims=True) + EPS) * g


def _layernorm(x, g, b=None):
    mu = jnp.mean(x, axis=-1, keepdims=True)
    d = x - mu
    var = jnp.mean(d * d, axis=-1, keepdims=True)
    y = d * lax.rsqrt(var + EPS) * g
    return y if b is None else y + b


def _gelu(x):
    return 0.5 * x * (1.0 + lax.erf(x * (0.5 ** 0.5)))


def _silu(x):
    return x * jax.nn.sigmoid(x)


def _dot(a, b):
    return jnp.dot(a, b, preferred_element_type=F32)


def _cumsum_rows(tril_b, x):
    x1 = x.astype(BF16)
    r1 = x - x1.astype(F32)
    x2 = r1.astype(BF16)
    x3 = (r1 - x2.astype(F32)).astype(BF16)
    return _dot(tril_b, x1) + _dot(tril_b, x2) + _dot(tril_b, x3)


def _prompt_kernel(x_ref, p_ref, norm_g_ref, w_in_ref, lnv_g_ref, lnv_b_ref, w_sp_ref, b_sp_ref,
                   conv_w_ref, conv_b_ref, w_q_ref, w_k_ref, w_v_ref, w_if_ref, b_if_ref,
                   gn_g_ref, skip_ref, w_out_ref, w_pg_ref, b_pg_ref, w_pp_ref, ple_g_ref, fin_g_ref,
                   sc_ref, sq_ref, sk_ref, svs_ref, sg_ref,
                   y_ref, c_ref, n_ref, m_ref, conv_ref, scnew_ref, scq_ref,
                   xpad_ref, qkv_ref, ycat_ref):
    n_seq, tt = x_ref.shape[0], x_ref.shape[1]
    n_chunks = tt // CHUNK
    step = pl.program_id(0) * pl.num_programs(1) + pl.program_id(1)

    @pl.when(pl.program_id(1) == 0)
    def _():
        c_ref[...] = jnp.zeros_like(c_ref)
        n_ref[...] = jnp.zeros_like(n_ref)
        m_ref[...] = jnp.zeros_like(m_ref)
        conv_ref[...] = jnp.zeros_like(conv_ref)

    @pl.when(step == 0)
    def _():
        scq_ref[...] = jnp.zeros_like(scq_ref)

    rows_per_step = sc_ref.shape[0]
    group_row = lax.broadcasted_iota(jnp.int32, (SUBLANES, DH_B), 0)

    def sample_state_pair(j, hh):
        sl = slice(hh * DH_B, (hh + 1) * DH_B)
        r = step * rows_per_step + j
        base = pl.multiple_of((r // SUBLANES) * SUBLANES, SUBLANES)
        is_row = group_row == (r - base)
        group = pl.ds(base, SUBLANES)
        c_old = sc_ref[j, hh]
        cq = lax.dot_general(sq_ref[group, sl].astype(BF16), c_old.astype(BF16), NT_DIMS, preferred_element_type=F32)
        scq_ref[group, sl] = jnp.where(is_row, cq, scq_ref[group, sl])
        k_row = jnp.where(is_row, sk_ref[group, sl], 0.0).astype(BF16)
        outer = lax.dot_general(svs_ref[group, sl].astype(BF16), k_row, TN_DIMS, preferred_element_type=F32)
        scnew_ref[j, hh] = sg_ref[pl.ds(r, 1), sl] * c_old + outer

    row = lax.broadcasted_iota(jnp.int32, (CHUNK, CHUNK), 0)
    col = lax.broadcasted_iota(jnp.int32, (CHUNK, CHUNK), 1)
    causal = row >= col
    tril = causal.astype(F32)
    tril_b = causal.astype(BF16)

    class Seq:
        pass

    seqs = [Seq() for _ in range(n_seq)]

    def proj(sq, i):
        return _dot(sq.hb, w_in_ref[:, i * W_A:(i + 1) * W_A])

    def stage_norm(sq, bi):
        sq.hb = _rmsnorm(x_ref[bi], norm_g_ref[...]).astype(BF16)

    def stage_u(sq, bi):
        sq.u = _gelu(proj(sq, 0))

    def stage_v(sq, bi):
        sq.v = _gelu(proj(sq, 1))

    def stage_za(sq, bi):
        sq.z_a = proj(sq, 2)

    def gmlp_heads(sq, bi, heads):
        for hh in heads:
            sl = slice(hh * DH_A, (hh + 1) * DH_A)
            vn = _layernorm(sq.v[:, sl], lnv_g_ref[:, sl], lnv_b_ref[:, sl]).astype(BF16)
            wm = (w_sp_ref[hh] * tril).astype(BF16)
            vn_wide = jnp.concatenate([vn[c * CHUNK:(c + 1) * CHUNK] for c in range(n_chunks)], axis=1)
            sp_wide = _dot(wm, vn_wide)
            for c in range(n_chunks):
                rows = slice(c * CHUNK, (c + 1) * CHUNK)
                sp = sp_wide[:, c * DH_A:(c + 1) * DH_A] + b_sp_ref[:, sl]
                ycat_ref[bi, rows, sl] = (sq.u[rows, sl] * sp * _silu(sq.z_a[rows, sl])).astype(BF16)

    def stage_conv(sq, bi):
        x_b = proj(sq, 3)
        xpad_ref[bi, SUBLANES - 3:SUBLANES, :] = conv_ref[bi]
        xpad_ref[bi, SUBLANES:SUBLANES + tt, :] = x_b
        conv_ref[bi] = x_b[tt - 3:tt, :]
        xc = xpad_ref[bi, SUBLANES - 3:SUBLANES - 3 + tt, :] * conv_w_ref[0:1, :]
        xc = xc + xpad_ref[bi, SUBLANES - 2:SUBLANES - 2 + tt, :] * conv_w_ref[1:2, :]
        xc = xc + xpad_ref[bi, SUBLANES - 1:SUBLANES - 1 + tt, :] * conv_w_ref[2:3, :]
        xc = xc + x_b * conv_w_ref[3:4, :]
        sq.xc = _silu(xc + conv_b_ref[...])
        sq.xcb = sq.xc.astype(BF16)
        sq.xbb = x_b.astype(BF16)

    def stage_gate_o(sq, bi):
        sq.o_sig = jax.nn.sigmoid(proj(sq, 4))

    def stage_gate_z(sq, bi):
        sq.z_silu = _silu(proj(sq, 5))

    def stage_qkv(sq, bi):
        for hh in range(H_B):
            sl = slice(hh * DH_B, (hh + 1) * DH_B)
            qkv_ref[bi, :, hh * DH_B:(hh + 1) * DH_B] = _dot(sq.xcb[:, sl], w_q_ref[hh]).astype(BF16)
            qkv_ref[bi, :, W_B + hh * DH_B:W_B + (hh + 1) * DH_B] = _dot(sq.xcb[:, sl], w_k_ref[hh]).astype(BF16)
            qkv_ref[bi, :, 2 * W_B + hh * DH_B:2 * W_B + (hh + 1) * DH_B] = _dot(sq.xbb[:, sl], w_v_ref[hh]).astype(BF16)
        gates = _dot(qkv_ref[bi], w_if_ref[...]) + b_if_ref[...]
        logf = jax.nn.log_sigmoid(gates)
        sq.decay = []
        for c in range(n_chunks):
            rows = slice(c * CHUNK, (c + 1) * CHUNK)
            bcum = pltpu.roll(_cumsum_rows(tril_b, logf[rows]), LANES - H_B, 1)
            a_all = gates[rows] - bcum
            sq.decay.append((bcum, a_all, a_all.T))

    def mlstm_unit(sq, bi, c, hh):
        rows = slice(c * CHUNK, (c + 1) * CHUNK)
        sl = slice(hh * DH_B, (hh + 1) * DH_B)
        bcum, a_all, a_t = sq.decay[c]
        qh = qkv_ref[bi, rows, hh * DH_B:(hh + 1) * DH_B]
        kh = qkv_ref[bi, rows, W_B + hh * DH_B:W_B + (hh + 1) * DH_B]
        vh = qkv_ref[bi, rows, 2 * W_B + hh * DH_B:2 * W_B + (hh + 1) * DH_B]
        m_prev = m_ref[bi, hh, 0:1, 0:1]
        a_mat = jnp.where(causal, a_t[hh:hh + 1, :], -jnp.inf)
        m_run = jnp.maximum(jnp.max(a_mat, axis=-1, keepdims=True), m_prev)
        s = jnp.exp(a_mat - m_run) * lax.dot_general(qh, kh, NT_DIMS, preferred_element_type=F32)
        g = jnp.exp(m_prev - m_run)
        c_old = c_ref[bi, hh]
        inter = lax.dot_general(qh, c_old.astype(BF16), NT_DIMS, preferred_element_type=F32)
        num = _dot(s.astype(BF16), vh) + g * inter
        n_old = n_ref[bi, hh:hh + 1, :]
        den = jnp.sum(s, axis=-1, keepdims=True) + g * jnp.sum(qh.astype(F32) * n_old, axis=-1, keepdims=True)
        m_t = bcum[:, hh:hh + 1] + m_run
        hcell = num / jnp.maximum(jnp.abs(den), jnp.exp(-m_t))
        m_last = m_run[CHUNK - 1:CHUNK, :]
        w_end = jnp.exp(a_all[:, hh:hh + 1] - m_last)
        g_end = jnp.exp(m_prev - m_last)
        kw = kh.astype(F32) * w_end
        c_ref[bi, hh] = g_end * c_old + lax.dot_general(vh, kw.astype(BF16), TN_DIMS, preferred_element_type=F32)
        n_ref[bi, hh:hh + 1, :] = g_end * n_old + jnp.sum(kw, axis=0, keepdims=True)
        m_ref[bi, hh] = jnp.broadcast_to(m_t[CHUNK - 1:CHUNK, :], (SUBLANES, LANES))
        hn = _layernorm(hcell * sq.o_sig[rows, sl], gn_g_ref[:, sl])
        yb = (hn + skip_ref[:, sl] * sq.xc[rows, sl]) * sq.z_silu[rows, sl]
        ycat_ref[bi, rows, W_A + hh * DH_B:W_A + (hh + 1) * DH_B] = yb.astype(BF16)

    def stage_ple(sq, bi):
        sq.e = _rmsnorm(_dot(p_ref[bi].astype(BF16), w_pp_ref[...]), ple_g_ref[...])

    def stage_merge_a(sq, bi):
        sq.x1 = x_ref[bi] + _dot(ycat_ref[bi, :, :W_A], w_out_ref[:W_A, :])

    def stage_merge_b(sq, bi):
        sq.x1 = sq.x1 + _dot(ycat_ref[bi, :, W_A:], w_out_ref[W_A:, :])

    def stage_out(sq, bi):
        gate = jax.nn.sigmoid(_dot(sq.x1.astype(BF16), w_pg_ref[...]) + b_pg_ref[...])
        x2 = sq.x1 + gate * sq.e
        y_ref[bi] = _rmsnorm(x2, fin_g_ref[...])

    units = iter([(c, hh) for c in range(n_chunks) for hh in range(H_B)])
    pairs = iter([(j, hh) for j in range(rows_per_step) for hh in range(H_B)])
    stages = {
        "norm": stage_norm, "conv": stage_conv, "u": stage_u, "qkv": stage_qkv, "v": stage_v,
        "gate_o": stage_gate_o, "gate_z": stage_gate_z, "za": stage_za,
        "heads_lo": lambda sq, bi: gmlp_heads(sq, bi, list(range(0, H_A // 2))),
        "heads_hi": lambda sq, bi: gmlp_heads(sq, bi, list(range(H_A // 2, H_A))),
        "unit": lambda sq, bi: mlstm_unit(sq, bi, *next(units)),
        "pair": lambda sq, bi: sample_state_pair(*next(pairs)),
        "ple": stage_ple, "merge_a": stage_merge_a, "merge_b": stage_merge_b, "out": stage_out,
    }
    assert STEP_ORDER.count("unit") == n_chunks * H_B and STEP_ORDER.count("pair") == rows_per_step * H_B
    for name in STEP_ORDER:
        stages[name](seqs[0], 0)


def _resident(shape):
    nd = len(shape)
    return pl.BlockSpec(shape, lambda *_: (0,) * nd, pipeline_mode=pl.Buffered(1))


def _prompt_call(x, p, weights, sample_c, sample_rows):
    batch, seq, _ = x.shape
    tt = PROMPT_TILE
    ns = SEQS_PER_STEP
    grid = (batch // ns, seq // tt)
    n_steps = grid[0] * grid[1]
    n_rows = sample_c.shape[0]
    rows_per_step = n_rows // n_steps
    assert rows_per_step * n_steps == n_rows, "sample rows are spread evenly over the prompt grid steps"

    def state_block(b, t):
        return (b * grid[1] + t, 0, 0, 0)

    in_specs = [
        pl.BlockSpec((ns, tt, D_MODEL), lambda b, t: (b, t, 0)),
        pl.BlockSpec((ns, tt, D_PLE), lambda b, t: (b, t, 0)),
    ] + [_resident(w.shape) for w in weights] + [
        pl.BlockSpec((rows_per_step, H_B, DH_B, DH_B), state_block),
    ] + [_resident(a.shape) for a in sample_rows]
    out_shape = (
        jax.ShapeDtypeStruct((batch, seq, D_MODEL), F32),
        jax.ShapeDtypeStruct((batch, H_B, DH_B, DH_B), F32),
        jax.ShapeDtypeStruct((batch, H_B, DH_B), F32),
        jax.ShapeDtypeStruct((batch, H_B, SUBLANES, LANES), F32),
        jax.ShapeDtypeStruct((batch, CONV_W - 1, W_B), F32),
        jax.ShapeDtypeStruct(sample_c.shape, F32),
        jax.ShapeDtypeStruct((n_rows, W_B), F32),
    )
    out_specs = (
        pl.BlockSpec((ns, tt, D_MODEL), lambda b, t: (b, t, 0)),
        pl.BlockSpec((ns, H_B, DH_B, DH_B), lambda b, t: (b, 0, 0, 0)),
        pl.BlockSpec((ns, H_B, DH_B), lambda b, t: (b, 0, 0)),
        pl.BlockSpec((ns, H_B, SUBLANES, LANES), lambda b, t: (b, 0, 0, 0)),
        pl.BlockSpec((ns, CONV_W - 1, W_B), lambda b, t: (b, 0, 0)),
        pl.BlockSpec((rows_per_step, H_B, DH_B, DH_B), state_block),
        pl.BlockSpec((n_rows, W_B), lambda b, t: (0, 0)),
    )
    scratch = [
        pltpu.VMEM((ns, tt + SUBLANES, W_B), F32),
        pltpu.VMEM((ns, tt, 3 * W_B), BF16),
        pltpu.VMEM((ns, tt, W_A + W_B), BF16),
    ]
    return pl.pallas_call(
        _prompt_kernel,
        grid=grid,
        in_specs=in_specs,
        out_specs=out_specs,
        out_shape=out_shape,
        scratch_shapes=scratch,
        compiler_params=pltpu.CompilerParams(
            dimension_semantics=("arbitrary", "arbitrary"),
            vmem_limit_bytes=PROMPT_VMEM_BYTES),
        name="prompt_fused",
    )(x, p, *weights, sample_c, *sample_rows)


def _sample_pre_kernel(x_ref, conv_ref, n_ref, m_ref, norm_g_ref, w_in_ref, lnv_g_ref, lnv_b_ref,
                       w00_ref, b0_ref, conv_w_ref, conv_b_ref, w_q_ref, w_k_ref, w_v_ref, w_if_ref,
                       b_if_ref, skip_ref,
                       vn_ref, ya_ref, convnew_ref, q_ref, k_ref, vs_ref, gfull_ref, numa_ref,
                       denom_ref, osig_ref, skipxc_ref, zsilu_ref, nnew_ref, mnew_ref, w_in_bf16_ref,
                       qkv_ref, hb_ref, proj_ref):
    slab = pl.program_id(0)

    @pl.when(slab == 0)
    def _():
        hb_ref[...] = _rmsnorm(x_ref[:, 0, :], norm_g_ref[...]).astype(BF16)

    w_slab = w_in_ref[...].astype(BF16)
    w_in_bf16_ref[...] = w_slab
    proj_ref[slab] = _dot(hb_ref[...], w_slab)

    @pl.when(slab == pl.num_programs(0) - 1)
    def _():
        _sample_front_end(conv_ref, n_ref, m_ref, lnv_g_ref, lnv_b_ref, w00_ref, b0_ref, conv_w_ref, conv_b_ref,
                          w_q_ref, w_k_ref, w_v_ref, w_if_ref, b_if_ref, skip_ref,
                          vn_ref, ya_ref, convnew_ref, q_ref, k_ref, vs_ref, gfull_ref, numa_ref,
                          denom_ref, osig_ref, skipxc_ref, zsilu_ref, nnew_ref, mnew_ref, qkv_ref, proj_ref)


def _sample_front_end(conv_ref, n_ref, m_ref, lnv_g_ref, lnv_b_ref, w00_ref, b0_ref, conv_w_ref, conv_b_ref,
                      w_q_ref, w_k_ref, w_v_ref, w_if_ref, b_if_ref, skip_ref,
                      vn_ref, ya_ref, convnew_ref, q_ref, k_ref, vs_ref, gfull_ref, numa_ref,
                      denom_ref, osig_ref, skipxc_ref, zsilu_ref, nnew_ref, mnew_ref, qkv_ref, proj_ref):
    def proj(i):
        return proj_ref[i]

    u = _gelu(proj(0))
    v = _gelu(proj(1))
    z_a = proj(2)
    for hh in range(H_A):
        sl = slice(hh * DH_A, (hh + 1) * DH_A)
        vn = _layernorm(v[:, sl], lnv_g_ref[:, sl], lnv_b_ref[:, sl])
        vn_ref[:, 0, sl] = vn
        sp = vn * w00_ref[:, sl] + b0_ref[:, sl]
        ya_ref[:, sl] = (u[:, sl] * sp * _silu(z_a[:, sl])).astype(BF16)

    x_b = proj(3)
    osig_ref[...] = jax.nn.sigmoid(proj(4))
    zsilu_ref[...] = _silu(proj(5))
    c0 = conv_ref[:, 0, :]
    c1 = conv_ref[:, 1, :]
    c2 = conv_ref[:, 2, :]
    xc = c0 * conv_w_ref[0:1, :]
    xc = xc + c1 * conv_w_ref[1:2, :]
    xc = xc + c2 * conv_w_ref[2:3, :]
    xc = xc + x_b * conv_w_ref[3:4, :]
    xc = _silu(xc + conv_b_ref[...])
    convnew_ref[:, 0, :] = c1
    convnew_ref[:, 1, :] = c2
    convnew_ref[:, 2, :] = x_b
    skipxc_ref[...] = skip_ref[...] * xc
    xcb = xc.astype(BF16)
    xbb = x_b.astype(BF16)
    for hh in range(H_B):
        sl = slice(hh * DH_B, (hh + 1) * DH_B)
        qkv_ref[:, hh * DH_B:(hh + 1) * DH_B] = _dot(xcb[:, sl], w_q_ref[hh]).astype(BF16)
        qkv_ref[:, W_B + hh * DH_B:W_B + (hh + 1) * DH_B] = _dot(xcb[:, sl], w_k_ref[hh]).astype(BF16)
        qkv_ref[:, 2 * W_B + hh * DH_B:2 * W_B + (hh + 1) * DH_B] = _dot(xbb[:, sl], w_v_ref[hh]).astype(BF16)
    gates = _dot(qkv_ref[...], w_if_ref[...]) + b_if_ref[...]
    logf = pltpu.roll(jax.nn.log_sigmoid(gates), LANES - H_B, 1)
    li = gates[:, 0:H_B]
    inter = logf[:, 0:H_B] + m_ref[...]
    m_t = jnp.maximum(inter, li)
    s_w = jnp.exp(li - m_t)
    g = jnp.exp(inter - m_t)
    e_neg = jnp.exp(-m_t)
    mnew_ref[...] = m_t
    for hh in range(H_B):
        sl = slice(hh * DH_B, (hh + 1) * DH_B)
        qf = qkv_ref[:, hh * DH_B:(hh + 1) * DH_B].astype(F32)
        kf = qkv_ref[:, W_B + hh * DH_B:W_B + (hh + 1) * DH_B].astype(F32)
        vf = qkv_ref[:, 2 * W_B + hh * DH_B:2 * W_B + (hh + 1) * DH_B].astype(F32)
        s_h = s_w[:, hh:hh + 1]
        g_h = g[:, hh:hh + 1]
        n_old = n_ref[:, hh, :]
        s_qk = s_h * jnp.sum(qf * kf, axis=-1, keepdims=True)
        den = s_qk + g_h * jnp.sum(n_old * qf, axis=-1, keepdims=True)
        denom = jnp.maximum(jnp.abs(den), e_neg[:, hh:hh + 1])
        q_ref[:, sl] = qf
        k_ref[:, sl] = kf
        vs_ref[:, sl] = s_h * vf
        gfull_ref[:, sl] = jnp.broadcast_to(g_h, qf.shape)
        numa_ref[:, sl] = s_qk * vf
        denom_ref[:, sl] = jnp.broadcast_to(denom, qf.shape)
        nnew_ref[:, hh, :] = g_h * n_old + s_h * kf


def _sample_post_kernel(x_ref, p_ref, ya_ref, cq_ref, gfull_ref, numa_ref, denom_ref, osig_ref,
                        skipxc_ref, zsilu_ref, gn_g_ref, w_out_ref, w_pg_ref, b_pg_ref, w_pp_ref,
                        ple_g_ref, fin_g_ref, y_ref, ycat_ref):
    ycat_ref[:, 0:W_A] = ya_ref[...]
    for hh in range(H_B):
        sl = slice(hh * DH_B, (hh + 1) * DH_B)
        hcell = (numa_ref[:, sl] + gfull_ref[:, sl] * cq_ref[:, sl]) / denom_ref[:, sl]
        hn = _layernorm(hcell * osig_ref[:, sl], gn_g_ref[:, sl])
        ycat_ref[:, W_A + hh * DH_B:W_A + (hh + 1) * DH_B] = ((hn + skipxc_ref[:, sl]) * zsilu_ref[:, sl]).astype(BF16)
    x1 = x_ref[:, 0, :] + _dot(ycat_ref[...], w_out_ref[...])
    e = _rmsnorm(_dot(p_ref[:, 0, :].astype(BF16), w_pp_ref[...]), ple_g_ref[...])
    gate = jax.nn.sigmoid(_dot(x1.astype(BF16), w_pg_ref[...]) + b_pg_ref[...])
    x2 = x1 + gate * e
    y_ref[:, 0, :] = _rmsnorm(x2, fin_g_ref[...])


def _sample_pre_call(before_w, w_in_f32, after_w, out_shape):
    nb = before_w[0].shape[0]
    n_slabs = N_IN // W_A
    whole = lambda s: pl.BlockSpec(s.shape, lambda i, nd=len(s.shape): (0,) * nd)
    slab_spec = pl.BlockSpec((D_MODEL, W_A), lambda i: (0, i))
    out_shape = tuple(out_shape) + (jax.ShapeDtypeStruct(w_in_f32.shape, BF16),)
    return pl.pallas_call(
        _sample_pre_kernel,
        grid=(n_slabs,),
        in_specs=[_resident(a.shape) for a in before_w] + [slab_spec] + [_resident(a.shape) for a in after_w],
        out_specs=tuple(whole(s) for s in out_shape[:-1]) + (slab_spec,),
        out_shape=out_shape,
        scratch_shapes=[pltpu.VMEM((nb, 3 * W_B), BF16), pltpu.VMEM((nb, D_MODEL), BF16),
                        pltpu.VMEM((n_slabs, nb, W_A), F32)],
        compiler_params=pltpu.CompilerParams(
            dimension_semantics=("arbitrary",), vmem_limit_bytes=SAMPLE_VMEM_BYTES),
        name="sample_pre",
    )(*before_w, w_in_f32, *after_w)


def _single_step_call(body, inputs, out_shape, scratch, name):
    return pl.pallas_call(
        body,
        grid=(1,),
        in_specs=[_resident(a.shape) for a in inputs],
        out_specs=tuple(pl.BlockSpec(s.shape, lambda *_, nd=len(s.shape): (0,) * nd) for s in out_shape),
        out_shape=out_shape,
        scratch_shapes=scratch,
        compiler_params=pltpu.CompilerParams(
            dimension_semantics=("arbitrary",), vmem_limit_bytes=SAMPLE_VMEM_BYTES),
        name=name,
    )(*inputs)


def kernel(x_prompt, x_sample, state_mlstm_C, state_mlstm_n, state_mlstm_m, state_conv, p_prompt, p_sample,
           norm_in_g, w_in, ln_v_g, ln_v_b, w_spatial, b_spatial, conv_w, conv_b, w_q, w_k, w_v, w_if, b_if,
           gn_g, skip, w_out, w_ple_gate, b_ple_gate, w_ple_proj, ple_norm_g, final_norm_g):
    assert norm_in_g.shape[0] == 1, "single-layer trunk"
    nb = x_sample.shape[0]
    row = lambda a: a.reshape(1, -1).astype(F32)

    norm_g = row(norm_in_g[0])
    lnv_g = row(ln_v_g[0])
    lnv_b = row(ln_v_b[0])
    w_sp = w_spatial[0]
    b_sp_full = jnp.repeat(b_spatial[0].T, DH_A, axis=1)
    cw = conv_w[0]
    cb = row(conv_b[0])
    wq_b = w_q[0].astype(BF16)
    wk_b = (w_k[0] * (DH_B ** -0.5)).astype(BF16)
    wv_b = w_v[0].astype(BF16)
    k_fix = jnp.concatenate([jnp.ones((W_B, 1), F32), jnp.full((W_B, 1), DH_B ** 0.5, F32), jnp.ones((W_B, 1), F32)], 0)
    w_if_p = jnp.pad(w_if[0] * k_fix, ((0, 0), (0, LANES - 2 * H_B))).astype(BF16)
    b_if_p = jnp.pad(b_if[0], (0, LANES - 2 * H_B)).reshape(1, LANES)
    gn = row(gn_g[0])
    sk = row(skip[0])
    w_out_b = w_out[0].astype(BF16)
    w_pg_b = w_ple_gate[0].astype(BF16)
    b_pg = row(b_ple_gate[0])
    w_pp_b = w_ple_proj[0].astype(BF16)
    ple_g = row(ple_norm_g[0])
    fin_g = row(final_norm_g)

    xs = x_sample
    ps = p_sample[0]
    conv_s = state_conv[0]
    n_s = state_mlstm_n[0]
    m_s = state_mlstm_m[0]
    w00 = jnp.repeat(w_sp[:, 0, 0], DH_A).reshape(1, W_A)
    b0 = jnp.repeat(b_spatial[0][:, 0], DH_A).reshape(1, W_A)
    full = jax.ShapeDtypeStruct((nb, W_B), F32)
    pre_out = (
        jax.ShapeDtypeStruct((nb, 1, W_A), F32),
        jax.ShapeDtypeStruct((nb, W_A), BF16),
        jax.ShapeDtypeStruct((nb, CONV_W - 1, W_B), F32),
        full, full, full, full, full, full, full, full, full,
        jax.ShapeDtypeStruct((nb, H_B, DH_B), F32),
        jax.ShapeDtypeStruct((nb, H_B), F32),
    )
    (vn_s, ya_s, conv_new, q_s, k_s, vs_s, gfull, numa, denom, osig, skipxc, zsilu, n_new, m_new,
     w_in_b) = _sample_pre_call(
        [xs, conv_s, n_s, m_s, norm_g], w_in[0],
        [lnv_g, lnv_b, w00, b0, cw, cb, wq_b, wk_b, wv_b, w_if_p, b_if_p, sk], pre_out)

    prompt_weights = [norm_g, w_in_b, lnv_g, lnv_b, w_sp, b_sp_full, cw, cb, wq_b, wk_b, wv_b, w_if_p, b_if_p,
                      gn, sk, w_out_b, w_pg_b, b_pg, w_pp_b, ple_g, fin_g]
    y_p, c_p, n_p, m_p, conv_p, c_new, cq = _prompt_call(
        x_prompt, p_prompt[0], prompt_weights, state_mlstm_C[0], [q_s, k_s, vs_s, gfull])

    (y_s,) = _single_step_call(
        _sample_post_kernel,
        [xs, ps, ya_s, cq, gfull, numa, denom, osig, skipxc, zsilu, gn, w_out_b, w_pg_b, b_pg, w_pp_b, ple_g, fin_g],
        (jax.ShapeDtypeStruct((nb, 1, D_MODEL), F32),), [pltpu.VMEM((nb, W_A + W_B), BF16)], "sample_post")

    return (y_p, y_s, c_p[None], n_p[None], m_p[:, :, 0, 0][None], conv_p[None],
            c_new[None], n_new[None], m_new[None], conv_new[None], vn_s[None])
```

```python
import jax
import jax.numpy as jnp
from jax import lax
from jax.experimental import pallas as pl
from jax.experimental.pallas import tpu as pltpu

D_MODEL = 1024
W_A = 1024
H_A = 8
DH_A = 128
W_B = 1024
H_B = 4
DH_B = 256
CHUNK = 128
CONV_W = 4
D_PLE = 256
N_IN = 3 * W_A + 3 * W_B
EPS = 1e-6

LANES = 128
SUBLANES = 8
PROMPT_TILE = 256
SEQS_PER_STEP = 1
STEP_ORDER = (
    "norm", "conv", "u", "v", "za", "qkv", "heads_lo", "gate_o", "heads_hi", "gate_z",
    "unit", "ple", "unit", "unit", "unit", "unit", "merge_a", "unit", "unit", "unit",
    "merge_b", "pair", "pair", "pair", "pair", "out", "pair", "pair", "pair", "pair",
)
PROMPT_VMEM_BYTES = 56 * 1024 * 1024
SAMPLE_VMEM_BYTES = 48 * 1024 * 1024

F32 = jnp.float32
BF16 = jnp.bfloat16
NT_DIMS = (((1,), (1,)), ((), ()))
TN_DIMS = (((0,), (0,)), ((), ()))


def _rmsnorm(x, g):
    return x * lax.rsqrt(jnp.mean(x * x, axis=-1, keepdims=True) + EPS) * g


def _layernorm(x, g, b=None):
    mu = jnp.mean(x, axis=-1, keepdims=True)
    d = x - mu
    var = jnp.mean(d * d, axis=-1, keepdims=True)
    y = d * lax.rsqrt(var + EPS) * g
    return y if b is None else y + b


def _gelu(x):
    return 0.5 * x * (1.0 + lax.erf(x * (0.5 ** 0.5)))


def _silu(x):
    return x * jax.nn.sigmoid(x)


def _dot(a, b):
    return jnp.dot(a, b, preferred_element_type=F32)


def _cumsum_rows(tril_b, x):
    x1 = x.astype(BF16)
    r1 = x - x1.astype(F32)
    x2 = r1.astype(BF16)
    x3 = (r1 - x2.astype(F32)).astype(BF16)
    return _dot(tril_b, x1) + _dot(tril_b, x2) + _dot(tril_b, x3)


def _prompt_kernel(x_ref, p_ref, norm_g_ref, w_in_ref, lnv_g_ref, lnv_b_ref, w_sp_ref, b_sp_ref,
                   conv_w_ref, conv_b_ref, w_q_ref, w_k_ref, w_v_ref, w_if_ref, b_if_ref,
                   gn_g_ref, skip_ref, w_out_ref, w_pg_ref, b_pg_ref, w_pp_ref, ple_g_ref, fin_g_ref,
                   sc_ref, sq_ref, sk_ref, svs_ref, sg_ref,
                   y_ref, c_ref, n_ref, m_ref, conv_ref, scnew_ref, scq_ref,
                   xpad_ref, qkv_ref, ycat_ref):
    n_seq, tt = x_ref.shape[0], x_ref.shape[1]
    n_chunks = tt // CHUNK
    step = pl.program_id(0) * pl.num_programs(1) + pl.program_id(1)

    @pl.when(pl.program_id(1) == 0)
    def _():
        c_ref[...] = jnp.zeros_like(c_ref)
        n_ref[...] = jnp.zeros_like(n_ref)
        m_ref[...] = jnp.zeros_like(m_ref)
        conv_ref[...] = jnp.zeros_like(conv_ref)

    @pl.when(step == 0)
    def _():
        scq_ref[...] = jnp.zeros_like(scq_ref)

    rows_per_step = sc_ref.shape[0]
    group_row = lax.broadcasted_iota(jnp.int32, (SUBLANES, DH_B), 0)

    def sample_state_pair(j, hh):
        sl = slice(hh * DH_B, (hh + 1) * DH_B)
        r = step * rows_per_step + j
        base = pl.multiple_of((r // SUBLANES) * SUBLANES, SUBLANES)
        is_row = group_row == (r - base)
        group = pl.ds(base, SUBLANES)
        c_old = sc_ref[j, hh]
        cq = lax.dot_general(sq_ref[group, sl].astype(BF16), c_old.astype(BF16), NT_DIMS, preferred_element_type=F32)
        scq_ref[group, sl] = jnp.where(is_row, cq, scq_ref[group, sl])
        k_row = jnp.where(is_row, sk_ref[group, sl], 0.0).astype(BF16)
        outer = lax.dot_general(svs_ref[group, sl].astype(BF16), k_row, TN_DIMS, preferred_element_type=F32)
        scnew_ref[j, hh] = sg_ref[pl.ds(r, 1), sl] * c_old + outer

    row = lax.broadcasted_iota(jnp.int32, (CHUNK, CHUNK), 0)
    col = lax.broadcasted_iota(jnp.int32, (CHUNK, CHUNK), 1)
    causal = row >= col
    tril = causal.astype(F32)
    tril_b = causal.astype(BF16)

    class Seq:
        pass

    seqs = [Seq() for _ in range(n_seq)]

    def proj(sq, i):
        return _dot(sq.hb, w_in_ref[:, i * W_A:(i + 1) * W_A])

    def stage_norm(sq, bi):
        sq.hb = _rmsnorm(x_ref[bi], norm_g_ref[...]).astype(BF16)

    def stage_u(sq, bi):
        sq.u = _gelu(proj(sq, 0))

    def stage_v(sq, bi):
        sq.v = _gelu(proj(sq, 1))

    def stage_za(sq, bi):
        sq.z_a = proj(sq, 2)

    def gmlp_heads(sq, bi, heads):
        for hh in heads:
            sl = slice(hh * DH_A, (hh + 1) * DH_A)
            vn = _layernorm(sq.v[:, sl], lnv_g_ref[:, sl], lnv_b_ref[:, sl]).astype(BF16)
            wm = (w_sp_ref[hh] * tril).astype(BF16)
            vn_wide = jnp.concatenate([vn[c * CHUNK:(c + 1) * CHUNK] for c in range(n_chunks)], axis=1)
            sp_wide = _dot(wm, vn_wide)
            for c in range(n_chunks):
                rows = slice(c * CHUNK, (c + 1) * CHUNK)
                sp = sp_wide[:, c * DH_A:(c + 1) * DH_A] + b_sp_ref[:, sl]
                ycat_ref[bi, rows, sl] = (sq.u[rows, sl] * sp * _silu(sq.z_a[rows, sl])).astype(BF16)

    def stage_conv(sq, bi):
        x_b = proj(sq, 3)
        xpad_ref[bi, SUBLANES - 3:SUBLANES, :] = conv_ref[bi]
        xpad_ref[bi, SUBLANES:SUBLANES + tt, :] = x_b
        conv_ref[bi] = x_b[tt - 3:tt, :]
        xc = xpad_ref[bi, SUBLANES - 3:SUBLANES - 3 + tt, :] * conv_w_ref[0:1, :]
        xc = xc + xpad_ref[bi, SUBLANES - 2:SUBLANES - 2 + tt, :] * conv_w_ref[1:2, :]
        xc = xc + xpad_ref[bi, SUBLANES - 1:SUBLANES - 1 + tt, :] * conv_w_ref[2:3, :]
        xc = xc + x_b * conv_w_ref[3:4, :]
        sq.xc = _silu(xc + conv_b_ref[...])
        sq.xcb = sq.xc.astype(BF16)
        sq.xbb = x_b.astype(BF16)

    def stage_gate_o(sq, bi):
        sq.o_sig = jax.nn.sigmoid(proj(sq, 4))

    def stage_gate_z(sq, bi):
        sq.z_silu = _silu(proj(sq, 5))

    def stage_qkv(sq, bi):
        for hh in range(H_B):
            sl = slice(hh * DH_B, (hh + 1) * DH_B)
            qkv_ref[bi, :, hh * DH_B:(hh + 1) * DH_B] = _dot(sq.xcb[:, sl], w_q_ref[hh]).astype(BF16)
            qkv_ref[bi, :, W_B + hh * DH_B:W_B + (hh + 1) * DH_B] = _dot(sq.xcb[:, sl], w_k_ref[hh]).astype(BF16)
            qkv_ref[bi, :, 2 * W_B + hh * DH_B:2 * W_B + (hh + 1) * DH_B] = _dot(sq.xbb[:, sl], w_v_ref[hh]).astype(BF16)
        gates = _dot(qkv_ref[bi], w_if_ref[...]) + b_if_ref[...]
        logf = jax.nn.log_sigmoid(gates)
        sq.decay = []
        for c in range(n_chunks):
            rows = slice(c * CHUNK, (c + 1) * CHUNK)
            bcum = pltpu.roll(_cumsum_rows(tril_b, logf[rows]), LANES - H_B, 1)
            a_all = gates[rows] - bcum
            sq.decay.append((bcum, a_all, a_all.T))

    def mlstm_unit(sq, bi, c, hh):
        rows = slice(c * CHUNK, (c + 1) * CHUNK)
        sl = slice(hh * DH_B, (hh + 1) * DH_B)
        bcum, a_all, a_t = sq.decay[c]
        qh = qkv_ref[bi, rows, hh * DH_B:(hh + 1) * DH_B]
        kh = qkv_ref[bi, rows, W_B + hh * DH_B:W_B + (hh + 1) * DH_B]
        vh = qkv_ref[bi, rows, 2 * W_B + hh * DH_B:2 * W_B + (hh + 1) * DH_B]
        m_prev = m_ref[bi, hh, 0:1, 0:1]
        a_mat = jnp.where(causal, a_t[hh:hh + 1, :], -jnp.inf)
        m_run = jnp.maximum(jnp.max(a_mat, axis=-1, keepdims=True), m_prev)
        s = jnp.exp(a_mat - m_run) * lax.dot_general(qh, kh, NT_DIMS, preferred_element_type=F32)
        g = jnp.exp(m_prev - m_run)
        c_old = c_ref[bi, hh]
        inter = lax.dot_general(qh, c_old.astype(BF16), NT_DIMS, preferred_element_type=F32)
        num = _dot(s.astype(BF16), vh) + g * inter
        n_old = n_ref[bi, hh:hh + 1, :]
        den = jnp.sum(s, axis=-1, keepdims=True) + g * jnp.sum(qh.astype(F32) * n_old, axis=-1, keepdims=True)
        m_t = bcum[:, hh:hh + 1] + m_run
        hcell = num / jnp.maximum(jnp.abs(den), jnp.exp(-m_t))
        m_last = m_run[CHUNK - 1:CHUNK, :]
        w_end = jnp.exp(a_all[:, hh:hh + 1] - m_last)
        g_end = jnp.exp(m_prev - m_last)
        kw = kh.astype(F32) * w_end
        c_ref[bi, hh] = g_end * c_old + lax.dot_general(vh, kw.astype(BF16), TN_DIMS, preferred_element_type=F32)
        n_ref[bi, hh:hh + 1, :] = g_end * n_old + jnp.sum(kw, axis=0, keepdims=True)
        m_ref[bi, hh] = jnp.broadcast_to(m_t[CHUNK - 1:CHUNK, :], (SUBLANES, LANES))
        hn = _layernorm(hcell * sq.o_sig[rows, sl], gn_g_ref[:, sl])
        yb = (hn + skip_ref[:, sl] * sq.xc[rows, sl]) * sq.z_silu[rows, sl]
        ycat_ref[bi, rows, W_A + hh * DH_B:W_A + (hh + 1) * DH_B] = yb.astype(BF16)

    def stage_ple(sq, bi):
        sq.e = _rmsnorm(_dot(p_ref[bi].astype(BF16), w_pp_ref[...]), ple_g_ref[...])

    def stage_merge_a(sq, bi):
        sq.x1 = x_ref[bi] + _dot(ycat_ref[bi, :, :W_A], w_out_ref[:W_A, :])

    def stage_merge_b(sq, bi):
        sq.x1 = sq.x1 + _dot(ycat_ref[bi, :, W_A:], w_out_ref[W_A:, :])

    def stage_out(sq, bi):
        gate = jax.nn.sigmoid(_dot(sq.x1.astype(BF16), w_pg_ref[...]) + b_pg_ref[...])
        x2 = sq.x1 + gate * sq.e
        y_ref[bi] = _rmsnorm(x2, fin_g_ref[...])

    units = iter([(c, hh) for c in range(n_chunks) for hh in range(H_B)])
    pairs = iter([(j, hh) for j in range(rows_per_step) for hh in range(H_B)])
    stages = {
        "norm": stage_norm, "conv": stage_conv, "u": stage_u, "qkv": stage_qkv, "v": stage_v,
        "gate_o": stage_gate_o, "gate_z": stage_gate_z, "za": stage_za,
        "heads_lo": lambda sq, bi: gmlp_heads(sq, bi, list(range(0, H_A // 2))),
        "heads_hi": lambda sq, bi: gmlp_heads(sq, bi, list(range(H_A // 2, H_A))),
        "unit": lambda sq, bi: mlstm_unit(sq, bi, *next(units)),
        "pair": lambda sq, bi: sample_state_pair(*next(pairs)),
        "ple": stage_ple, "merge_a": stage_merge_a, "merge_b": stage_merge_b, "out": stage_out,
    }
    assert STEP_ORDER.count("unit") == n_chunks * H_B and STEP_ORDER.count("pair") == rows_per_step * H_B
    for name in STEP_ORDER:
        stages[name](seqs[0], 0)


def _resident(shape):
    nd = len(shape)
    return pl.BlockSpec(shape, lambda *_: (0,) * nd, pipeline_mode=pl.Buffered(1))


def _prompt_call(x, p, weights, sample_c, sample_rows):
    batch, seq, _ = x.shape
    tt = PROMPT_TILE
    ns = SEQS_PER_STEP
    grid = (batch // ns, seq // tt)
    n_steps = grid[0] * grid[1]
    n_rows = sample_c.shape[0]
    rows_per_step = n_rows // n_steps
    assert rows_per_step * n_steps == n_rows, "sample rows are spread evenly over the prompt grid steps"

    def state_block(b, t):
        return (b * grid[1] + t, 0, 0, 0)

    in_specs = [
        pl.BlockSpec((ns, tt, D_MODEL), lambda b, t: (b, t, 0)),
        pl.BlockSpec((ns, tt, D_PLE), lambda b, t: (b, t, 0)),
    ] + [_resident(w.shape) for w in weights] + [
        pl.BlockSpec((rows_per_step, H_B, DH_B, DH_B), state_block),
    ] + [_resident(a.shape) for a in sample_rows]
    out_shape = (
        jax.ShapeDtypeStruct((batch, seq, D_MODEL), F32),
        jax.ShapeDtypeStruct((batch, H_B, DH_B, DH_B), F32),
        jax.ShapeDtypeStruct((batch, H_B, DH_B), F32),
        jax.ShapeDtypeStruct((batch, H_B, SUBLANES, LANES), F32),
        jax.ShapeDtypeStruct((batch, CONV_W - 1, W_B), F32),
        jax.ShapeDtypeStruct(sample_c.shape, F32),
        jax.ShapeDtypeStruct((n_rows, W_B), F32),
    )
    out_specs = (
        pl.BlockSpec((ns, tt, D_MODEL), lambda b, t: (b, t, 0)),
        pl.BlockSpec((ns, H_B, DH_B, DH_B), lambda b, t: (b, 0, 0, 0)),
        pl.BlockSpec((ns, H_B, DH_B), lambda b, t: (b, 0, 0)),
        pl.BlockSpec((ns, H_B, SUBLANES, LANES), lambda b, t: (b, 0, 0, 0)),
        pl.BlockSpec((ns, CONV_W - 1, W_B), lambda b, t: (b, 0, 0)),
        pl.BlockSpec((rows_per_step, H_B, DH_B, DH_B), state_block),
        pl.BlockSpec((n_rows, W_B), lambda b, t: (0, 0)),
    )
    scratch = [
        pltpu.VMEM((ns, tt + SUBLANES, W_B), F32),
        pltpu.VMEM((ns, tt, 3 * W_B), BF16),
        pltpu.VMEM((ns, tt, W_A + W_B), BF16),
    ]
    return pl.pallas_call(
        _prompt_kernel,
        grid=grid,
        in_specs=in_specs,
        out_specs=out_specs,
        out_shape=out_shape,
        scratch_shapes=scratch,
        compiler_params=pltpu.CompilerParams(
            dimension_semantics=("arbitrary", "arbitrary"),
            vmem_limit_bytes=PROMPT_VMEM_BYTES),
        name="prompt_fused",
    )(x, p, *weights, sample_c, *sample_rows)


def _sample_pre_kernel(x_ref, conv_ref, n_ref, m_ref, norm_g_ref, w_lo_ref, w_hi_ref, lnv_g_ref, lnv_b_ref,
                       w00_ref, b0_ref, conv_w_ref, conv_b_ref, w_q_ref, w_k_ref, w_v_ref, w_if_ref,
                       b_if_ref, skip_ref,
                       vn_ref, ya_ref, convnew_ref, q_ref, k_ref, vs_ref, gfull_ref, numa_ref,
                       denom_ref, osig_ref, skipxc_ref, zsilu_ref, nnew_ref, mnew_ref, w_in_bf16_ref,
                       qkv_ref, hb_ref, proj_ref):
    slab = pl.program_id(0)

    @pl.when(slab == 0)
    def _():
        hb_ref[...] = _rmsnorm(x_ref[:, 0, :], norm_g_ref[...]).astype(BF16)

    half = w_lo_ref.shape[1]
    for w_ref, cols in ((w_lo_ref, slice(0, half)), (w_hi_ref, slice(half, 2 * half))):
        w_half = w_ref[...].astype(BF16)
        w_in_bf16_ref[:, cols] = w_half
        proj_ref[slab, :, cols] = _dot(hb_ref[...], w_half)

    @pl.when(slab == pl.num_programs(0) - 1)
    def _():
        _sample_front_end(conv_ref, n_ref, m_ref, lnv_g_ref, lnv_b_ref, w00_ref, b0_ref, conv_w_ref, conv_b_ref,
                          w_q_ref, w_k_ref, w_v_ref, w_if_ref, b_if_ref, skip_ref,
                          vn_ref, ya_ref, convnew_ref, q_ref, k_ref, vs_ref, gfull_ref, numa_ref,
                          denom_ref, osig_ref, skipxc_ref, zsilu_ref, nnew_ref, mnew_ref, qkv_ref, proj_ref)


def _sample_front_end(conv_ref, n_ref, m_ref, lnv_g_ref, lnv_b_ref, w00_ref, b0_ref, conv_w_ref, conv_b_ref,
                      w_q_ref, w_k_ref, w_v_ref, w_if_ref, b_if_ref, skip_ref,
                      vn_ref, ya_ref, convnew_ref, q_ref, k_ref, vs_ref, gfull_ref, numa_ref,
                      denom_ref, osig_ref, skipxc_ref, zsilu_ref, nnew_ref, mnew_ref, qkv_ref, proj_ref):
    def proj(i):
        return proj_ref[i]

    u = _gelu(proj(0))
    v = _gelu(proj(1))
    z_a = proj(2)
    for hh in range(H_A):
        sl = slice(hh * DH_A, (hh + 1) * DH_A)
        vn = _layernorm(v[:, sl], lnv_g_ref[:, sl], lnv_b_ref[:, sl])
        vn_ref[:, 0, sl] = vn
        sp = vn * w00_ref[:, sl] + b0_ref[:, sl]
        ya_ref[:, sl] = (u[:, sl] * sp * _silu(z_a[:, sl])).astype(BF16)

    x_b = proj(3)
    osig_ref[...] = jax.nn.sigmoid(proj(4))
    zsilu_ref[...] = _silu(proj(5))
    c0 = conv_ref[:, 0, :]
    c1 = conv_ref[:, 1, :]
    c2 = conv_ref[:, 2, :]
    xc = c0 * conv_w_ref[0:1, :]
    xc = xc + c1 * conv_w_ref[1:2, :]
    xc = xc + c2 * conv_w_ref[2:3, :]
    xc = xc + x_b * conv_w_ref[3:4, :]
    xc = _silu(xc + conv_b_ref[...])
    convnew_ref[:, 0, :] = c1
    convnew_ref[:, 1, :] = c2
    convnew_ref[:, 2, :] = x_b
    skipxc_ref[...] = skip_ref[...] * xc
    xcb = xc.astype(BF16)
    xbb = x_b.astype(BF16)
    for hh in range(H_B):
        sl = slice(hh * DH_B, (hh + 1) * DH_B)
        qkv_ref[:, hh * DH_B:(hh + 1) * DH_B] = _dot(xcb[:, sl], w_q_ref[hh]).astype(BF16)
        qkv_ref[:, W_B + hh * DH_B:W_B + (hh + 1) * DH_B] = _dot(xcb[:, sl], w_k_ref[hh]).astype(BF16)
        qkv_ref[:, 2 * W_B + hh * DH_B:2 * W_B + (hh + 1) * DH_B] = _dot(xbb[:, sl], w_v_ref[hh]).astype(BF16)
    gates = _dot(qkv_ref[...], w_if_ref[...]) + b_if_ref[...]
    logf = pltpu.roll(jax.nn.log_sigmoid(gates), LANES - H_B, 1)
    li = gates[:, 0:H_B]
    inter = logf[:, 0:H_B] + m_ref[...]
    m_t = jnp.maximum(inter, li)
    s_w = jnp.exp(li - m_t)
    g = jnp.exp(inter - m_t)
    e_neg = jnp.exp(-m_t)
    mnew_ref[...] = m_t
    for hh in range(H_B):
        sl = slice(hh * DH_B, (hh + 1) * DH_B)
        qf = qkv_ref[:, hh * DH_B:(hh + 1) * DH_B].astype(F32)
        kf = qkv_ref[:, W_B + hh * DH_B:W_B + (hh + 1) * DH_B].astype(F32)
        vf = qkv_ref[:, 2 * W_B + hh * DH_B:2 * W_B + (hh + 1) * DH_B].astype(F32)
        s_h = s_w[:, hh:hh + 1]
        g_h = g[:, hh:hh + 1]
        n_old = n_ref[:, hh, :]
        s_qk = s_h * jnp.sum(qf * kf, axis=-1, keepdims=True)
        den = s_qk + g_h * jnp.sum(n_old * qf, axis=-1, keepdims=True)
        denom = jnp.maximum(jnp.abs(den), e_neg[:, hh:hh + 1])
        q_ref[:, sl] = qf
        k_ref[:, sl] = kf
        vs_ref[:, sl] = s_h * vf
        gfull_ref[:, sl] = jnp.broadcast_to(g_h, qf.shape)
        numa_ref[:, sl] = s_qk * vf
        denom_ref[:, sl] = jnp.broadcast_to(denom, qf.shape)
        nnew_ref[:, hh, :] = g_h * n_old + s_h * kf


def _sample_post_kernel(x_ref, p_ref, ya_ref, cq_ref, gfull_ref, numa_ref, denom_ref, osig_ref,
                        skipxc_ref, zsilu_ref, gn_g_ref, w_out_ref, w_pg_ref, b_pg_ref, w_pp_ref,
                        ple_g_ref, fin_g_ref, y_ref, ycat_ref):
    ycat_ref[:, 0:W_A] = ya_ref[...]
    for hh in range(H_B):
        sl = slice(hh * DH_B, (hh + 1) * DH_B)
        hcell = (numa_ref[:, sl] + gfull_ref[:, sl] * cq_ref[:, sl]) / denom_ref[:, sl]
        hn = _layernorm(hcell * osig_ref[:, sl], gn_g_ref[:, sl])
        ycat_ref[:, W_A + hh * DH_B:W_A + (hh + 1) * DH_B] = ((hn + skipxc_ref[:, sl]) * zsilu_ref[:, sl]).astype(BF16)
    x1 = x_ref[:, 0, :] + _dot(ycat_ref[...], w_out_ref[...])
    e = _rmsnorm(_dot(p_ref[:, 0, :].astype(BF16), w_pp_ref[...]), ple_g_ref[...])
    gate = jax.nn.sigmoid(_dot(x1.astype(BF16), w_pg_ref[...]) + b_pg_ref[...])
    x2 = x1 + gate * e
    y_ref[:, 0, :] = _rmsnorm(x2, fin_g_ref[...])


def _sample_pre_call(before_w, w_in_f32, after_w, out_shape):
    nb = before_w[0].shape[0]
    n_slabs = N_IN // W_A
    whole = lambda s: pl.BlockSpec(s.shape, lambda i, nd=len(s.shape): (0,) * nd)
    slab_spec = pl.BlockSpec((D_MODEL, W_A), lambda i: (0, i))
    out_shape = tuple(out_shape) + (jax.ShapeDtypeStruct(w_in_f32.shape, BF16),)
    return pl.pallas_call(
        _sample_pre_kernel,
        grid=(n_slabs,),
        in_specs=[_resident(a.shape) for a in before_w]
        + [pl.BlockSpec((D_MODEL, W_A // 2), lambda i: (0, 2 * i)),
           pl.BlockSpec((D_MODEL, W_A // 2), lambda i: (0, 2 * i + 1))]
        + [_resident(a.shape) for a in after_w],
        out_specs=tuple(whole(s) for s in out_shape[:-1]) + (slab_spec,),
        out_shape=out_shape,
        scratch_shapes=[pltpu.VMEM((nb, 3 * W_B), BF16), pltpu.VMEM((nb, D_MODEL), BF16),
                        pltpu.VMEM((n_slabs, nb, W_A), F32)],
        compiler_params=pltpu.CompilerParams(
            dimension_semantics=("arbitrary",), vmem_limit_bytes=SAMPLE_VMEM_BYTES),
        name="sample_pre",
    )(*before_w, w_in_f32, w_in_f32, *after_w)


def _single_step_call(body, inputs, out_shape, scratch, name):
    return pl.pallas_call(
        body,
        grid=(1,),
        in_specs=[_resident(a.shape) for a in inputs],
        out_specs=tuple(pl.BlockSpec(s.shape, lambda *_, nd=len(s.shape): (0,) * nd) for s in out_shape),
        out_shape=out_shape,
        scratch_shapes=scratch,
        compiler_params=pltpu.CompilerParams(
            dimension_semantics=("arbitrary",), vmem_limit_bytes=SAMPLE_VMEM_BYTES),
        name=name,
    )(*inputs)


def kernel(x_prompt, x_sample, state_mlstm_C, state_mlstm_n, state_mlstm_m, state_conv, p_prompt, p_sample,
           norm_in_g, w_in, ln_v_g, ln_v_b, w_spatial, b_spatial, conv_w, conv_b, w_q, w_k, w_v, w_if, b_if,
           gn_g, skip, w_out, w_ple_gate, b_ple_gate, w_ple_proj, ple_norm_g, final_norm_g):
    assert norm_in_g.shape[0] == 1, "single-layer trunk"
    nb = x_sample.shape[0]
    row = lambda a: a.reshape(1, -1).astype(F32)

    norm_g = row(norm_in_g[0])
    lnv_g = row(ln_v_g[0])
    lnv_b = row(ln_v_b[0])
    w_sp = w_spatial[0]
    b_sp_full = jnp.repeat(b_spatial[0].T, DH_A, axis=1)
    cw = conv_w[0]
    cb = row(conv_b[0])
    wq_b = w_q[0].astype(BF16)
    wk_b = (w_k[0] * (DH_B ** -0.5)).astype(BF16)
    wv_b = w_v[0].astype(BF16)
    k_fix = jnp.concatenate([jnp.ones((W_B, 1), F32), jnp.full((W_B, 1), DH_B ** 0.5, F32), jnp.ones((W_B, 1), F32)], 0)
    w_if_p = jnp.pad(w_if[0] * k_fix, ((0, 0), (0, LANES - 2 * H_B))).astype(BF16)
    b_if_p = jnp.pad(b_if[0], (0, LANES - 2 * H_B)).reshape(1, LANES)
    gn = row(gn_g[0])
    sk = row(skip[0])
    w_out_b = w_out[0].astype(BF16)
    w_pg_b = w_ple_gate[0].astype(BF16)
    b_pg = row(b_ple_gate[0])
    w_pp_b = w_ple_proj[0].astype(BF16)
    ple_g = row(ple_norm_g[0])
    fin_g = row(final_norm_g)

    xs = x_sample
    ps = p_sample[0]
    conv_s = state_conv[0]
    n_s = state_mlstm_n[0]
    m_s = state_mlstm_m[0]
    w00 = jnp.repeat(w_sp[:, 0, 0], DH_A).reshape(1, W_A)
    b0 = jnp.repeat(b_spatial[0][:, 0], DH_A).reshape(1, W_A)
    full = jax.ShapeDtypeStruct((nb, W_B), F32)
    pre_out = (
        jax.ShapeDtypeStruct((nb, 1, W_A), F32),
        jax.ShapeDtypeStruct((nb, W_A), BF16),
        jax.ShapeDtypeStruct((nb, CONV_W - 1, W_B), F32),
        full, full, full, full, full, full, full, full, full,
        jax.ShapeDtypeStruct((nb, H_B, DH_B), F32),
        jax.ShapeDtypeStruct((nb, H_B), F32),
    )
    (vn_s, ya_s, conv_new, q_s, k_s, vs_s, gfull, numa, denom, osig, skipxc, zsilu, n_new, m_new,
     w_in_b) = _sample_pre_call(
        [xs, conv_s, n_s, m_s, norm_g], w_in[0],
        [lnv_g, lnv_b, w00, b0, cw, cb, wq_b, wk_b, wv_b, w_if_p, b_if_p, sk], pre_out)

    prompt_weights = [norm_g, w_in_b, lnv_g, lnv_b, w_sp, b_sp_full, cw, cb, wq_b, wk_b, wv_b, w_if_p, b_if_p,
                      gn, sk, w_out_b, w_pg_b, b_pg, w_pp_b, ple_g, fin_g]
    y_p, c_p, n_p, m_p, conv_p, c_new, cq = _prompt_call(
        x_prompt, p_prompt[0], prompt_weights, state_mlstm_C[0], [q_s, k_s, vs_s, gfull])

    (y_s,) = _single_step_call(
        _sample_post_kernel,
        [xs, ps, ya_s, cq, gfull, numa, denom, osig, skipxc, zsilu, gn, w_out_b, w_pg_b, b_pg, w_pp_b, ple_g, fin_g],
        (jax.ShapeDtypeStruct((nb, 1, D_MODEL), F32),), [pltpu.VMEM((nb, W_A + W_B), BF16)], "sample_post")

    return (y_p, y_s, c_p[None], n_p[None], m_p[:, :, 0, 0][None], conv_p[None],
            c_new[None], n_new[None], m_new[None], conv_new[None], vn_s[None])
```

```python
import jax
import jax.numpy as jnp
from jax import lax
from jax.experimental import pallas as pl
from jax.experimental.pallas import tpu as pltpu

D_MODEL = 1024
W_A = 1024
H_A = 8
DH_A = 128
W_B = 1024
H_B = 4
DH_B = 256
CHUNK = 128
CONV_W = 4
D_PLE = 256
N_IN = 3 * W_A + 3 * W_B
EPS = 1e-6

LANES = 128
SUBLANES = 8
PROMPT_TILE = 256
SEQS_PER_STEP = 1
STEP_ORDER = (
    "ple", "norm", "conv", "u", "v", "za", "qkv", "heads_lo", "gate_o", "heads_hi", "gate_z",
    "unit", "unit", "unit", "unit", "unit", "merge_a", "unit", "unit", "unit",
    "merge_b", "pair", "pair", "pair", "pair", "out", "pair", "pair", "pair", "pair",
)
PROMPT_VMEM_BYTES = 56 * 1024 * 1024
SAMPLE_VMEM_BYTES = 48 * 1024 * 1024

F32 = jnp.float32
BF16 = jnp.bfloat16
NT_DIMS = (((1,), (1,)), ((), ()))
TN_DIMS = (((0,), (0,)), ((), ()))


def _rmsnorm(x, g):
    return x * lax.rsqrt(jnp.mean(x * x, axis=-1, keepdims=True) + EPS) * g


def _layernorm(x, g, b=None):
    mu = jnp.mean(x, axis=-1, keepdims=True)
    d = x - mu
    var = jnp.mean(d * d, axis=-1, keepdims=True)
    y = d * lax.rsqrt(var + EPS) * g
    return y if b is None else y + b


def _gelu(x):
    return 0.5 * x * (1.0 + lax.erf(x * (0.5 ** 0.5)))


def _silu(x):
    return x * jax.nn.sigmoid(x)


def _dot(a, b):
    return jnp.dot(a, b, preferred_element_type=F32)


def _cumsum_rows(tril_b, x):
    x1 = x.astype(BF16)
    r1 = x - x1.astype(F32)
    x2 = r1.astype(BF16)
    x3 = (r1 - x2.astype(F32)).astype(BF16)
    return _dot(tril_b, x1) + _dot(tril_b, x2) + _dot(tril_b, x3)


def _vector_params(vec_ref):
    row = lambda i: vec_ref.at[i:i + 1]
    return dict(norm_g=row(0), lnv_g=row(1), lnv_b=row(2), conv_b=row(3), gn_g=row(4), skip=row(5), b_pg=row(6),
                ple_g=row(7), fin_g=row(8), conv_w=vec_ref.at[9:9 + CONV_W], b_if=vec_ref.at[13:14, 0:LANES],
                w00=row(14), b0=row(15))


def _pack_vectors(norm_g, lnv_g, lnv_b, conv_b, gn_g, skip, b_pg, ple_g, fin_g, conv_w, b_if, w00, b0):
    wide = lambda a: jnp.pad(a.reshape(1, -1).astype(F32), ((0, 0), (0, D_MODEL - a.size)))
    rows = [wide(a) for a in (norm_g, lnv_g, lnv_b, conv_b, gn_g, skip, b_pg, ple_g, fin_g)]
    return jnp.concatenate(rows + [conv_w.astype(F32), wide(b_if), wide(w00), wide(b0)], axis=0)


def _prompt_kernel(x_ref, p_ref, vec_ref, w_in_ref, w_sp_ref, b_sp_ref, w_q_ref, w_k_ref, w_v_ref, w_if_ref,
                   w_out_ref, w_pg_ref, w_pp_ref,
                   sc_ref, sq_ref, sk_ref, svs_ref, sg_ref,
                   y_ref, c_ref, n_ref, m_ref, conv_ref, scnew_ref, scq_ref,
                   xpad_ref, qkv_ref, ycat_ref):
    vp = _vector_params(vec_ref)
    norm_g_ref, lnv_g_ref, lnv_b_ref, conv_w_ref, conv_b_ref = (
        vp["norm_g"], vp["lnv_g"], vp["lnv_b"], vp["conv_w"], vp["conv_b"])
    b_if_ref, gn_g_ref, skip_ref, b_pg_ref, ple_g_ref, fin_g_ref = (
        vp["b_if"], vp["gn_g"], vp["skip"], vp["b_pg"], vp["ple_g"], vp["fin_g"])
    n_seq, tt = x_ref.shape[0], x_ref.shape[1]
    n_chunks = tt // CHUNK
    step = pl.program_id(0) * pl.num_programs(1) + pl.program_id(1)

    @pl.when(pl.program_id(1) == 0)
    def _():
        c_ref[...] = jnp.zeros_like(c_ref)
        n_ref[...] = jnp.zeros_like(n_ref)
        m_ref[...] = jnp.zeros_like(m_ref)
        conv_ref[...] = jnp.zeros_like(conv_ref)

    @pl.when(step == 0)
    def _():
        scq_ref[...] = jnp.zeros_like(scq_ref)

    rows_per_step = sc_ref.shape[0]
    group_row = lax.broadcasted_iota(jnp.int32, (SUBLANES, DH_B), 0)

    def sample_state_pair(j, hh):
        sl = slice(hh * DH_B, (hh + 1) * DH_B)
        r = step * rows_per_step + j
        base = pl.multiple_of((r // SUBLANES) * SUBLANES, SUBLANES)
        is_row = group_row == (r - base)
        group = pl.ds(base, SUBLANES)
        c_old = sc_ref[j, hh]
        cq = lax.dot_general(sq_ref[group, sl].astype(BF16), c_old.astype(BF16), NT_DIMS, preferred_element_type=F32)
        scq_ref[group, sl] = jnp.where(is_row, cq, scq_ref[group, sl])
        k_row = jnp.where(is_row, sk_ref[group, sl], 0.0).astype(BF16)
        outer = lax.dot_general(svs_ref[group, sl].astype(BF16), k_row, TN_DIMS, preferred_element_type=F32)
        scnew_ref[j, hh] = sg_ref[pl.ds(r, 1), sl] * c_old + outer

    row = lax.broadcasted_iota(jnp.int32, (CHUNK, CHUNK), 0)
    col = lax.broadcasted_iota(jnp.int32, (CHUNK, CHUNK), 1)
    causal = row >= col
    tril = causal.astype(F32)
    tril_b = causal.astype(BF16)

    class Seq:
        pass

    seqs = [Seq() for _ in range(n_seq)]

    def proj(sq, i):
        return _dot(sq.hb, w_in_ref[:, i * W_A:(i + 1) * W_A])

    def stage_norm(sq, bi):
        sq.hb = _rmsnorm(x_ref[bi], norm_g_ref[...]).astype(BF16)

    def stage_u(sq, bi):
        sq.u = _gelu(proj(sq, 0))

    def stage_v(sq, bi):
        sq.v = _gelu(proj(sq, 1))

    def stage_za(sq, bi):
        sq.z_a = proj(sq, 2)

    def gmlp_heads(sq, bi, heads):
        for hh in heads:
            sl = slice(hh * DH_A, (hh + 1) * DH_A)
            vn = _layernorm(sq.v[:, sl], lnv_g_ref[:, sl], lnv_b_ref[:, sl]).astype(BF16)
            wm = (w_sp_ref[hh] * tril).astype(BF16)
            vn_wide = jnp.concatenate([vn[c * CHUNK:(c + 1) * CHUNK] for c in range(n_chunks)], axis=1)
            sp_wide = _dot(wm, vn_wide)
            for c in range(n_chunks):
                rows = slice(c * CHUNK, (c + 1) * CHUNK)
                sp = sp_wide[:, c * DH_A:(c + 1) * DH_A] + b_sp_ref[:, sl]
                ycat_ref[bi, rows, sl] = (sq.u[rows, sl] * sp * _silu(sq.z_a[rows, sl])).astype(BF16)

    def stage_conv(sq, bi):
        x_b = proj(sq, 3)
        xpad_ref[bi, SUBLANES - 3:SUBLANES, :] = conv_ref[bi]
        xpad_ref[bi, SUBLANES:SUBLANES + tt, :] = x_b
        conv_ref[bi] = x_b[tt - 3:tt, :]
        xc = xpad_ref[bi, SUBLANES - 3:SUBLANES - 3 + tt, :] * conv_w_ref[0:1, :]
        xc = xc + xpad_ref[bi, SUBLANES - 2:SUBLANES - 2 + tt, :] * conv_w_ref[1:2, :]
        xc = xc + xpad_ref[bi, SUBLANES - 1:SUBLANES - 1 + tt, :] * conv_w_ref[2:3, :]
        xc = xc + x_b * conv_w_ref[3:4, :]
        sq.xc = _silu(xc + conv_b_ref[...])
        sq.xcb = sq.xc.astype(BF16)
        sq.xbb = x_b.astype(BF16)

    def stage_gate_o(sq, bi):
        sq.o_sig = jax.nn.sigmoid(proj(sq, 4))

    def stage_gate_z(sq, bi):
        sq.z_silu = _silu(proj(sq, 5))

    def stage_qkv(sq, bi):
        for hh in range(H_B):
            sl = slice(hh * DH_B, (hh + 1) * DH_B)
            qkv_ref[bi, :, hh * DH_B:(hh + 1) * DH_B] = _dot(sq.xcb[:, sl], w_q_ref[hh]).astype(BF16)
            qkv_ref[bi, :, W_B + hh * DH_B:W_B + (hh + 1) * DH_B] = _dot(sq.xcb[:, sl], w_k_ref[hh]).astype(BF16)
            qkv_ref[bi, :, 2 * W_B + hh * DH_B:2 * W_B + (hh + 1) * DH_B] = _dot(sq.xbb[:, sl], w_v_ref[hh]).astype(BF16)
        gates = _dot(qkv_ref[bi], w_if_ref[...]) + b_if_ref[...]
        logf = jax.nn.log_sigmoid(gates)
        sq.decay = []
        for c in range(n_chunks):
            rows = slice(c * CHUNK, (c + 1) * CHUNK)
            bcum = pltpu.roll(_cumsum_rows(tril_b, logf[rows]), LANES - H_B, 1)
            a_all = gates[rows] - bcum
            sq.decay.append((bcum, a_all, a_all.T))

    def mlstm_unit(sq, bi, c, hh):
        rows = slice(c * CHUNK, (c + 1) * CHUNK)
        sl = slice(hh * DH_B, (hh + 1) * DH_B)
        bcum, a_all, a_t = sq.decay[c]
        qh = qkv_ref[bi, rows, hh * DH_B:(hh + 1) * DH_B]
        kh = qkv_ref[bi, rows, W_B + hh * DH_B:W_B + (hh + 1) * DH_B]
        vh = qkv_ref[bi, rows, 2 * W_B + hh * DH_B:2 * W_B + (hh + 1) * DH_B]
        m_prev = m_ref[bi, hh, 0:1, 0:1]
        a_mat = jnp.where(causal, a_t[hh:hh + 1, :], -jnp.inf)
        m_run = jnp.maximum(jnp.max(a_mat, axis=-1, keepdims=True), m_prev)
        s = jnp.exp(a_mat - m_run) * lax.dot_general(qh, kh, NT_DIMS, preferred_element_type=F32)
        g = jnp.exp(m_prev - m_run)
        c_old = c_ref[bi, hh]
        inter = lax.dot_general(qh, c_old.astype(BF16), NT_DIMS, preferred_element_type=F32)
        num = _dot(s.astype(BF16), vh) + g * inter
        n_old = n_ref[bi, hh:hh + 1, :]
        den = jnp.sum(s, axis=-1, keepdims=True) + g * jnp.sum(qh.astype(F32) * n_old, axis=-1, keepdims=True)
        m_t = bcum[:, hh:hh + 1] + m_run
        hcell = num / jnp.maximum(jnp.abs(den), jnp.exp(-m_t))
        m_last = m_run[CHUNK - 1:CHUNK, :]
        w_end = jnp.exp(a_all[:, hh:hh + 1] - m_last)
        g_end = jnp.exp(m_prev - m_last)
        kw = kh.astype(F32) * w_end
        c_ref[bi, hh] = g_end * c_old + lax.dot_general(vh, kw.astype(BF16), TN_DIMS, preferred_element_type=F32)
        n_ref[bi, hh:hh + 1, :] = g_end * n_old + jnp.sum(kw, axis=0, keepdims=True)
        m_ref[bi, hh] = jnp.broadcast_to(m_t[CHUNK - 1:CHUNK, :], (SUBLANES, LANES))
        hn = _layernorm(hcell * sq.o_sig[rows, sl], gn_g_ref[:, sl])
        yb = (hn + skip_ref[:, sl] * sq.xc[rows, sl]) * sq.z_silu[rows, sl]
        ycat_ref[bi, rows, W_A + hh * DH_B:W_A + (hh + 1) * DH_B] = yb.astype(BF16)

    def stage_ple(sq, bi):
        sq.e = _rmsnorm(_dot(p_ref[bi].astype(BF16), w_pp_ref[...]), ple_g_ref[...])

    def stage_merge_a(sq, bi):
        sq.x1 = x_ref[bi] + _dot(ycat_ref[bi, :, :W_A], w_out_ref[:W_A, :])

    def stage_merge_b(sq, bi):
        sq.x1 = sq.x1 + _dot(ycat_ref[bi, :, W_A:], w_out_ref[W_A:, :])

    def stage_out(sq, bi):
        gate = jax.nn.sigmoid(_dot(sq.x1.astype(BF16), w_pg_ref[...]) + b_pg_ref[...])
        x2 = sq.x1 + gate * sq.e
        y_ref[bi] = _rmsnorm(x2, fin_g_ref[...])

    units = iter([(c, hh) for c in range(n_chunks) for hh in range(H_B)])
    pairs = iter([(j, hh) for j in range(rows_per_step) for hh in range(H_B)])
    stages = {
        "norm": stage_norm, "conv": stage_conv, "u": stage_u, "qkv": stage_qkv, "v": stage_v,
        "gate_o": stage_gate_o, "gate_z": stage_gate_z, "za": stage_za,
        "heads_lo": lambda sq, bi: gmlp_heads(sq, bi, list(range(0, H_A // 2))),
        "heads_hi": lambda sq, bi: gmlp_heads(sq, bi, list(range(H_A // 2, H_A))),
        "unit": lambda sq, bi: mlstm_unit(sq, bi, *next(units)),
        "pair": lambda sq, bi: sample_state_pair(*next(pairs)),
        "ple": stage_ple, "merge_a": stage_merge_a, "merge_b": stage_merge_b, "out": stage_out,
    }
    assert STEP_ORDER.count("unit") == n_chunks * H_B and STEP_ORDER.count("pair") == rows_per_step * H_B
    for name in STEP_ORDER:
        stages[name](seqs[0], 0)


def _resident(shape):
    nd = len(shape)
    return pl.BlockSpec(shape, lambda *_: (0,) * nd, pipeline_mode=pl.Buffered(1))


def _prompt_call(x, p, weights, sample_c, sample_rows):
    batch, seq, _ = x.shape
    tt = PROMPT_TILE
    ns = SEQS_PER_STEP
    grid = (batch // ns, seq // tt)
    n_steps = grid[0] * grid[1]
    n_rows = sample_c.shape[0]
    rows_per_step = n_rows // n_steps
    assert rows_per_step * n_steps == n_rows, "sample rows are spread evenly over the prompt grid steps"

    def state_block(b, t):
        return (b * grid[1] + t, 0, 0, 0)

    in_specs = [
        pl.BlockSpec((ns, tt, D_MODEL), lambda b, t: (b, t, 0)),
        pl.BlockSpec((ns, tt, D_PLE), lambda b, t: (b, t, 0)),
    ] + [_resident(w.shape) for w in weights] + [
        pl.BlockSpec((rows_per_step, H_B, DH_B, DH_B), state_block),
    ] + [_resident(a.shape) for a in sample_rows]
    out_shape = (
        jax.ShapeDtypeStruct((batch, seq, D_MODEL), F32),
        jax.ShapeDtypeStruct((batch, H_B, DH_B, DH_B), F32),
        jax.ShapeDtypeStruct((batch, H_B, DH_B), F32),
        jax.ShapeDtypeStruct((batch, H_B, SUBLANES, LANES), F32),
        jax.ShapeDtypeStruct((batch, CONV_W - 1, W_B), F32),
        jax.ShapeDtypeStruct(sample_c.shape, F32),
        jax.ShapeDtypeStruct((n_rows, W_B), F32),
    )
    out_specs = (
        pl.BlockSpec((ns, tt, D_MODEL), lambda b, t: (b, t, 0)),
        pl.BlockSpec((ns, H_B, DH_B, DH_B), lambda b, t: (b, 0, 0, 0)),
        pl.BlockSpec((ns, H_B, DH_B), lambda b, t: (b, 0, 0)),
        pl.BlockSpec((ns, H_B, SUBLANES, LANES), lambda b, t: (b, 0, 0, 0)),
        pl.BlockSpec((ns, CONV_W - 1, W_B), lambda b, t: (b, 0, 0)),
        pl.BlockSpec((rows_per_step, H_B, DH_B, DH_B), state_block),
        pl.BlockSpec((n_rows, W_B), lambda b, t: (0, 0)),
    )
    scratch = [
        pltpu.VMEM((ns, tt + SUBLANES, W_B), F32),
        pltpu.VMEM((ns, tt, 3 * W_B), BF16),
        pltpu.VMEM((ns, tt, W_A + W_B), BF16),
    ]
    return pl.pallas_call(
        _prompt_kernel,
        grid=grid,
        in_specs=in_specs,
        out_specs=out_specs,
        out_shape=out_shape,
        scratch_shapes=scratch,
        compiler_params=pltpu.CompilerParams(
            dimension_semantics=("arbitrary", "arbitrary"),
            vmem_limit_bytes=PROMPT_VMEM_BYTES),
        name="prompt_fused",
    )(x, p, *weights, sample_c, *sample_rows)


def _sample_pre_kernel(x_ref, conv_ref, n_ref, m_ref, vec_ref, w_lo_ref, w_hi_ref,
                       w_q_ref, w_k_ref, w_v_ref, w_if_ref,
                       vn_ref, ya_ref, convnew_ref, q_ref, k_ref, vs_ref, gfull_ref, numa_ref,
                       denom_ref, osig_ref, skipxc_ref, zsilu_ref, nnew_ref, mnew_ref, w_in_bf16_ref,
                       qkv_ref, hb_ref, proj_ref):
    slab = pl.program_id(0)
    vp = _vector_params(vec_ref)

    @pl.when(slab == 0)
    def _():
        hb_ref[...] = _rmsnorm(x_ref[:, 0, :], vp["norm_g"][...]).astype(BF16)

    half = w_lo_ref.shape[1]
    for w_ref, cols in ((w_lo_ref, slice(0, half)), (w_hi_ref, slice(half, 2 * half))):
        w_half = w_ref[...].astype(BF16)
        w_in_bf16_ref[:, cols] = w_half
        proj_ref[slab, :, cols] = _dot(hb_ref[...], w_half)

    @pl.when(slab == pl.num_programs(0) - 1)
    def _():
        _sample_front_end(conv_ref, n_ref, m_ref, vp["lnv_g"], vp["lnv_b"], vp["w00"], vp["b0"], vp["conv_w"],
                          vp["conv_b"], w_q_ref, w_k_ref, w_v_ref, w_if_ref, vp["b_if"], vp["skip"],
                          vn_ref, ya_ref, convnew_ref, q_ref, k_ref, vs_ref, gfull_ref, numa_ref,
                          denom_ref, osig_ref, skipxc_ref, zsilu_ref, nnew_ref, mnew_ref, qkv_ref, proj_ref)


def _sample_front_end(conv_ref, n_ref, m_ref, lnv_g_ref, lnv_b_ref, w00_ref, b0_ref, conv_w_ref, conv_b_ref,
                      w_q_ref, w_k_ref, w_v_ref, w_if_ref, b_if_ref, skip_ref,
                      vn_ref, ya_ref, convnew_ref, q_ref, k_ref, vs_ref, gfull_ref, numa_ref,
                      denom_ref, osig_ref, skipxc_ref, zsilu_ref, nnew_ref, mnew_ref, qkv_ref, proj_ref):
    def proj(i):
        return proj_ref[i]

    u = _gelu(proj(0))
    v = _gelu(proj(1))
    z_a = proj(2)
    for hh in range(H_A):
        sl = slice(hh * DH_A, (hh + 1) * DH_A)
        vn = _layernorm(v[:, sl], lnv_g_ref[:, sl], lnv_b_ref[:, sl])
        vn_ref[:, 0, sl] = vn
        sp = vn * w00_ref[:, sl] + b0_ref[:, sl]
        ya_ref[:, sl] = (u[:, sl] * sp * _silu(z_a[:, sl])).astype(BF16)

    x_b = proj(3)
    osig_ref[...] = jax.nn.sigmoid(proj(4))
    zsilu_ref[...] = _silu(proj(5))
    c0 = conv_ref[:, 0, :]
    c1 = conv_ref[:, 1, :]
    c2 = conv_ref[:, 2, :]
    xc = c0 * conv_w_ref[0:1, :]
    xc = xc + c1 * conv_w_ref[1:2, :]
    xc = xc + c2 * conv_w_ref[2:3, :]
    xc = xc + x_b * conv_w_ref[3:4, :]
    xc = _silu(xc + conv_b_ref[...])
    convnew_ref[:, 0, :] = c1
    convnew_ref[:, 1, :] = c2
    convnew_ref[:, 2, :] = x_b
    skipxc_ref[...] = skip_ref[...] * xc
    xcb = xc.astype(BF16)
    xbb = x_b.astype(BF16)
    for hh in range(H_B):
        sl = slice(hh * DH_B, (hh + 1) * DH_B)
        qkv_ref[:, hh * DH_B:(hh + 1) * DH_B] = _dot(xcb[:, sl], w_q_ref[hh]).astype(BF16)
        qkv_ref[:, W_B + hh * DH_B:W_B + (hh + 1) * DH_B] = _dot(xcb[:, sl], w_k_ref[hh]).astype(BF16)
        qkv_ref[:, 2 * W_B + hh * DH_B:2 * W_B + (hh + 1) * DH_B] = _dot(xbb[:, sl], w_v_ref[hh]).astype(BF16)
    gates = _dot(qkv_ref[...], w_if_ref[...]) + b_if_ref[...]
    logf = pltpu.roll(jax.nn.log_sigmoid(gates), LANES - H_B, 1)
    li = gates[:, 0:H_B]
    inter = logf[:, 0:H_B] + m_ref[...]
    m_t = jnp.maximum(inter, li)
    s_w = jnp.exp(li - m_t)
    g = jnp.exp(inter - m_t)
    e_neg = jnp.exp(-m_t)
    mnew_ref[...] = m_t
    for hh in range(H_B):
        sl = slice(hh * DH_B, (hh + 1) * DH_B)
        qf = qkv_ref[:, hh * DH_B:(hh + 1) * DH_B].astype(F32)
        kf = qkv_ref[:, W_B + hh * DH_B:W_B + (hh + 1) * DH_B].astype(F32)
        vf = qkv_ref[:, 2 * W_B + hh * DH_B:2 * W_B + (hh + 1) * DH_B].astype(F32)
        s_h = s_w[:, hh:hh + 1]
        g_h = g[:, hh:hh + 1]
        n_old = n_ref[:, hh, :]
        s_qk = s_h * jnp.sum(qf * kf, axis=-1, keepdims=True)
        den = s_qk + g_h * jnp.sum(n_old * qf, axis=-1, keepdims=True)
        denom = jnp.maximum(jnp.abs(den), e_neg[:, hh:hh + 1])
        q_ref[:, sl] = qf
        k_ref[:, sl] = kf
        vs_ref[:, sl] = s_h * vf
        gfull_ref[:, sl] = jnp.broadcast_to(g_h, qf.shape)
        numa_ref[:, sl] = s_qk * vf
        denom_ref[:, sl] = jnp.broadcast_to(denom, qf.shape)
        nnew_ref[:, hh, :] = g_h * n_old + s_h * kf


def _sample_post_kernel(x_ref, p_ref, ya_ref, cq_ref, gfull_ref, numa_ref, denom_ref, osig_ref,
                        skipxc_ref, zsilu_ref, vec_ref, w_out_ref, w_pg_ref, w_pp_ref, y_ref, ycat_ref):
    vp = _vector_params(vec_ref)
    gn_g_ref, b_pg_ref, ple_g_ref, fin_g_ref = vp["gn_g"], vp["b_pg"], vp["ple_g"], vp["fin_g"]
    ycat_ref[:, 0:W_A] = ya_ref[...]
    for hh in range(H_B):
        sl = slice(hh * DH_B, (hh + 1) * DH_B)
        hcell = (numa_ref[:, sl] + gfull_ref[:, sl] * cq_ref[:, sl]) / denom_ref[:, sl]
        hn = _layernorm(hcell * osig_ref[:, sl], gn_g_ref[:, sl])
        ycat_ref[:, W_A + hh * DH_B:W_A + (hh + 1) * DH_B] = ((hn + skipxc_ref[:, sl]) * zsilu_ref[:, sl]).astype(BF16)
    x1 = x_ref[:, 0, :] + _dot(ycat_ref[...], w_out_ref[...])
    e = _rmsnorm(_dot(p_ref[:, 0, :].astype(BF16), w_pp_ref[...]), ple_g_ref[...])
    gate = jax.nn.sigmoid(_dot(x1.astype(BF16), w_pg_ref[...]) + b_pg_ref[...])
    x2 = x1 + gate * e
    y_ref[:, 0, :] = _rmsnorm(x2, fin_g_ref[...])


def _sample_pre_call(before_w, w_in_f32, after_w, out_shape):
    nb = before_w[0].shape[0]
    n_slabs = N_IN // W_A
    whole = lambda s: pl.BlockSpec(s.shape, lambda i, nd=len(s.shape): (0,) * nd)
    slab_spec = pl.BlockSpec((D_MODEL, W_A), lambda i: (0, i))
    out_shape = tuple(out_shape) + (jax.ShapeDtypeStruct(w_in_f32.shape, BF16),)
    return pl.pallas_call(
        _sample_pre_kernel,
        grid=(n_slabs,),
        in_specs=[_resident(a.shape) for a in before_w]
        + [pl.BlockSpec((D_MODEL, W_A // 2), lambda i: (0, 2 * i)),
           pl.BlockSpec((D_MODEL, W_A // 2), lambda i: (0, 2 * i + 1))]
        + [_resident(a.shape) for a in after_w],
        out_specs=tuple(whole(s) for s in out_shape[:-1]) + (slab_spec,),
        out_shape=out_shape,
        scratch_shapes=[pltpu.VMEM((nb, 3 * W_B), BF16), pltpu.VMEM((nb, D_MODEL), BF16),
                        pltpu.VMEM((n_slabs, nb, W_A), F32)],
        compiler_params=pltpu.CompilerParams(
            dimension_semantics=("arbitrary",), vmem_limit_bytes=SAMPLE_VMEM_BYTES),
        name="sample_pre",
    )(*before_w, w_in_f32, w_in_f32, *after_w)


def _single_step_call(body, inputs, out_shape, scratch, name):
    return pl.pallas_call(
        body,
        grid=(1,),
        in_specs=[_resident(a.shape) for a in inputs],
        out_specs=tuple(pl.BlockSpec(s.shape, lambda *_, nd=len(s.shape): (0,) * nd) for s in out_shape),
        out_shape=out_shape,
        scratch_shapes=scratch,
        compiler_params=pltpu.CompilerParams(
            dimension_semantics=("arbitrary",), vmem_limit_bytes=SAMPLE_VMEM_BYTES),
        name=name,
    )(*inputs)


def kernel(x_prompt, x_sample, state_mlstm_C, state_mlstm_n, state_mlstm_m, state_conv, p_prompt, p_sample,
           norm_in_g, w_in, ln_v_g, ln_v_b, w_spatial, b_spatial, conv_w, conv_b, w_q, w_k, w_v, w_if, b_if,
           gn_g, skip, w_out, w_ple_gate, b_ple_gate, w_ple_proj, ple_norm_g, final_norm_g):
    assert norm_in_g.shape[0] == 1, "single-layer trunk"
    nb = x_sample.shape[0]

    w_sp = w_spatial[0]
    b_sp_full = jnp.repeat(b_spatial[0].T, DH_A, axis=1)
    wq_b = w_q[0].astype(BF16)
    wk_b = (w_k[0] * (DH_B ** -0.5)).astype(BF16)
    wv_b = w_v[0].astype(BF16)
    k_fix = jnp.concatenate([jnp.ones((W_B, 1), F32), jnp.full((W_B, 1), DH_B ** 0.5, F32), jnp.ones((W_B, 1), F32)], 0)
    w_if_p = jnp.pad(w_if[0] * k_fix, ((0, 0), (0, LANES - 2 * H_B))).astype(BF16)
    w_out_b = w_out[0].astype(BF16)
    w_pg_b = w_ple_gate[0].astype(BF16)
    w_pp_b = w_ple_proj[0].astype(BF16)
    w00 = jnp.repeat(w_sp[:, 0, 0], DH_A)
    b0 = jnp.repeat(b_spatial[0][:, 0], DH_A)
    vec = _pack_vectors(norm_in_g[0], ln_v_g[0], ln_v_b[0], conv_b[0], gn_g[0], skip[0], b_ple_gate[0],
                        ple_norm_g[0], final_norm_g, conv_w[0], b_if[0], w00, b0)

    full = jax.ShapeDtypeStruct((nb, W_B), F32)
    pre_out = (
        jax.ShapeDtypeStruct((nb, 1, W_A), F32),
        jax.ShapeDtypeStruct((nb, W_A), BF16),
        jax.ShapeDtypeStruct((nb, CONV_W - 1, W_B), F32),
        full, full, full, full, full, full, full, full, full,
        jax.ShapeDtypeStruct((nb, H_B, DH_B), F32),
        jax.ShapeDtypeStruct((nb, H_B), F32),
    )
    (vn_s, ya_s, conv_new, q_s, k_s, vs_s, gfull, numa, denom, osig, skipxc, zsilu, n_new, m_new,
     w_in_b) = _sample_pre_call(
        [x_sample, state_conv[0], state_mlstm_n[0], state_mlstm_m[0], vec], w_in[0],
        [wq_b, wk_b, wv_b, w_if_p], pre_out)

    prompt_weights = [vec, w_in_b, w_sp, b_sp_full, wq_b, wk_b, wv_b, w_if_p, w_out_b, w_pg_b, w_pp_b]
    y_p, c_p, n_p, m_p, conv_p, c_new, cq = _prompt_call(
        x_prompt, p_prompt[0], prompt_weights, state_mlstm_C[0], [q_s, k_s, vs_s, gfull])

    (y_s,) = _single_step_call(
        _sample_post_kernel,
        [x_sample, p_sample[0], ya_s, cq, gfull, numa, denom, osig, skipxc, zsilu, vec, w_out_b, w_pg_b, w_pp_b],
        (jax.ShapeDtypeStruct((nb, 1, D_MODEL), F32),), [pltpu.VMEM((nb, W_A + W_B), BF16)], "sample_post")

    return (y_p, y_s, c_p[None], n_p[None], m_p[:, :, 0, 0][None], conv_p[None],
            c_new[None], n_new[None], m_new[None], conv_new[None], vn_s[None])
```

```python
import jax
import jax.numpy as jnp
from jax import lax
from jax.experimental import pallas as pl
from jax.experimental.pallas import tpu as pltpu

D_MODEL = 1024
W_A = 1024
H_A = 8
DH_A = 128
W_B = 1024
H_B = 4
DH_B = 256
CHUNK = 128
CONV_W = 4
D_PLE = 256
N_IN = 3 * W_A + 3 * W_B
EPS = 1e-6

LANES = 128
SUBLANES = 8
PROMPT_TILE = 256
SEQS_PER_STEP = 1
STEP_ORDER = (
    "ple", "norm", "conv", "u", "v", "za", "qkv", "heads_lo", "gate_o", "heads_hi", "gate_z",
    "unit", "unit", "unit", "unit", "unit", "merge_a", "unit", "unit", "unit",
    "merge_b", "pair", "pair", "pair", "pair", "out", "pair", "pair", "pair", "pair",
)
PROMPT_VMEM_BYTES = 56 * 1024 * 1024
SAMPLE_VMEM_BYTES = 48 * 1024 * 1024

F32 = jnp.float32
BF16 = jnp.bfloat16
NT_DIMS = (((1,), (1,)), ((), ()))
TN_DIMS = (((0,), (0,)), ((), ()))


def _rmsnorm(x, g):
    return x * lax.rsqrt(jnp.mean(x * x, axis=-1, keepdims=True) + EPS) * g


def _layernorm(x, g, b=None):
    mu = jnp.mean(x, axis=-1, keepdims=True)
    d = x - mu
    var = jnp.mean(d * d, axis=-1, keepdims=True)
    y = d * lax.rsqrt(var + EPS) * g
    return y if b is None else y + b


def _gelu(x):
    return 0.5 * x * (1.0 + lax.erf(x * (0.5 ** 0.5)))


def _silu(x):
    return x * jax.nn.sigmoid(x)


def _dot(a, b):
    return jnp.dot(a, b, preferred_element_type=F32)


def _cumsum_rows(tril_b, x):
    x1 = x.astype(BF16)
    r1 = x - x1.astype(F32)
    x2 = r1.astype(BF16)
    x3 = (r1 - x2.astype(F32)).astype(BF16)
    return _dot(tril_b, x1) + _dot(tril_b, x2) + _dot(tril_b, x3)


def _vector_params(vec_ref):
    row = lambda i: vec_ref.at[i:i + 1]
    return dict(norm_g=row(0), lnv_g=row(1), lnv_b=row(2), conv_b=row(3), gn_g=row(4), skip=row(5), b_pg=row(6),
                ple_g=row(7), fin_g=row(8), conv_w=vec_ref.at[9:9 + CONV_W], b_if=vec_ref.at[13:14, 0:LANES],
                w00=row(14), b0=row(15))


def _pack_vectors(norm_g, lnv_g, lnv_b, conv_b, gn_g, skip, b_pg, ple_g, fin_g, conv_w, b_if, w00, b0):
    wide = lambda a: jnp.pad(a.reshape(1, -1).astype(F32), ((0, 0), (0, D_MODEL - a.size)))
    rows = [wide(a) for a in (norm_g, lnv_g, lnv_b, conv_b, gn_g, skip, b_pg, ple_g, fin_g)]
    return jnp.concatenate(rows + [conv_w.astype(F32), wide(b_if), wide(w00), wide(b0)], axis=0)


def _qkv_weights(w_qkv_ref):
    return w_qkv_ref.at[0:H_B], w_qkv_ref.at[H_B:2 * H_B], w_qkv_ref.at[2 * H_B:3 * H_B]


def _tail_weights(w_tail_ref):
    return (w_tail_ref.at[0:W_A + W_B], w_tail_ref.at[W_A + W_B:W_A + W_B + D_MODEL],
            w_tail_ref.at[W_A + W_B + D_MODEL:W_A + W_B + D_MODEL + D_PLE])


def _prompt_kernel(x_ref, p_ref, vec_ref, w_in_ref, w_sp_ref, b_sp_ref, w_qkv_ref, w_if_ref, w_tail_ref,
                   sc_ref, sq_ref, sk_ref, svs_ref, sg_ref,
                   y_ref, c_ref, n_ref, m_ref, conv_ref, scnew_ref, scq_ref,
                   xpad_ref, qkv_ref, ycat_ref):
    w_q_ref, w_k_ref, w_v_ref = _qkv_weights(w_qkv_ref)
    w_out_ref, w_pg_ref, w_pp_ref = _tail_weights(w_tail_ref)
    vp = _vector_params(vec_ref)
    norm_g_ref, lnv_g_ref, lnv_b_ref, conv_w_ref, conv_b_ref = (
        vp["norm_g"], vp["lnv_g"], vp["lnv_b"], vp["conv_w"], vp["conv_b"])
    b_if_ref, gn_g_ref, skip_ref, b_pg_ref, ple_g_ref, fin_g_ref = (
        vp["b_if"], vp["gn_g"], vp["skip"], vp["b_pg"], vp["ple_g"], vp["fin_g"])
    n_seq, tt = x_ref.shape[0], x_ref.shape[1]
    n_chunks = tt // CHUNK
    step = pl.program_id(0) * pl.num_programs(1) + pl.program_id(1)

    @pl.when(pl.program_id(1) == 0)
    def _():
        c_ref[...] = jnp.zeros_like(c_ref)
        n_ref[...] = jnp.zeros_like(n_ref)
        m_ref[...] = jnp.zeros_like(m_ref)
        conv_ref[...] = jnp.zeros_like(conv_ref)

    @pl.when(step == 0)
    def _():
        scq_ref[...] = jnp.zeros_like(scq_ref)

    rows_per_step = sc_ref.shape[0]
    group_row = lax.broadcasted_iota(jnp.int32, (SUBLANES, DH_B), 0)

    def sample_state_pair(j, hh):
        sl = slice(hh * DH_B, (hh + 1) * DH_B)
        r = step * rows_per_step + j
        base = pl.multiple_of((r // SUBLANES) * SUBLANES, SUBLANES)
        is_row = group_row == (r - base)
        group = pl.ds(base, SUBLANES)
        c_old = sc_ref[j, hh]
        cq = lax.dot_general(sq_ref[group, sl].astype(BF16), c_old.astype(BF16), NT_DIMS, preferred_element_type=F32)
        scq_ref[group, sl] = jnp.where(is_row, cq, scq_ref[group, sl])
        k_row = jnp.where(is_row, sk_ref[group, sl], 0.0).astype(BF16)
        outer = lax.dot_general(svs_ref[group, sl].astype(BF16), k_row, TN_DIMS, preferred_element_type=F32)
        scnew_ref[j, hh] = sg_ref[pl.ds(r, 1), sl] * c_old + outer

    row = lax.broadcasted_iota(jnp.int32, (CHUNK, CHUNK), 0)
    col = lax.broadcasted_iota(jnp.int32, (CHUNK, CHUNK), 1)
    causal = row >= col
    tril = causal.astype(F32)
    tril_b = causal.astype(BF16)

    class Seq:
        pass

    seqs = [Seq() for _ in range(n_seq)]

    def proj(sq, i):
        return _dot(sq.hb, w_in_ref[:, i * W_A:(i + 1) * W_A])

    def stage_norm(sq, bi):
        sq.hb = _rmsnorm(x_ref[bi], norm_g_ref[...]).astype(BF16)

    def stage_u(sq, bi):
        sq.u = _gelu(proj(sq, 0))

    def stage_v(sq, bi):
        sq.v = _gelu(proj(sq, 1))

    def stage_za(sq, bi):
        sq.z_a = proj(sq, 2)

    def gmlp_heads(sq, bi, heads):
        for hh in heads:
            sl = slice(hh * DH_A, (hh + 1) * DH_A)
            vn = _layernorm(sq.v[:, sl], lnv_g_ref[:, sl], lnv_b_ref[:, sl]).astype(BF16)
            wm = (w_sp_ref[hh] * tril).astype(BF16)
            vn_wide = jnp.concatenate([vn[c * CHUNK:(c + 1) * CHUNK] for c in range(n_chunks)], axis=1)
            sp_wide = _dot(wm, vn_wide)
            for c in range(n_chunks):
                rows = slice(c * CHUNK, (c + 1) * CHUNK)
                sp = sp_wide[:, c * DH_A:(c + 1) * DH_A] + b_sp_ref[:, sl]
                ycat_ref[bi, rows, sl] = (sq.u[rows, sl] * sp * _silu(sq.z_a[rows, sl])).astype(BF16)

    def stage_conv(sq, bi):
        x_b = proj(sq, 3)
        xpad_ref[bi, SUBLANES - 3:SUBLANES, :] = conv_ref[bi]
        xpad_ref[bi, SUBLANES:SUBLANES + tt, :] = x_b
        conv_ref[bi] = x_b[tt - 3:tt, :]
        xc = xpad_ref[bi, SUBLANES - 3:SUBLANES - 3 + tt, :] * conv_w_ref[0:1, :]
        xc = xc + xpad_ref[bi, SUBLANES - 2:SUBLANES - 2 + tt, :] * conv_w_ref[1:2, :]
        xc = xc + xpad_ref[bi, SUBLANES - 1:SUBLANES - 1 + tt, :] * conv_w_ref[2:3, :]
        xc = xc + x_b * conv_w_ref[3:4, :]
        sq.xc = _silu(xc + conv_b_ref[...])
        sq.xcb = sq.xc.astype(BF16)
        sq.xbb = x_b.astype(BF16)

    def stage_gate_o(sq, bi):
        sq.o_sig = jax.nn.sigmoid(proj(sq, 4))

    def stage_gate_z(sq, bi):
        sq.z_silu = _silu(proj(sq, 5))

    def stage_qkv(sq, bi):
        for hh in range(H_B):
            sl = slice(hh * DH_B, (hh + 1) * DH_B)
            qkv_ref[bi, :, hh * DH_B:(hh + 1) * DH_B] = _dot(sq.xcb[:, sl], w_q_ref[hh]).astype(BF16)
            qkv_ref[bi, :, W_B + hh * DH_B:W_B + (hh + 1) * DH_B] = _dot(sq.xcb[:, sl], w_k_ref[hh]).astype(BF16)
            qkv_ref[bi, :, 2 * W_B + hh * DH_B:2 * W_B + (hh + 1) * DH_B] = _dot(sq.xbb[:, sl], w_v_ref[hh]).astype(BF16)
        gates = _dot(qkv_ref[bi], w_if_ref[...]) + b_if_ref[...]
        logf = jax.nn.log_sigmoid(gates)
        sq.decay = []
        for c in range(n_chunks):
            rows = slice(c * CHUNK, (c + 1) * CHUNK)
            bcum = pltpu.roll(_cumsum_rows(tril_b, logf[rows]), LANES - H_B, 1)
            a_all = gates[rows] - bcum
            sq.decay.append((bcum, a_all, a_all.T))

    def mlstm_unit(sq, bi, c, hh):
        rows = slice(c * CHUNK, (c + 1) * CHUNK)
        sl = slice(hh * DH_B, (hh + 1) * DH_B)
        bcum, a_all, a_t = sq.decay[c]
        qh = qkv_ref[bi, rows, hh * DH_B:(hh + 1) * DH_B]
        kh = qkv_ref[bi, rows, W_B + hh * DH_B:W_B + (hh + 1) * DH_B]
        vh = qkv_ref[bi, rows, 2 * W_B + hh * DH_B:2 * W_B + (hh + 1) * DH_B]
        m_prev = m_ref[bi, hh, 0:1, 0:1]
        a_mat = jnp.where(causal, a_t[hh:hh + 1, :], -jnp.inf)
        m_run = jnp.maximum(jnp.max(a_mat, axis=-1, keepdims=True), m_prev)
        s = jnp.exp(a_mat - m_run) * lax.dot_general(qh, kh, NT_DIMS, preferred_element_type=F32)
        g = jnp.exp(m_prev - m_run)
        c_old = c_ref[bi, hh]
        inter = lax.dot_general(qh, c_old.astype(BF16), NT_DIMS, preferred_element_type=F32)
        num = _dot(s.astype(BF16), vh) + g * inter
        n_old = n_ref[bi, hh:hh + 1, :]
        den = jnp.sum(s, axis=-1, keepdims=True) + g * jnp.sum(qh.astype(F32) * n_old, axis=-1, keepdims=True)
        m_t = bcum[:, hh:hh + 1] + m_run
        hcell = num / jnp.maximum(jnp.abs(den), jnp.exp(-m_t))
        m_last = m_run[CHUNK - 1:CHUNK, :]
        w_end = jnp.exp(a_all[:, hh:hh + 1] - m_last)
        g_end = jnp.exp(m_prev - m_last)
        kw = kh.astype(F32) * w_end
        c_ref[bi, hh] = g_end * c_old + lax.dot_general(vh, kw.astype(BF16), TN_DIMS, preferred_element_type=F32)
        n_ref[bi, hh:hh + 1, :] = g_end * n_old + jnp.sum(kw, axis=0, keepdims=True)
        m_ref[bi, hh] = jnp.broadcast_to(m_t[CHUNK - 1:CHUNK, :], (SUBLANES, LANES))
        hn = _layernorm(hcell * sq.o_sig[rows, sl], gn_g_ref[:, sl])
        yb = (hn + skip_ref[:, sl] * sq.xc[rows, sl]) * sq.z_silu[rows, sl]
        ycat_ref[bi, rows, W_A + hh * DH_B:W_A + (hh + 1) * DH_B] = yb.astype(BF16)

    def stage_ple(sq, bi):
        sq.e = _rmsnorm(_dot(p_ref[bi].astype(BF16), w_pp_ref[...]), ple_g_ref[...])

    def stage_merge_a(sq, bi):
        sq.x1 = x_ref[bi] + _dot(ycat_ref[bi, :, :W_A], w_out_ref[:W_A, :])

    def stage_merge_b(sq, bi):
        sq.x1 = sq.x1 + _dot(ycat_ref[bi, :, W_A:], w_out_ref[W_A:, :])

    def stage_out(sq, bi):
        gate = jax.nn.sigmoid(_dot(sq.x1.astype(BF16), w_pg_ref[...]) + b_pg_ref[...])
        x2 = sq.x1 + gate * sq.e
        y_ref[bi] = _rmsnorm(x2, fin_g_ref[...])

    units = iter([(c, hh) for c in range(n_chunks) for hh in range(H_B)])
    pairs = iter([(j, hh) for j in range(rows_per_step) for hh in range(H_B)])
    stages = {
        "norm": stage_norm, "conv": stage_conv, "u": stage_u, "qkv": stage_qkv, "v": stage_v,
        "gate_o": stage_gate_o, "gate_z": stage_gate_z, "za": stage_za,
        "heads_lo": lambda sq, bi: gmlp_heads(sq, bi, list(range(0, H_A // 2))),
        "heads_hi": lambda sq, bi: gmlp_heads(sq, bi, list(range(H_A // 2, H_A))),
        "unit": lambda sq, bi: mlstm_unit(sq, bi, *next(units)),
        "pair": lambda sq, bi: sample_state_pair(*next(pairs)),
        "ple": stage_ple, "merge_a": stage_merge_a, "merge_b": stage_merge_b, "out": stage_out,
    }
    assert STEP_ORDER.count("unit") == n_chunks * H_B and STEP_ORDER.count("pair") == rows_per_step * H_B
    for name in STEP_ORDER:
        stages[name](seqs[0], 0)


def _resident(shape):
    nd = len(shape)
    return pl.BlockSpec(shape, lambda *_: (0,) * nd, pipeline_mode=pl.Buffered(1))


def _prompt_call(x, p, weights, sample_c, sample_rows):
    batch, seq, _ = x.shape
    tt = PROMPT_TILE
    ns = SEQS_PER_STEP
    grid = (batch // ns, seq // tt)
    n_steps = grid[0] * grid[1]
    n_rows = sample_c.shape[0]
    rows_per_step = n_rows // n_steps
    assert rows_per_step * n_steps == n_rows, "sample rows are spread evenly over the prompt grid steps"

    def state_block(b, t):
        return (b * grid[1] + t, 0, 0, 0)

    in_specs = [
        pl.BlockSpec((ns, tt, D_MODEL), lambda b, t: (b, t, 0)),
        pl.BlockSpec((ns, tt, D_PLE), lambda b, t: (b, t, 0)),
    ] + [_resident(w.shape) for w in weights] + [
        pl.BlockSpec((rows_per_step, H_B, DH_B, DH_B), state_block),
    ] + [_resident(a.shape) for a in sample_rows]
    out_shape = (
        jax.ShapeDtypeStruct((batch, seq, D_MODEL), F32),
        jax.ShapeDtypeStruct((batch, H_B, DH_B, DH_B), F32),
        jax.ShapeDtypeStruct((batch, H_B, DH_B), F32),
        jax.ShapeDtypeStruct((batch, H_B, SUBLANES, LANES), F32),
        jax.ShapeDtypeStruct((batch, CONV_W - 1, W_B), F32),
        jax.ShapeDtypeStruct(sample_c.shape, F32),
        jax.ShapeDtypeStruct((n_rows, W_B), F32),
    )
    out_specs = (
        pl.BlockSpec((ns, tt, D_MODEL), lambda b, t: (b, t, 0)),
        pl.BlockSpec((ns, H_B, DH_B, DH_B), lambda b, t: (b, 0, 0, 0)),
        pl.BlockSpec((ns, H_B, DH_B), lambda b, t: (b, 0, 0)),
        pl.BlockSpec((ns, H_B, SUBLANES, LANES), lambda b, t: (b, 0, 0, 0)),
        pl.BlockSpec((ns, CONV_W - 1, W_B), lambda b, t: (b, 0, 0)),
        pl.BlockSpec((rows_per_step, H_B, DH_B, DH_B), state_block),
        pl.BlockSpec((n_rows, W_B), lambda b, t: (0, 0)),
    )
    scratch = [
        pltpu.VMEM((ns, tt + SUBLANES, W_B), F32),
        pltpu.VMEM((ns, tt, 3 * W_B), BF16),
        pltpu.VMEM((ns, tt, W_A + W_B), BF16),
    ]
    return pl.pallas_call(
        _prompt_kernel,
        grid=grid,
        in_specs=in_specs,
        out_specs=out_specs,
        out_shape=out_shape,
        scratch_shapes=scratch,
        compiler_params=pltpu.CompilerParams(
            dimension_semantics=("arbitrary", "arbitrary"),
            vmem_limit_bytes=PROMPT_VMEM_BYTES),
        name="prompt_fused",
    )(x, p, *weights, sample_c, *sample_rows)


def _sample_pre_kernel(x_ref, conv_ref, n_ref, m_ref, vec_ref, w_lo_ref, w_hi_ref, w_qkv_ref, w_if_ref,
                       vn_ref, ya_ref, convnew_ref, q_ref, k_ref, vs_ref, gfull_ref, numa_ref,
                       denom_ref, osig_ref, skipxc_ref, zsilu_ref, nnew_ref, mnew_ref, w_in_bf16_ref,
                       qkv_ref, hb_ref, proj_ref):
    slab = pl.program_id(0)
    vp = _vector_params(vec_ref)

    @pl.when(slab == 0)
    def _():
        hb_ref[...] = _rmsnorm(x_ref[:, 0, :], vp["norm_g"][...]).astype(BF16)

    half = w_lo_ref.shape[1]
    for w_ref, cols in ((w_lo_ref, slice(0, half)), (w_hi_ref, slice(half, 2 * half))):
        w_half = w_ref[...].astype(BF16)
        w_in_bf16_ref[:, cols] = w_half
        proj_ref[slab, :, cols] = _dot(hb_ref[...], w_half)

    @pl.when(slab == pl.num_programs(0) - 1)
    def _():
        _sample_front_end(conv_ref, n_ref, m_ref, vp["lnv_g"], vp["lnv_b"], vp["w00"], vp["b0"], vp["conv_w"],
                          vp["conv_b"], *_qkv_weights(w_qkv_ref), w_if_ref, vp["b_if"], vp["skip"],
                          vn_ref, ya_ref, convnew_ref, q_ref, k_ref, vs_ref, gfull_ref, numa_ref,
                          denom_ref, osig_ref, skipxc_ref, zsilu_ref, nnew_ref, mnew_ref, qkv_ref, proj_ref)


def _sample_front_end(conv_ref, n_ref, m_ref, lnv_g_ref, lnv_b_ref, w00_ref, b0_ref, conv_w_ref, conv_b_ref,
                      w_q_ref, w_k_ref, w_v_ref, w_if_ref, b_if_ref, skip_ref,
                      vn_ref, ya_ref, convnew_ref, q_ref, k_ref, vs_ref, gfull_ref, numa_ref,
                      denom_ref, osig_ref, skipxc_ref, zsilu_ref, nnew_ref, mnew_ref, qkv_ref, proj_ref):
    def proj(i):
        return proj_ref[i]

    u = _gelu(proj(0))
    v = _gelu(proj(1))
    z_a = proj(2)
    for hh in range(H_A):
        sl = slice(hh * DH_A, (hh + 1) * DH_A)
        vn = _layernorm(v[:, sl], lnv_g_ref[:, sl], lnv_b_ref[:, sl])
        vn_ref[:, 0, sl] = vn
        sp = vn * w00_ref[:, sl] + b0_ref[:, sl]
        ya_ref[:, sl] = (u[:, sl] * sp * _silu(z_a[:, sl])).astype(BF16)

    x_b = proj(3)
    osig_ref[...] = jax.nn.sigmoid(proj(4))
    zsilu_ref[...] = _silu(proj(5))
    c0 = conv_ref[:, 0, :]
    c1 = conv_ref[:, 1, :]
    c2 = conv_ref[:, 2, :]
    xc = c0 * conv_w_ref[0:1, :]
    xc = xc + c1 * conv_w_ref[1:2, :]
    xc = xc + c2 * conv_w_ref[2:3, :]
    xc = xc + x_b * conv_w_ref[3:4, :]
    xc = _silu(xc + conv_b_ref[...])
    convnew_ref[:, 0, :] = c1
    convnew_ref[:, 1, :] = c2
    convnew_ref[:, 2, :] = x_b
    skipxc_ref[...] = skip_ref[...] * xc
    xcb = xc.astype(BF16)
    xbb = x_b.astype(BF16)
    for hh in range(H_B):
        sl = slice(hh * DH_B, (hh + 1) * DH_B)
        qkv_ref[:, hh * DH_B:(hh + 1) * DH_B] = _dot(xcb[:, sl], w_q_ref[hh]).astype(BF16)
        qkv_ref[:, W_B + hh * DH_B:W_B + (hh + 1) * DH_B] = _dot(xcb[:, sl], w_k_ref[hh]).astype(BF16)
        qkv_ref[:, 2 * W_B + hh * DH_B:2 * W_B + (hh + 1) * DH_B] = _dot(xbb[:, sl], w_v_ref[hh]).astype(BF16)
    gates = _dot(qkv_ref[...], w_if_ref[...]) + b_if_ref[...]
    logf = pltpu.roll(jax.nn.log_sigmoid(gates), LANES - H_B, 1)
    li = gates[:, 0:H_B]
    inter = logf[:, 0:H_B] + m_ref[...]
    m_t = jnp.maximum(inter, li)
    s_w = jnp.exp(li - m_t)
    g = jnp.exp(inter - m_t)
    e_neg = jnp.exp(-m_t)
    mnew_ref[...] = m_t
    for hh in range(H_B):
        sl = slice(hh * DH_B, (hh + 1) * DH_B)
        qf = qkv_ref[:, hh * DH_B:(hh + 1) * DH_B].astype(F32)
        kf = qkv_ref[:, W_B + hh * DH_B:W_B + (hh + 1) * DH_B].astype(F32)
        vf = qkv_ref[:, 2 * W_B + hh * DH_B:2 * W_B + (hh + 1) * DH_B].astype(F32)
        s_h = s_w[:, hh:hh + 1]
        g_h = g[:, hh:hh + 1]
        n_old = n_ref[:, hh, :]
        s_qk = s_h * jnp.sum(qf * kf, axis=-1, keepdims=True)
        den = s_qk + g_h * jnp.sum(n_old * qf, axis=-1, keepdims=True)
        denom = jnp.maximum(jnp.abs(den), e_neg[:, hh:hh + 1])
        q_ref[:, sl] = qf
        k_ref[:, sl] = kf
        vs_ref[:, sl] = s_h * vf
        gfull_ref[:, sl] = jnp.broadcast_to(g_h, qf.shape)
        numa_ref[:, sl] = s_qk * vf
        denom_ref[:, sl] = jnp.broadcast_to(denom, qf.shape)
        nnew_ref[:, hh, :] = g_h * n_old + s_h * kf


def _sample_post_kernel(x_ref, p_ref, ya_ref, cq_ref, gfull_ref, numa_ref, denom_ref, osig_ref,
                        skipxc_ref, zsilu_ref, vec_ref, w_tail_ref, y_ref, ycat_ref):
    w_out_ref, w_pg_ref, w_pp_ref = _tail_weights(w_tail_ref)
    vp = _vector_params(vec_ref)
    gn_g_ref, b_pg_ref, ple_g_ref, fin_g_ref = vp["gn_g"], vp["b_pg"], vp["ple_g"], vp["fin_g"]
    ycat_ref[:, 0:W_A] = ya_ref[...]
    for hh in range(H_B):
        sl = slice(hh * DH_B, (hh + 1) * DH_B)
        hcell = (numa_ref[:, sl] + gfull_ref[:, sl] * cq_ref[:, sl]) / denom_ref[:, sl]
        hn = _layernorm(hcell * osig_ref[:, sl], gn_g_ref[:, sl])
        ycat_ref[:, W_A + hh * DH_B:W_A + (hh + 1) * DH_B] = ((hn + skipxc_ref[:, sl]) * zsilu_ref[:, sl]).astype(BF16)
    x1 = x_ref[:, 0, :] + _dot(ycat_ref[...], w_out_ref[...])
    e = _rmsnorm(_dot(p_ref[:, 0, :].astype(BF16), w_pp_ref[...]), ple_g_ref[...])
    gate = jax.nn.sigmoid(_dot(x1.astype(BF16), w_pg_ref[...]) + b_pg_ref[...])
    x2 = x1 + gate * e
    y_ref[:, 0, :] = _rmsnorm(x2, fin_g_ref[...])


def _sample_pre_call(before_w, w_in_f32, after_w, out_shape):
    nb = before_w[0].shape[0]
    n_slabs = N_IN // W_A
    whole = lambda s: pl.BlockSpec(s.shape, lambda i, nd=len(s.shape): (0,) * nd)
    slab_spec = pl.BlockSpec((D_MODEL, W_A), lambda i: (0, i))
    out_shape = tuple(out_shape) + (jax.ShapeDtypeStruct(w_in_f32.shape, BF16),)
    return pl.pallas_call(
        _sample_pre_kernel,
        grid=(n_slabs,),
        in_specs=[_resident(a.shape) for a in before_w]
        + [pl.BlockSpec((D_MODEL, W_A // 2), lambda i: (0, 2 * i)),
           pl.BlockSpec((D_MODEL, W_A // 2), lambda i: (0, 2 * i + 1))]
        + [_resident(a.shape) for a in after_w],
        out_specs=tuple(whole(s) for s in out_shape[:-1]) + (slab_spec,),
        out_shape=out_shape,
        scratch_shapes=[pltpu.VMEM((nb, 3 * W_B), BF16), pltpu.VMEM((nb, D_MODEL), BF16),
                        pltpu.VMEM((n_slabs, nb, W_A), F32)],
        compiler_params=pltpu.CompilerParams(
            dimension_semantics=("arbitrary",), vmem_limit_bytes=SAMPLE_VMEM_BYTES),
        name="sample_pre",
    )(*before_w, w_in_f32, w_in_f32, *after_w)


def _single_step_call(body, inputs, out_shape, scratch, name):
    return pl.pallas_call(
        body,
        grid=(1,),
        in_specs=[_resident(a.shape) for a in inputs],
        out_specs=tuple(pl.BlockSpec(s.shape, lambda *_, nd=len(s.shape): (0,) * nd) for s in out_shape),
        out_shape=out_shape,
        scratch_shapes=scratch,
        compiler_params=pltpu.CompilerParams(
            dimension_semantics=("arbitrary",), vmem_limit_bytes=SAMPLE_VMEM_BYTES),
        name=name,
    )(*inputs)


def kernel(x_prompt, x_sample, state_mlstm_C, state_mlstm_n, state_mlstm_m, state_conv, p_prompt, p_sample,
           norm_in_g, w_in, ln_v_g, ln_v_b, w_spatial, b_spatial, conv_w, conv_b, w_q, w_k, w_v, w_if, b_if,
           gn_g, skip, w_out, w_ple_gate, b_ple_gate, w_ple_proj, ple_norm_g, final_norm_g):
    assert norm_in_g.shape[0] == 1, "single-layer trunk"
    nb = x_sample.shape[0]

    w_sp = w_spatial[0]
    b_sp_full = jnp.repeat(b_spatial[0].T, DH_A, axis=1)
    w_qkv_b = jnp.concatenate([w_q[0], w_k[0] * (DH_B ** -0.5), w_v[0]], axis=0).astype(BF16)
    k_fix = jnp.concatenate([jnp.ones((W_B, 1), F32), jnp.full((W_B, 1), DH_B ** 0.5, F32), jnp.ones((W_B, 1), F32)], 0)
    w_if_p = jnp.pad(w_if[0] * k_fix, ((0, 0), (0, LANES - 2 * H_B))).astype(BF16)
    w_tail_b = jnp.concatenate([w_out[0], w_ple_gate[0], w_ple_proj[0]], axis=0).astype(BF16)
    w00 = jnp.repeat(w_sp[:, 0, 0], DH_A)
    b0 = jnp.repeat(b_spatial[0][:, 0], DH_A)
    vec = _pack_vectors(norm_in_g[0], ln_v_g[0], ln_v_b[0], conv_b[0], gn_g[0], skip[0], b_ple_gate[0],
                        ple_norm_g[0], final_norm_g, conv_w[0], b_if[0], w00, b0)

    full = jax.ShapeDtypeStruct((nb, W_B), F32)
    pre_out = (
        jax.ShapeDtypeStruct((nb, 1, W_A), F32),
        jax.ShapeDtypeStruct((nb, W_A), BF16),
        jax.ShapeDtypeStruct((nb, CONV_W - 1, W_B), F32),
        full, full, full, full, full, full, full, full, full,
        jax.ShapeDtypeStruct((nb, H_B, DH_B), F32),
        jax.ShapeDtypeStruct((nb, H_B), F32),
    )
    (vn_s, ya_s, conv_new, q_s, k_s, vs_s, gfull, numa, denom, osig, skipxc, zsilu, n_new, m_new,
     w_in_b) = _sample_pre_call(
        [x_sample, state_conv[0], state_mlstm_n[0], state_mlstm_m[0], vec], w_in[0],
        [w_qkv_b, w_if_p], pre_out)

    prompt_weights = [vec, w_in_b, w_sp, b_sp_full, w_qkv_b, w_if_p, w_tail_b]
    y_p, c_p, n_p, m_p, conv_p, c_new, cq = _prompt_call(
        x_prompt, p_prompt[0], prompt_weights, state_mlstm_C[0], [q_s, k_s, vs_s, gfull])

    (y_s,) = _single_step_call(
        _sample_post_kernel,
        [x_sample, p_sample[0], ya_s, cq, gfull, numa, denom, osig, skipxc, zsilu, vec, w_tail_b],
        (jax.ShapeDtypeStruct((nb, 1, D_MODEL), F32),), [pltpu.VMEM((nb, W_A + W_B), BF16)], "sample_post")

    return (y_p, y_s, c_p[None], n_p[None], m_p[:, :, 0, 0][None], conv_p[None],
            c_new[None], n_new[None], m_new[None], conv_new[None], vn_s[None])
```

```python
import jax
import jax.numpy as jnp
from jax import lax
from jax.experimental import pallas as pl
from jax.experimental.pallas import tpu as pltpu

D_MODEL = 1024
W_A = 1024
H_A = 8
DH_A = 128
W_B = 1024
H_B = 4
DH_B = 256
CHUNK = 128
CONV_W = 4
D_PLE = 256
N_IN = 3 * W_A + 3 * W_B
EPS = 1e-6

LANES = 128
SUBLANES = 8
PROMPT_TILE = 256
SEQS_PER_STEP = 1
STEP_ORDER = (
    "ple", "norm", "conv", "u", "v", "za", "qkv", "heads_lo", "gate_o", "heads_hi", "gate_z",
    "unit", "unit", "unit", "unit", "unit", "merge_a", "unit", "unit", "unit",
    "merge_b", "pair", "pair", "pair", "out", "pair", "pair", "pair", "pair", "pair",
)
PROMPT_VMEM_BYTES = 56 * 1024 * 1024
SAMPLE_VMEM_BYTES = 48 * 1024 * 1024

F32 = jnp.float32
BF16 = jnp.bfloat16
NT_DIMS = (((1,), (1,)), ((), ()))
TN_DIMS = (((0,), (0,)), ((), ()))


def _rmsnorm(x, g):
    return x * lax.rsqrt(jnp.mean(x * x, axis=-1, keepdims=True) + EPS) * g


def _layernorm(x, g, b=None):
    mu = jnp.mean(x, axis=-1, keepdims=True)
    d = x - mu
    var = jnp.mean(d * d, axis=-1, keepdims=True)
    y = d * lax.rsqrt(var + EPS) * g
    return y if b is None else y + b


def _gelu(x):
    return 0.5 * x * (1.0 + lax.erf(x * (0.5 ** 0.5)))


def _silu(x):
    return x * jax.nn.sigmoid(x)


def _dot(a, b):
    return jnp.dot(a, b, preferred_element_type=F32)


def _cumsum_rows(tril_b, x):
    x1 = x.astype(BF16)
    r1 = x - x1.astype(F32)
    x2 = r1.astype(BF16)
    x3 = (r1 - x2.astype(F32)).astype(BF16)
    return _dot(tril_b, x1) + _dot(tril_b, x2) + _dot(tril_b, x3)


def _vector_params(vec_ref):
    row = lambda i: vec_ref.at[i:i + 1]
    return dict(norm_g=row(0), lnv_g=row(1), lnv_b=row(2), conv_b=row(3), gn_g=row(4), skip=row(5), b_pg=row(6),
                ple_g=row(7), fin_g=row(8), conv_w=vec_ref.at[9:9 + CONV_W], b_if=vec_ref.at[13:14, 0:LANES],
                w00=row(14), b0=row(15))


def _pack_vectors(norm_g, lnv_g, lnv_b, conv_b, gn_g, skip, b_pg, ple_g, fin_g, conv_w, b_if, w00, b0):
    wide = lambda a: jnp.pad(a.reshape(1, -1).astype(F32), ((0, 0), (0, D_MODEL - a.size)))
    rows = [wide(a) for a in (norm_g, lnv_g, lnv_b, conv_b, gn_g, skip, b_pg, ple_g, fin_g)]
    return jnp.concatenate(rows + [conv_w.astype(F32), wide(b_if), wide(w00), wide(b0)], axis=0)


def _prompt_kernel(x_ref, p_ref, vec_ref, w_in_ref, w_sp_ref, b_sp_ref, w_q_ref, w_k_ref, w_v_ref, w_if_ref,
                   w_out_ref, w_pg_ref, w_pp_ref,
                   sc_ref, sq_ref, sk_ref, svs_ref, sg_ref,
                   y_ref, c_ref, n_ref, m_ref, conv_ref, scnew_ref, scq_ref,
                   xpad_ref, qkv_ref, ycat_ref):
    vp = _vector_params(vec_ref)
    norm_g_ref, lnv_g_ref, lnv_b_ref, conv_w_ref, conv_b_ref = (
        vp["norm_g"], vp["lnv_g"], vp["lnv_b"], vp["conv_w"], vp["conv_b"])
    b_if_ref, gn_g_ref, skip_ref, b_pg_ref, ple_g_ref, fin_g_ref = (
        vp["b_if"], vp["gn_g"], vp["skip"], vp["b_pg"], vp["ple_g"], vp["fin_g"])
    n_seq, tt = x_ref.shape[0], x_ref.shape[1]
    n_chunks = tt // CHUNK
    step = pl.program_id(0) * pl.num_programs(1) + pl.program_id(1)

    @pl.when(pl.program_id(1) == 0)
    def _():
        c_ref[...] = jnp.zeros_like(c_ref)
        n_ref[...] = jnp.zeros_like(n_ref)
        m_ref[...] = jnp.zeros_like(m_ref)
        conv_ref[...] = jnp.zeros_like(conv_ref)

    @pl.when(step == 0)
    def _():
        scq_ref[...] = jnp.zeros_like(scq_ref)

    rows_per_step = sc_ref.shape[0]
    group_row = lax.broadcasted_iota(jnp.int32, (SUBLANES, DH_B), 0)

    def sample_state_pair(j, hh):
        sl = slice(hh * DH_B, (hh + 1) * DH_B)
        r = step * rows_per_step + j
        base = pl.multiple_of((r // SUBLANES) * SUBLANES, SUBLANES)
        is_row = group_row == (r - base)
        group = pl.ds(base, SUBLANES)
        c_old = sc_ref[j, hh]
        cq = lax.dot_general(sq_ref[group, sl].astype(BF16), c_old.astype(BF16), NT_DIMS, preferred_element_type=F32)
        scq_ref[group, sl] = jnp.where(is_row, cq, scq_ref[group, sl])
        k_row = jnp.where(is_row, sk_ref[group, sl], 0.0).astype(BF16)
        outer = lax.dot_general(svs_ref[group, sl].astype(BF16), k_row, TN_DIMS, preferred_element_type=F32)
        scnew_ref[j, hh] = sg_ref[pl.ds(r, 1), sl] * c_old + outer

    row = lax.broadcasted_iota(jnp.int32, (CHUNK, CHUNK), 0)
    col = lax.broadcasted_iota(jnp.int32, (CHUNK, CHUNK), 1)
    causal = row >= col
    tril = causal.astype(F32)
    tril_b = causal.astype(BF16)

    class Seq:
        pass

    seqs = [Seq() for _ in range(n_seq)]

    def proj(sq, i):
        return _dot(sq.hb, w_in_ref[:, i * W_A:(i + 1) * W_A])

    def stage_norm(sq, bi):
        sq.hb = _rmsnorm(x_ref[bi], norm_g_ref[...]).astype(BF16)

    def stage_u(sq, bi):
        sq.u = _gelu(proj(sq, 0))

    def stage_v(sq, bi):
        sq.v = _gelu(proj(sq, 1))

    def stage_za(sq, bi):
        sq.z_a = proj(sq, 2)

    def gmlp_heads(sq, bi, heads):
        for hh in heads:
            sl = slice(hh * DH_A, (hh + 1) * DH_A)
            vn = _layernorm(sq.v[:, sl], lnv_g_ref[:, sl], lnv_b_ref[:, sl]).astype(BF16)
            wm = (w_sp_ref[hh] * tril).astype(BF16)
            vn_wide = jnp.concatenate([vn[c * CHUNK:(c + 1) * CHUNK] for c in range(n_chunks)], axis=1)
            sp_wide = _dot(wm, vn_wide)
            for c in range(n_chunks):
                rows = slice(c * CHUNK, (c + 1) * CHUNK)
                sp = sp_wide[:, c * DH_A:(c + 1) * DH_A] + b_sp_ref[:, sl]
                ycat_ref[bi, rows, sl] = (sq.u[rows, sl] * sp * _silu(sq.z_a[rows, sl])).astype(BF16)

    def stage_conv(sq, bi):
        x_b = proj(sq, 3)
        xpad_ref[bi, SUBLANES - 3:SUBLANES, :] = conv_ref[bi]
        xpad_ref[bi, SUBLANES:SUBLANES + tt, :] = x_b
        conv_ref[bi] = x_b[tt - 3:tt, :]
        xc = xpad_ref[bi, SUBLANES - 3:SUBLANES - 3 + tt, :] * conv_w_ref[0:1, :]
        xc = xc + xpad_ref[bi, SUBLANES - 2:SUBLANES - 2 + tt, :] * conv_w_ref[1:2, :]
        xc = xc + xpad_ref[bi, SUBLANES - 1:SUBLANES - 1 + tt, :] * conv_w_ref[2:3, :]
        xc = xc + x_b * conv_w_ref[3:4, :]
        sq.xc = _silu(xc + conv_b_ref[...])
        sq.xcb = sq.xc.astype(BF16)
        sq.xbb = x_b.astype(BF16)

    def stage_gate_o(sq, bi):
        sq.o_sig = jax.nn.sigmoid(proj(sq, 4))

    def stage_gate_z(sq, bi):
        sq.z_silu = _silu(proj(sq, 5))

    def stage_qkv(sq, bi):
        for hh in range(H_B):
            sl = slice(hh * DH_B, (hh + 1) * DH_B)
            qkv_ref[bi, :, hh * DH_B:(hh + 1) * DH_B] = _dot(sq.xcb[:, sl], w_q_ref[hh]).astype(BF16)
            qkv_ref[bi, :, W_B + hh * DH_B:W_B + (hh + 1) * DH_B] = _dot(sq.xcb[:, sl], w_k_ref[hh]).astype(BF16)
            qkv_ref[bi, :, 2 * W_B + hh * DH_B:2 * W_B + (hh + 1) * DH_B] = _dot(sq.xbb[:, sl], w_v_ref[hh]).astype(BF16)
        gates = _dot(qkv_ref[bi], w_if_ref[...]) + b_if_ref[...]
        logf = jax.nn.log_sigmoid(gates)
        sq.decay = []
        for c in range(n_chunks):
            rows = slice(c * CHUNK, (c + 1) * CHUNK)
            bcum = pltpu.roll(_cumsum_rows(tril_b, logf[rows]), LANES - H_B, 1)
            a_all = gates[rows] - bcum
            sq.decay.append((bcum, a_all, a_all.T))

    def mlstm_unit(sq, bi, c, hh):
        rows = slice(c * CHUNK, (c + 1) * CHUNK)
        sl = slice(hh * DH_B, (hh + 1) * DH_B)
        bcum, a_all, a_t = sq.decay[c]
        qh = qkv_ref[bi, rows, hh * DH_B:(hh + 1) * DH_B]
        kh = qkv_ref[bi, rows, W_B + hh * DH_B:W_B + (hh + 1) * DH_B]
        vh = qkv_ref[bi, rows, 2 * W_B + hh * DH_B:2 * W_B + (hh + 1) * DH_B]
        m_prev = m_ref[bi, hh, 0:1, 0:1]
        a_mat = jnp.where(causal, a_t[hh:hh + 1, :], -jnp.inf)
        m_run = jnp.maximum(jnp.max(a_mat, axis=-1, keepdims=True), m_prev)
        s = jnp.exp(a_mat - m_run) * lax.dot_general(qh, kh, NT_DIMS, preferred_element_type=F32)
        g = jnp.exp(m_prev - m_run)
        c_old = c_ref[bi, hh]
        inter = lax.dot_general(qh, c_old.astype(BF16), NT_DIMS, preferred_element_type=F32)
        num = _dot(s.astype(BF16), vh) + g * inter
        n_old = n_ref[bi, hh:hh + 1, :]
        den = jnp.sum(s, axis=-1, keepdims=True) + g * jnp.sum(qh.astype(F32) * n_old, axis=-1, keepdims=True)
        m_t = bcum[:, hh:hh + 1] + m_run
        hcell = num / jnp.maximum(jnp.abs(den), jnp.exp(-m_t))
        m_last = m_run[CHUNK - 1:CHUNK, :]
        w_end = jnp.exp(a_all[:, hh:hh + 1] - m_last)
        g_end = jnp.exp(m_prev - m_last)
        kw = kh.astype(F32) * w_end
        c_ref[bi, hh] = g_end * c_old + lax.dot_general(vh, kw.astype(BF16), TN_DIMS, preferred_element_type=F32)
        n_ref[bi, hh:hh + 1, :] = g_end * n_old + jnp.sum(kw, axis=0, keepdims=True)
        m_ref[bi, hh] = jnp.broadcast_to(m_t[CHUNK - 1:CHUNK, :], (SUBLANES, LANES))
        hn = _layernorm(hcell * sq.o_sig[rows, sl], gn_g_ref[:, sl])
        yb = (hn + skip_ref[:, sl] * sq.xc[rows, sl]) * sq.z_silu[rows, sl]
        ycat_ref[bi, rows, W_A + hh * DH_B:W_A + (hh + 1) * DH_B] = yb.astype(BF16)

    def stage_ple(sq, bi):
        sq.e = _rmsnorm(_dot(p_ref[bi].astype(BF16), w_pp_ref[...]), ple_g_ref[...])

    def stage_merge_a(sq, bi):
        sq.x1 = x_ref[bi] + _dot(ycat_ref[bi, :, :W_A], w_out_ref[:W_A, :])

    def stage_merge_b(sq, bi):
        sq.x1 = sq.x1 + _dot(ycat_ref[bi, :, W_A:], w_out_ref[W_A:, :])

    def stage_out(sq, bi):
        gate = jax.nn.sigmoid(_dot(sq.x1.astype(BF16), w_pg_ref[...]) + b_pg_ref[...])
        x2 = sq.x1 + gate * sq.e
        y_ref[bi] = _rmsnorm(x2, fin_g_ref[...])

    units = iter([(c, hh) for c in range(n_chunks) for hh in range(H_B)])
    pairs = iter([(j, hh) for j in range(rows_per_step) for hh in range(H_B)])
    stages = {
        "norm": stage_norm, "conv": stage_conv, "u": stage_u, "qkv": stage_qkv, "v": stage_v,
        "gate_o": stage_gate_o, "gate_z": stage_gate_z, "za": stage_za,
        "heads_lo": lambda sq, bi: gmlp_heads(sq, bi, list(range(0, H_A // 2))),
        "heads_hi": lambda sq, bi: gmlp_heads(sq, bi, list(range(H_A // 2, H_A))),
        "unit": lambda sq, bi: mlstm_unit(sq, bi, *next(units)),
        "pair": lambda sq, bi: sample_state_pair(*next(pairs)),
        "ple": stage_ple, "merge_a": stage_merge_a, "merge_b": stage_merge_b, "out": stage_out,
    }
    assert STEP_ORDER.count("unit") == n_chunks * H_B and STEP_ORDER.count("pair") == rows_per_step * H_B
    for name in STEP_ORDER:
        stages[name](seqs[0], 0)


def _resident(shape):
    nd = len(shape)
    return pl.BlockSpec(shape, lambda *_: (0,) * nd, pipeline_mode=pl.Buffered(1))


def _prompt_call(x, p, weights, sample_c, sample_rows):
    batch, seq, _ = x.shape
    tt = PROMPT_TILE
    ns = SEQS_PER_STEP
    grid = (batch // ns, seq // tt)
    n_steps = grid[0] * grid[1]
    n_rows = sample_c.shape[0]
    rows_per_step = n_rows // n_steps
    assert rows_per_step * n_steps == n_rows, "sample rows are spread evenly over the prompt grid steps"

    def state_block(b, t):
        return (b * grid[1] + t, 0, 0, 0)

    in_specs = [
        pl.BlockSpec((ns, tt, D_MODEL), lambda b, t: (b, t, 0)),
        pl.BlockSpec((ns, tt, D_PLE), lambda b, t: (b, t, 0)),
    ] + [_resident(w.shape) for w in weights] + [
        pl.BlockSpec((rows_per_step, H_B, DH_B, DH_B), state_block),
    ] + [_resident(a.shape) for a in sample_rows]
    out_shape = (
        jax.ShapeDtypeStruct((batch, seq, D_MODEL), F32),
        jax.ShapeDtypeStruct((batch, H_B, DH_B, DH_B), F32),
        jax.ShapeDtypeStruct((batch, H_B, DH_B), F32),
        jax.ShapeDtypeStruct((batch, H_B, SUBLANES, LANES), F32),
        jax.ShapeDtypeStruct((batch, CONV_W - 1, W_B), F32),
        jax.ShapeDtypeStruct(sample_c.shape, F32),
        jax.ShapeDtypeStruct((n_rows, W_B), F32),
    )
    out_specs = (
        pl.BlockSpec((ns, tt, D_MODEL), lambda b, t: (b, t, 0)),
        pl.BlockSpec((ns, H_B, DH_B, DH_B), lambda b, t: (b, 0, 0, 0)),
        pl.BlockSpec((ns, H_B, DH_B), lambda b, t: (b, 0, 0)),
        pl.BlockSpec((ns, H_B, SUBLANES, LANES), lambda b, t: (b, 0, 0, 0)),
        pl.BlockSpec((ns, CONV_W - 1, W_B), lambda b, t: (b, 0, 0)),
        pl.BlockSpec((rows_per_step, H_B, DH_B, DH_B), state_block),
        pl.BlockSpec((n_rows, W_B), lambda b, t: (0, 0)),
    )
    scratch = [
        pltpu.VMEM((ns, tt + SUBLANES, W_B), F32),
        pltpu.VMEM((ns, tt, 3 * W_B), BF16),
        pltpu.VMEM((ns, tt, W_A + W_B), BF16),
    ]
    return pl.pallas_call(
        _prompt_kernel,
        grid=grid,
        in_specs=in_specs,
        out_specs=out_specs,
        out_shape=out_shape,
        scratch_shapes=scratch,
        compiler_params=pltpu.CompilerParams(
            dimension_semantics=("arbitrary", "arbitrary"),
            vmem_limit_bytes=PROMPT_VMEM_BYTES),
        name="prompt_fused",
    )(x, p, *weights, sample_c, *sample_rows)


def _sample_pre_kernel(x_ref, conv_ref, n_ref, m_ref, vec_ref, w_in_ref,
                       w_q_ref, w_k_ref, w_v_ref, w_if_ref,
                       vn_ref, ya_ref, convnew_ref, q_ref, k_ref, vs_ref, gfull_ref, numa_ref,
                       denom_ref, osig_ref, skipxc_ref, zsilu_ref, nnew_ref, mnew_ref, w_in_bf16_ref,
                       qkv_ref, hb_ref, proj_ref):
    slab = pl.program_id(0)
    vp = _vector_params(vec_ref)

    @pl.when(slab == 0)
    def _():
        hb_ref[...] = _rmsnorm(x_ref[:, 0, :], vp["norm_g"][...]).astype(BF16)

    w_slab = w_in_ref[...].astype(BF16)
    w_in_bf16_ref[...] = w_slab
    proj_ref[slab] = _dot(hb_ref[...], w_slab)

    @pl.when(slab == pl.num_programs(0) - 1)
    def _():
        _sample_front_end(conv_ref, n_ref, m_ref, vp["lnv_g"], vp["lnv_b"], vp["w00"], vp["b0"], vp["conv_w"],
                          vp["conv_b"], w_q_ref, w_k_ref, w_v_ref, w_if_ref, vp["b_if"], vp["skip"],
                          vn_ref, ya_ref, convnew_ref, q_ref, k_ref, vs_ref, gfull_ref, numa_ref,
                          denom_ref, osig_ref, skipxc_ref, zsilu_ref, nnew_ref, mnew_ref, qkv_ref, proj_ref)


def _sample_front_end(conv_ref, n_ref, m_ref, lnv_g_ref, lnv_b_ref, w00_ref, b0_ref, conv_w_ref, conv_b_ref,
                      w_q_ref, w_k_ref, w_v_ref, w_if_ref, b_if_ref, skip_ref,
                      vn_ref, ya_ref, convnew_ref, q_ref, k_ref, vs_ref, gfull_ref, numa_ref,
                      denom_ref, osig_ref, skipxc_ref, zsilu_ref, nnew_ref, mnew_ref, qkv_ref, proj_ref):
    def proj(i):
        return proj_ref[i]

    u = _gelu(proj(0))
    v = _gelu(proj(1))
    z_a = proj(2)
    for hh in range(H_A):
        sl = slice(hh * DH_A, (hh + 1) * DH_A)
        vn = _layernorm(v[:, sl], lnv_g_ref[:, sl], lnv_b_ref[:, sl])
        vn_ref[:, 0, sl] = vn
        sp = vn * w00_ref[:, sl] + b0_ref[:, sl]
        ya_ref[:, sl] = (u[:, sl] * sp * _silu(z_a[:, sl])).astype(BF16)

    x_b = proj(3)
    osig_ref[...] = jax.nn.sigmoid(proj(4))
    zsilu_ref[...] = _silu(proj(5))
    c0 = conv_ref[:, 0, :]
    c1 = conv_ref[:, 1, :]
    c2 = conv_ref[:, 2, :]
    xc = c0 * conv_w_ref[0:1, :]
    xc = xc + c1 * conv_w_ref[1:2, :]
    xc = xc + c2 * conv_w_ref[2:3, :]
    xc = xc + x_b * conv_w_ref[3:4, :]
    xc = _silu(xc + conv_b_ref[...])
    convnew_ref[:, 0, :] = c1
    convnew_ref[:, 1, :] = c2
    convnew_ref[:, 2, :] = x_b
    skipxc_ref[...] = skip_ref[...] * xc
    xcb = xc.astype(BF16)
    xbb = x_b.astype(BF16)
    for hh in range(H_B):
        sl = slice(hh * DH_B, (hh + 1) * DH_B)
        qkv_ref[:, hh * DH_B:(hh + 1) * DH_B] = _dot(xcb[:, sl], w_q_ref[hh]).astype(BF16)
        qkv_ref[:, W_B + hh * DH_B:W_B + (hh + 1) * DH_B] = _dot(xcb[:, sl], w_k_ref[hh]).astype(BF16)
        qkv_ref[:, 2 * W_B + hh * DH_B:2 * W_B + (hh + 1) * DH_B] = _dot(xbb[:, sl], w_v_ref[hh]).astype(BF16)
    gates = _dot(qkv_ref[...], w_if_ref[...]) + b_if_ref[...]
    logf = pltpu.roll(jax.nn.log_sigmoid(gates), LANES - H_B, 1)
    li = gates[:, 0:H_B]
    inter = logf[:, 0:H_B] + m_ref[...]
    m_t = jnp.maximum(inter, li)
    s_w = jnp.exp(li - m_t)
    g = jnp.exp(inter - m_t)
    e_neg = jnp.exp(-m_t)
    mnew_ref[...] = m_t
    for hh in range(H_B):
        sl = slice(hh * DH_B, (hh + 1) * DH_B)
        qf = qkv_ref[:, hh * DH_B:(hh + 1) * DH_B].astype(F32)
        kf = qkv_ref[:, W_B + hh * DH_B:W_B + (hh + 1) * DH_B].astype(F32)
        vf = qkv_ref[:, 2 * W_B + hh * DH_B:2 * W_B + (hh + 1) * DH_B].astype(F32)
        s_h = s_w[:, hh:hh + 1]
        g_h = g[:, hh:hh + 1]
        n_old = n_ref[:, hh, :]
        s_qk = s_h * jnp.sum(qf * kf, axis=-1, keepdims=True)
        den = s_qk + g_h * jnp.sum(n_old * qf, axis=-1, keepdims=True)
        denom = jnp.maximum(jnp.abs(den), e_neg[:, hh:hh + 1])
        q_ref[:, sl] = qf
        k_ref[:, sl] = kf
        vs_ref[:, sl] = s_h * vf
        gfull_ref[:, sl] = jnp.broadcast_to(g_h, qf.shape)
        numa_ref[:, sl] = s_qk * vf
        denom_ref[:, sl] = jnp.broadcast_to(denom, qf.shape)
        nnew_ref[:, hh, :] = g_h * n_old + s_h * kf


def _sample_post_kernel(x_ref, p_ref, ya_ref, cq_ref, gfull_ref, numa_ref, denom_ref, osig_ref,
                        skipxc_ref, zsilu_ref, vec_ref, w_out_ref, w_pg_ref, w_pp_ref, y_ref, ycat_ref):
    vp = _vector_params(vec_ref)
    gn_g_ref, b_pg_ref, ple_g_ref, fin_g_ref = vp["gn_g"], vp["b_pg"], vp["ple_g"], vp["fin_g"]
    ycat_ref[:, 0:W_A] = ya_ref[...]
    for hh in range(H_B):
        sl = slice(hh * DH_B, (hh + 1) * DH_B)
        hcell = (numa_ref[:, sl] + gfull_ref[:, sl] * cq_ref[:, sl]) / denom_ref[:, sl]
        hn = _layernorm(hcell * osig_ref[:, sl], gn_g_ref[:, sl])
        ycat_ref[:, W_A + hh * DH_B:W_A + (hh + 1) * DH_B] = ((hn + skipxc_ref[:, sl]) * zsilu_ref[:, sl]).astype(BF16)
    x1 = x_ref[:, 0, :] + _dot(ycat_ref[...], w_out_ref[...])
    e = _rmsnorm(_dot(p_ref[:, 0, :].astype(BF16), w_pp_ref[...]), ple_g_ref[...])
    gate = jax.nn.sigmoid(_dot(x1.astype(BF16), w_pg_ref[...]) + b_pg_ref[...])
    x2 = x1 + gate * e
    y_ref[:, 0, :] = _rmsnorm(x2, fin_g_ref[...])


def _sample_pre_call(before_w, w_in_f32, after_w, out_shape):
    nb = before_w[0].shape[0]
    n_slabs = N_IN // W_A
    whole = lambda s: pl.BlockSpec(s.shape, lambda i, nd=len(s.shape): (0,) * nd)
    slab_spec = pl.BlockSpec((D_MODEL, W_A), lambda i: (0, i))
    out_shape = tuple(out_shape) + (jax.ShapeDtypeStruct(w_in_f32.shape, BF16),)
    return pl.pallas_call(
        _sample_pre_kernel,
        grid=(n_slabs,),
        in_specs=[_resident(a.shape) for a in before_w] + [slab_spec] + [_resident(a.shape) for a in after_w],
        out_specs=tuple(whole(s) for s in out_shape[:-1]) + (slab_spec,),
        out_shape=out_shape,
        scratch_shapes=[pltpu.VMEM((nb, 3 * W_B), BF16), pltpu.VMEM((nb, D_MODEL), BF16),
                        pltpu.VMEM((n_slabs, nb, W_A), F32)],
        compiler_params=pltpu.CompilerParams(
            dimension_semantics=("arbitrary",), vmem_limit_bytes=SAMPLE_VMEM_BYTES),
        name="sample_pre",
    )(*before_w, w_in_f32, *after_w)


def _single_step_call(body, inputs, out_shape, scratch, name):
    return pl.pallas_call(
        body,
        grid=(1,),
        in_specs=[_resident(a.shape) for a in inputs],
        out_specs=tuple(pl.BlockSpec(s.shape, lambda *_, nd=len(s.shape): (0,) * nd) for s in out_shape),
        out_shape=out_shape,
        scratch_shapes=scratch,
        compiler_params=pltpu.CompilerParams(
            dimension_semantics=("arbitrary",), vmem_limit_bytes=SAMPLE_VMEM_BYTES),
        name=name,
    )(*inputs)


def kernel(x_prompt, x_sample, state_mlstm_C, state_mlstm_n, state_mlstm_m, state_conv, p_prompt, p_sample,
           norm_in_g, w_in, ln_v_g, ln_v_b, w_spatial, b_spatial, conv_w, conv_b, w_q, w_k, w_v, w_if, b_if,
           gn_g, skip, w_out, w_ple_gate, b_ple_gate, w_ple_proj, ple_norm_g, final_norm_g):
    assert norm_in_g.shape[0] == 1, "single-layer trunk"
    nb = x_sample.shape[0]

    w_sp = w_spatial[0]
    b_sp_full = jnp.repeat(b_spatial[0].T, DH_A, axis=1)
    wq_b = w_q[0].astype(BF16)
    wk_b = (w_k[0] * (DH_B ** -0.5)).astype(BF16)
    wv_b = w_v[0].astype(BF16)
    k_fix = jnp.concatenate([jnp.ones((W_B, 1), F32), jnp.full((W_B, 1), DH_B ** 0.5, F32), jnp.ones((W_B, 1), F32)], 0)
    w_if_p = jnp.pad(w_if[0] * k_fix, ((0, 0), (0, LANES - 2 * H_B))).astype(BF16)
    w_out_b = w_out[0].astype(BF16)
    w_pg_b = w_ple_gate[0].astype(BF16)
    w_pp_b = w_ple_proj[0].astype(BF16)
    w00 = jnp.repeat(w_sp[:, 0, 0], DH_A)
    b0 = jnp.repeat(b_spatial[0][:, 0], DH_A)
    vec = _pack_vectors(norm_in_g[0], ln_v_g[0], ln_v_b[0], conv_b[0], gn_g[0], skip[0], b_ple_gate[0],
                        ple_norm_g[0], final_norm_g, conv_w[0], b_if[0], w00, b0)

    full = jax.ShapeDtypeStruct((nb, W_B), F32)
    pre_out = (
        jax.ShapeDtypeStruct((nb, 1, W_A), F32),
        jax.ShapeDtypeStruct((nb, W_A), BF16),
        jax.ShapeDtypeStruct((nb, CONV_W - 1, W_B), F32),
        full, full, full, full, full, full, full, full, full,
        jax.ShapeDtypeStruct((nb, H_B, DH_B), F32),
        jax.ShapeDtypeStruct((nb, H_B), F32),
    )
    (vn_s, ya_s, conv_new, q_s, k_s, vs_s, gfull, numa, denom, osig, skipxc, zsilu, n_new, m_new,
     w_in_b) = _sample_pre_call(
        [x_sample, state_conv[0], state_mlstm_n[0], state_mlstm_m[0], vec], w_in[0],
        [wq_b, wk_b, wv_b, w_if_p], pre_out)

    prompt_weights = [vec, w_in_b, w_sp, b_sp_full, wq_b, wk_b, wv_b, w_if_p, w_out_b, w_pg_b, w_pp_b]
    y_p, c_p, n_p, m_p, conv_p, c_new, cq = _prompt_call(
        x_prompt, p_prompt[0], prompt_weights, state_mlstm_C[0], [q_s, k_s, vs_s, gfull])

    (y_s,) = _single_step_call(
        _sample_post_kernel,
        [x_sample, p_sample[0], ya_s, cq, gfull, numa, denom, osig, skipxc, zsilu, vec, w_out_b, w_pg_b, w_pp_b],
        (jax.ShapeDtypeStruct((nb, 1, D_MODEL), F32),), [pltpu.VMEM((nb, W_A + W_B), BF16)], "sample_post")

    return (y_p, y_s, c_p[None], n_p[None], m_p[:, :, 0, 0][None], conv_p[None],
            c_new[None], n_new[None], m_new[None], conv_new[None], vn_s[None])
```

```python
import jax
import jax.numpy as jnp
from jax import lax
from jax.experimental import pallas as pl
from jax.experimental.pallas import tpu as pltpu

D_MODEL = 1024
W_A = 1024
H_A = 8
DH_A = 128
W_B = 1024
H_B = 4
DH_B = 256
CHUNK = 128
CONV_W = 4
D_PLE = 256
N_IN = 3 * W_A + 3 * W_B
EPS = 1e-6

LANES = 128
SUBLANES = 8
PROMPT_TILE = 256
SEQS_PER_STEP = 1
STEP_ORDER = (
    "ple", "norm", "conv", "u", "v", "za", "qkv", "heads_lo", "gate_o", "heads_hi", "gate_z",
    "unit", "unit", "unit", "unit", "unit", "merge_a", "unit", "unit", "unit",
    "merge_b", "pair", "pair", "pair", "out", "pair", "pair", "pair", "pair", "pair",
)
PROMPT_VMEM_BYTES = 56 * 1024 * 1024
SAMPLE_VMEM_BYTES = 58 * 1024 * 1024

F32 = jnp.float32
BF16 = jnp.bfloat16
NT_DIMS = (((1,), (1,)), ((), ()))
TN_DIMS = (((0,), (0,)), ((), ()))


def _rmsnorm(x, g):
    return x * lax.rsqrt(jnp.mean(x * x, axis=-1, keepdims=True) + EPS) * g


def _layernorm(x, g, b=None):
    mu = jnp.mean(x, axis=-1, keepdims=True)
    d = x - mu
    var = jnp.mean(d * d, axis=-1, keepdims=True)
    y = d * lax.rsqrt(var + EPS) * g
    return y if b is None else y + b


def _gelu(x):
    return 0.5 * x * (1.0 + lax.erf(x * (0.5 ** 0.5)))


def _silu(x):
    return x * jax.nn.sigmoid(x)


def _dot(a, b):
    return jnp.dot(a, b, preferred_element_type=F32)


def _cumsum_rows(tril_b, x):
    x1 = x.astype(BF16)
    r1 = x - x1.astype(F32)
    x2 = r1.astype(BF16)
    x3 = (r1 - x2.astype(F32)).astype(BF16)
    return _dot(tril_b, x1) + _dot(tril_b, x2) + _dot(tril_b, x3)


def _vector_params(vec_ref):
    row = lambda i: vec_ref.at[i:i + 1]
    return dict(norm_g=row(0), lnv_g=row(1), lnv_b=row(2), conv_b=row(3), gn_g=row(4), skip=row(5), b_pg=row(6),
                ple_g=row(7), fin_g=row(8), conv_w=vec_ref.at[9:9 + CONV_W], b_if=vec_ref.at[13:14, 0:LANES],
                w00=row(14), b0=row(15))


def _pack_vectors(norm_g, lnv_g, lnv_b, conv_b, gn_g, skip, b_pg, ple_g, fin_g, conv_w, b_if, w00, b0):
    wide = lambda a: jnp.pad(a.reshape(1, -1).astype(F32), ((0, 0), (0, D_MODEL - a.size)))
    rows = [wide(a) for a in (norm_g, lnv_g, lnv_b, conv_b, gn_g, skip, b_pg, ple_g, fin_g)]
    return jnp.concatenate(rows + [conv_w.astype(F32), wide(b_if), wide(w00), wide(b0)], axis=0)


def _prompt_kernel(x_ref, p_ref, vec_ref, w_in_ref, w_sp_ref, b_sp_ref, w_q_ref, w_k_ref, w_v_ref, w_if_ref,
                   w_out_ref, w_pg_ref, w_pp_ref,
                   sc_ref, sq_ref, sk_ref, svs_ref, sg_ref,
                   y_ref, c_ref, n_ref, m_ref, conv_ref, scnew_ref, scq_ref,
                   xpad_ref, qkv_ref, ycat_ref):
    vp = _vector_params(vec_ref)
    norm_g_ref, lnv_g_ref, lnv_b_ref, conv_w_ref, conv_b_ref = (
        vp["norm_g"], vp["lnv_g"], vp["lnv_b"], vp["conv_w"], vp["conv_b"])
    b_if_ref, gn_g_ref, skip_ref, b_pg_ref, ple_g_ref, fin_g_ref = (
        vp["b_if"], vp["gn_g"], vp["skip"], vp["b_pg"], vp["ple_g"], vp["fin_g"])
    n_seq, tt = x_ref.shape[0], x_ref.shape[1]
    n_chunks = tt // CHUNK
    step = pl.program_id(0) * pl.num_programs(1) + pl.program_id(1)

    @pl.when(pl.program_id(1) == 0)
    def _():
        c_ref[...] = jnp.zeros_like(c_ref)
        n_ref[...] = jnp.zeros_like(n_ref)
        m_ref[...] = jnp.zeros_like(m_ref)
        conv_ref[...] = jnp.zeros_like(conv_ref)

    @pl.when(step == 0)
    def _():
        scq_ref[...] = jnp.zeros_like(scq_ref)

    rows_per_step = sc_ref.shape[0]
    group_row = lax.broadcasted_iota(jnp.int32, (SUBLANES, DH_B), 0)

    def sample_state_pair(j, hh):
        sl = slice(hh * DH_B, (hh + 1) * DH_B)
        r = step * rows_per_step + j
        base = pl.multiple_of((r // SUBLANES) * SUBLANES, SUBLANES)
        is_row = group_row == (r - base)
        group = pl.ds(base, SUBLANES)
        c_old = sc_ref[j, hh]
        cq = lax.dot_general(sq_ref[group, sl].astype(BF16), c_old.astype(BF16), NT_DIMS, preferred_element_type=F32)
        scq_ref[group, sl] = jnp.where(is_row, cq, scq_ref[group, sl])
        k_row = jnp.where(is_row, sk_ref[group, sl], 0.0).astype(BF16)
        outer = lax.dot_general(svs_ref[group, sl].astype(BF16), k_row, TN_DIMS, preferred_element_type=F32)
        scnew_ref[j, hh] = sg_ref[pl.ds(r, 1), sl] * c_old + outer

    row = lax.broadcasted_iota(jnp.int32, (CHUNK, CHUNK), 0)
    col = lax.broadcasted_iota(jnp.int32, (CHUNK, CHUNK), 1)
    causal = row >= col
    tril = causal.astype(F32)
    tril_b = causal.astype(BF16)

    class Seq:
        pass

    seqs = [Seq() for _ in range(n_seq)]

    def proj(sq, i):
        return _dot(sq.hb, w_in_ref[:, i * W_A:(i + 1) * W_A])

    def stage_norm(sq, bi):
        sq.hb = _rmsnorm(x_ref[bi], norm_g_ref[...]).astype(BF16)

    def stage_u(sq, bi):
        sq.u = _gelu(proj(sq, 0))

    def stage_v(sq, bi):
        sq.v = _gelu(proj(sq, 1))

    def stage_za(sq, bi):
        sq.z_a = proj(sq, 2)

    def gmlp_heads(sq, bi, heads):
        for hh in heads:
            sl = slice(hh * DH_A, (hh + 1) * DH_A)
            vn = _layernorm(sq.v[:, sl], lnv_g_ref[:, sl], lnv_b_ref[:, sl]).astype(BF16)
            wm = (w_sp_ref[hh] * tril).astype(BF16)
            vn_wide = jnp.concatenate([vn[c * CHUNK:(c + 1) * CHUNK] for c in range(n_chunks)], axis=1)
            sp_wide = _dot(wm, vn_wide)
            for c in range(n_chunks):
                rows = slice(c * CHUNK, (c + 1) * CHUNK)
                sp = sp_wide[:, c * DH_A:(c + 1) * DH_A] + b_sp_ref[:, sl]
                ycat_ref[bi, rows, sl] = (sq.u[rows, sl] * sp * _silu(sq.z_a[rows, sl])).astype(BF16)

    def stage_conv(sq, bi):
        x_b = proj(sq, 3)
        xpad_ref[bi, SUBLANES - 3:SUBLANES, :] = conv_ref[bi]
        xpad_ref[bi, SUBLANES:SUBLANES + tt, :] = x_b
        conv_ref[bi] = x_b[tt - 3:tt, :]
        xc = xpad_ref[bi, SUBLANES - 3:SUBLANES - 3 + tt, :] * conv_w_ref[0:1, :]
        xc = xc + xpad_ref[bi, SUBLANES - 2:SUBLANES - 2 + tt, :] * conv_w_ref[1:2, :]
        xc = xc + xpad_ref[bi, SUBLANES - 1:SUBLANES - 1 + tt, :] * conv_w_ref[2:3, :]
        xc = xc + x_b * conv_w_ref[3:4, :]
        sq.xc = _silu(xc + conv_b_ref[...])
        sq.xcb = sq.xc.astype(BF16)
        sq.xbb = x_b.astype(BF16)

    def stage_gate_o(sq, bi):
        sq.o_sig = jax.nn.sigmoid(proj(sq, 4))

    def stage_gate_z(sq, bi):
        sq.z_silu = _silu(proj(sq, 5))

    def stage_qkv(sq, bi):
        for hh in range(H_B):
            sl = slice(hh * DH_B, (hh + 1) * DH_B)
            qkv_ref[bi, :, hh * DH_B:(hh + 1) * DH_B] = _dot(sq.xcb[:, sl], w_q_ref[hh]).astype(BF16)
            qkv_ref[bi, :, W_B + hh * DH_B:W_B + (hh + 1) * DH_B] = _dot(sq.xcb[:, sl], w_k_ref[hh]).astype(BF16)
            qkv_ref[bi, :, 2 * W_B + hh * DH_B:2 * W_B + (hh + 1) * DH_B] = _dot(sq.xbb[:, sl], w_v_ref[hh]).astype(BF16)
        gates = _dot(qkv_ref[bi], w_if_ref[...]) + b_if_ref[...]
        logf = jax.nn.log_sigmoid(gates)
        sq.decay = []
        for c in range(n_chunks):
            rows = slice(c * CHUNK, (c + 1) * CHUNK)
            bcum = pltpu.roll(_cumsum_rows(tril_b, logf[rows]), LANES - H_B, 1)
            a_all = gates[rows] - bcum
            sq.decay.append((bcum, a_all, a_all.T))

    def mlstm_unit(sq, bi, c, hh):
        rows = slice(c * CHUNK, (c + 1) * CHUNK)
        sl = slice(hh * DH_B, (hh + 1) * DH_B)
        bcum, a_all, a_t = sq.decay[c]
        qh = qkv_ref[bi, rows, hh * DH_B:(hh + 1) * DH_B]
        kh = qkv_ref[bi, rows, W_B + hh * DH_B:W_B + (hh + 1) * DH_B]
        vh = qkv_ref[bi, rows, 2 * W_B + hh * DH_B:2 * W_B + (hh + 1) * DH_B]
        m_prev = m_ref[bi, hh, 0:1, 0:1]
        a_mat = jnp.where(causal, a_t[hh:hh + 1, :], -jnp.inf)
        m_run = jnp.maximum(jnp.max(a_mat, axis=-1, keepdims=True), m_prev)
        s = jnp.exp(a_mat - m_run) * lax.dot_general(qh, kh, NT_DIMS, preferred_element_type=F32)
        g = jnp.exp(m_prev - m_run)
        c_old = c_ref[bi, hh]
        inter = lax.dot_general(qh, c_old.astype(BF16), NT_DIMS, preferred_element_type=F32)
        num = _dot(s.astype(BF16), vh) + g * inter
        n_old = n_ref[bi, hh:hh + 1, :]
        den = jnp.sum(s, axis=-1, keepdims=True) + g * jnp.sum(qh.astype(F32) * n_old, axis=-1, keepdims=True)
        m_t = bcum[:, hh:hh + 1] + m_run
        hcell = num / jnp.maximum(jnp.abs(den), jnp.exp(-m_t))
        m_last = m_run[CHUNK - 1:CHUNK, :]
        w_end = jnp.exp(a_all[:, hh:hh + 1] - m_last)
        g_end = jnp.exp(m_prev - m_last)
        kw = kh.astype(F32) * w_end
        c_ref[bi, hh] = g_end * c_old + lax.dot_general(vh, kw.astype(BF16), TN_DIMS, preferred_element_type=F32)
        n_ref[bi, hh:hh + 1, :] = g_end * n_old + jnp.sum(kw, axis=0, keepdims=True)
        m_ref[bi, hh] = jnp.broadcast_to(m_t[CHUNK - 1:CHUNK, :], (SUBLANES, LANES))
        hn = _layernorm(hcell * sq.o_sig[rows, sl], gn_g_ref[:, sl])
        yb = (hn + skip_ref[:, sl] * sq.xc[rows, sl]) * sq.z_silu[rows, sl]
        ycat_ref[bi, rows, W_A + hh * DH_B:W_A + (hh + 1) * DH_B] = yb.astype(BF16)

    def stage_ple(sq, bi):
        sq.e = _rmsnorm(_dot(p_ref[bi].astype(BF16), w_pp_ref[...]), ple_g_ref[...])

    def stage_merge_a(sq, bi):
        sq.x1 = x_ref[bi] + _dot(ycat_ref[bi, :, :W_A], w_out_ref[:W_A, :])

    def stage_merge_b(sq, bi):
        sq.x1 = sq.x1 + _dot(ycat_ref[bi, :, W_A:], w_out_ref[W_A:, :])

    def stage_out(sq, bi):
        gate = jax.nn.sigmoid(_dot(sq.x1.astype(BF16), w_pg_ref[...]) + b_pg_ref[...])
        x2 = sq.x1 + gate * sq.e
        y_ref[bi] = _rmsnorm(x2, fin_g_ref[...])

    units = iter([(c, hh) for c in range(n_chunks) for hh in range(H_B)])
    pairs = iter([(j, hh) for j in range(rows_per_step) for hh in range(H_B)])
    stages = {
        "norm": stage_norm, "conv": stage_conv, "u": stage_u, "qkv": stage_qkv, "v": stage_v,
        "gate_o": stage_gate_o, "gate_z": stage_gate_z, "za": stage_za,
        "heads_lo": lambda sq, bi: gmlp_heads(sq, bi, list(range(0, H_A // 2))),
        "heads_hi": lambda sq, bi: gmlp_heads(sq, bi, list(range(H_A // 2, H_A))),
        "unit": lambda sq, bi: mlstm_unit(sq, bi, *next(units)),
        "pair": lambda sq, bi: sample_state_pair(*next(pairs)),
        "ple": stage_ple, "merge_a": stage_merge_a, "merge_b": stage_merge_b, "out": stage_out,
    }
    assert STEP_ORDER.count("unit") == n_chunks * H_B and STEP_ORDER.count("pair") == rows_per_step * H_B
    for name in STEP_ORDER:
        stages[name](seqs[0], 0)


def _resident(shape):
    nd = len(shape)
    return pl.BlockSpec(shape, lambda *_: (0,) * nd, pipeline_mode=pl.Buffered(1))


def _prompt_call(x, p, weights, sample_c, sample_rows):
    batch, seq, _ = x.shape
    tt = PROMPT_TILE
    ns = SEQS_PER_STEP
    grid = (batch // ns, seq // tt)
    n_steps = grid[0] * grid[1]
    n_rows = sample_c.shape[0]
    rows_per_step = n_rows // n_steps
    assert rows_per_step * n_steps == n_rows, "sample rows are spread evenly over the prompt grid steps"

    def state_block(b, t):
        return (b * grid[1] + t, 0, 0, 0)

    in_specs = [
        pl.BlockSpec((ns, tt, D_MODEL), lambda b, t: (b, t, 0)),
        pl.BlockSpec((ns, tt, D_PLE), lambda b, t: (b, t, 0)),
    ] + [_resident(w.shape) for w in weights] + [
        pl.BlockSpec((rows_per_step, H_B, DH_B, DH_B), state_block),
    ] + [_resident(a.shape) for a in sample_rows]
    out_shape = (
        jax.ShapeDtypeStruct((batch, seq, D_MODEL), F32),
        jax.ShapeDtypeStruct((batch, H_B, DH_B, DH_B), F32),
        jax.ShapeDtypeStruct((batch, H_B, DH_B), F32),
        jax.ShapeDtypeStruct((batch, H_B, SUBLANES, LANES), F32),
        jax.ShapeDtypeStruct((batch, CONV_W - 1, W_B), F32),
        jax.ShapeDtypeStruct(sample_c.shape, F32),
        jax.ShapeDtypeStruct((n_rows, W_B), F32),
    )
    out_specs = (
        pl.BlockSpec((ns, tt, D_MODEL), lambda b, t: (b, t, 0)),
        pl.BlockSpec((ns, H_B, DH_B, DH_B), lambda b, t: (b, 0, 0, 0)),
        pl.BlockSpec((ns, H_B, DH_B), lambda b, t: (b, 0, 0)),
        pl.BlockSpec((ns, H_B, SUBLANES, LANES), lambda b, t: (b, 0, 0, 0)),
        pl.BlockSpec((ns, CONV_W - 1, W_B), lambda b, t: (b, 0, 0)),
        pl.BlockSpec((rows_per_step, H_B, DH_B, DH_B), state_block),
        pl.BlockSpec((n_rows, W_B), lambda b, t: (0, 0)),
    )
    scratch = [
        pltpu.VMEM((ns, tt + SUBLANES, W_B), F32),
        pltpu.VMEM((ns, tt, 3 * W_B), BF16),
        pltpu.VMEM((ns, tt, W_A + W_B), BF16),
    ]
    return pl.pallas_call(
        _prompt_kernel,
        grid=grid,
        in_specs=in_specs,
        out_specs=out_specs,
        out_shape=out_shape,
        scratch_shapes=scratch,
        compiler_params=pltpu.CompilerParams(
            dimension_semantics=("arbitrary", "arbitrary"),
            vmem_limit_bytes=PROMPT_VMEM_BYTES),
        name="prompt_fused",
    )(x, p, *weights, sample_c, *sample_rows)


def _sample_pre_kernel(x_ref, conv_ref, n_ref, m_ref, vec_ref, w_in_ref,
                       wq_f32_ref, wk_f32_ref, wv_f32_ref, w_if_ref, w_out_f32_ref, w_pg_f32_ref, w_pp_f32_ref,
                       vn_ref, ya_ref, convnew_ref, q_ref, k_ref, vs_ref, gfull_ref, numa_ref,
                       denom_ref, osig_ref, skipxc_ref, zsilu_ref, nnew_ref, mnew_ref, w_in_bf16_ref,
                       w_q_ref, w_k_ref, w_v_ref, w_out_ref, w_pg_ref, w_pp_ref,
                       qkv_ref, hb_ref, proj_ref):
    slab = pl.program_id(0)
    vp = _vector_params(vec_ref)

    @pl.when(slab == 0)
    def _():
        hb_ref[...] = _rmsnorm(x_ref[:, 0, :], vp["norm_g"][...]).astype(BF16)

    w_slab = w_in_ref[...].astype(BF16)
    w_in_bf16_ref[...] = w_slab
    proj_ref[slab] = _dot(hb_ref[...], w_slab)

    @pl.when(slab < 2)
    def _():
        w_out_ref[...] = w_out_f32_ref[...].astype(BF16)

    @pl.when(slab == 2)
    def _():
        w_pg_ref[...] = w_pg_f32_ref[...].astype(BF16)

    @pl.when(slab == 3)
    def _():
        w_pp_ref[...] = w_pp_f32_ref[...].astype(BF16)
        w_q_ref[...] = wq_f32_ref[...].astype(BF16)

    @pl.when(slab == 4)
    def _():
        w_k_ref[...] = (wk_f32_ref[...] * (DH_B ** -0.5)).astype(BF16)
        w_v_ref[...] = wv_f32_ref[...].astype(BF16)

    @pl.when(slab == pl.num_programs(0) - 1)
    def _():
        _sample_front_end(conv_ref, n_ref, m_ref, vp["lnv_g"], vp["lnv_b"], vp["w00"], vp["b0"], vp["conv_w"],
                          vp["conv_b"], w_q_ref, w_k_ref, w_v_ref, w_if_ref, vp["b_if"], vp["skip"],
                          vn_ref, ya_ref, convnew_ref, q_ref, k_ref, vs_ref, gfull_ref, numa_ref,
                          denom_ref, osig_ref, skipxc_ref, zsilu_ref, nnew_ref, mnew_ref, qkv_ref, proj_ref)


def _sample_front_end(conv_ref, n_ref, m_ref, lnv_g_ref, lnv_b_ref, w00_ref, b0_ref, conv_w_ref, conv_b_ref,
                      w_q_ref, w_k_ref, w_v_ref, w_if_ref, b_if_ref, skip_ref,
                      vn_ref, ya_ref, convnew_ref, q_ref, k_ref, vs_ref, gfull_ref, numa_ref,
                      denom_ref, osig_ref, skipxc_ref, zsilu_ref, nnew_ref, mnew_ref, qkv_ref, proj_ref):
    def proj(i):
        return proj_ref[i]

    u = _gelu(proj(0))
    v = _gelu(proj(1))
    z_a = proj(2)
    for hh in range(H_A):
        sl = slice(hh * DH_A, (hh + 1) * DH_A)
        vn = _layernorm(v[:, sl], lnv_g_ref[:, sl], lnv_b_ref[:, sl])
        vn_ref[:, 0, sl] = vn
        sp = vn * w00_ref[:, sl] + b0_ref[:, sl]
        ya_ref[:, sl] = (u[:, sl] * sp * _silu(z_a[:, sl])).astype(BF16)

    x_b = proj(3)
    osig_ref[...] = jax.nn.sigmoid(proj(4))
    zsilu_ref[...] = _silu(proj(5))
    c0 = conv_ref[:, 0, :]
    c1 = conv_ref[:, 1, :]
    c2 = conv_ref[:, 2, :]
    xc = c0 * conv_w_ref[0:1, :]
    xc = xc + c1 * conv_w_ref[1:2, :]
    xc = xc + c2 * conv_w_ref[2:3, :]
    xc = xc + x_b * conv_w_ref[3:4, :]
    xc = _silu(xc + conv_b_ref[...])
    convnew_ref[:, 0, :] = c1
    convnew_ref[:, 1, :] = c2
    convnew_ref[:, 2, :] = x_b
    skipxc_ref[...] = skip_ref[...] * xc
    xcb = xc.astype(BF16)
    xbb = x_b.astype(BF16)
    for hh in range(H_B):
        sl = slice(hh * DH_B, (hh + 1) * DH_B)
        qkv_ref[:, hh * DH_B:(hh + 1) * DH_B] = _dot(xcb[:, sl], w_q_ref[hh]).astype(BF16)
        qkv_ref[:, W_B + hh * DH_B:W_B + (hh + 1) * DH_B] = _dot(xcb[:, sl], w_k_ref[hh]).astype(BF16)
        qkv_ref[:, 2 * W_B + hh * DH_B:2 * W_B + (hh + 1) * DH_B] = _dot(xbb[:, sl], w_v_ref[hh]).astype(BF16)
    gates = _dot(qkv_ref[...], w_if_ref[...]) + b_if_ref[...]
    logf = pltpu.roll(jax.nn.log_sigmoid(gates), LANES - H_B, 1)
    li = gates[:, 0:H_B]
    inter = logf[:, 0:H_B] + m_ref[...]
    m_t = jnp.maximum(inter, li)
    s_w = jnp.exp(li - m_t)
    g = jnp.exp(inter - m_t)
    e_neg = jnp.exp(-m_t)
    mnew_ref[...] = m_t
    for hh in range(H_B):
        sl = slice(hh * DH_B, (hh + 1) * DH_B)
        qf = qkv_ref[:, hh * DH_B:(hh + 1) * DH_B].astype(F32)
        kf = qkv_ref[:, W_B + hh * DH_B:W_B + (hh + 1) * DH_B].astype(F32)
        vf = qkv_ref[:, 2 * W_B + hh * DH_B:2 * W_B + (hh + 1) * DH_B].astype(F32)
        s_h = s_w[:, hh:hh + 1]
        g_h = g[:, hh:hh + 1]
        n_old = n_ref[:, hh, :]
        s_qk = s_h * jnp.sum(qf * kf, axis=-1, keepdims=True)
        den = s_qk + g_h * jnp.sum(n_old * qf, axis=-1, keepdims=True)
        denom = jnp.maximum(jnp.abs(den), e_neg[:, hh:hh + 1])
        q_ref[:, sl] = qf
        k_ref[:, sl] = kf
        vs_ref[:, sl] = s_h * vf
        gfull_ref[:, sl] = jnp.broadcast_to(g_h, qf.shape)
        numa_ref[:, sl] = s_qk * vf
        denom_ref[:, sl] = jnp.broadcast_to(denom, qf.shape)
        nnew_ref[:, hh, :] = g_h * n_old + s_h * kf


def _sample_post_kernel(x_ref, p_ref, ya_ref, cq_ref, gfull_ref, numa_ref, denom_ref, osig_ref,
                        skipxc_ref, zsilu_ref, vec_ref, w_out_ref, w_pg_ref, w_pp_ref, y_ref, ycat_ref):
    vp = _vector_params(vec_ref)
    gn_g_ref, b_pg_ref, ple_g_ref, fin_g_ref = vp["gn_g"], vp["b_pg"], vp["ple_g"], vp["fin_g"]
    ycat_ref[:, 0:W_A] = ya_ref[...]
    for hh in range(H_B):
        sl = slice(hh * DH_B, (hh + 1) * DH_B)
        hcell = (numa_ref[:, sl] + gfull_ref[:, sl] * cq_ref[:, sl]) / denom_ref[:, sl]
        hn = _layernorm(hcell * osig_ref[:, sl], gn_g_ref[:, sl])
        ycat_ref[:, W_A + hh * DH_B:W_A + (hh + 1) * DH_B] = ((hn + skipxc_ref[:, sl]) * zsilu_ref[:, sl]).astype(BF16)
    x1 = x_ref[:, 0, :] + _dot(ycat_ref[...], w_out_ref[...])
    e = _rmsnorm(_dot(p_ref[:, 0, :].astype(BF16), w_pp_ref[...]), ple_g_ref[...])
    gate = jax.nn.sigmoid(_dot(x1.astype(BF16), w_pg_ref[...]) + b_pg_ref[...])
    x2 = x1 + gate * e
    y_ref[:, 0, :] = _rmsnorm(x2, fin_g_ref[...])


def _sample_pre_call(before_w, w_in_f32, w_qkv_f32, w_if, w_out_f32, w_tail_f32, out_shape):
    nb = before_w[0].shape[0]
    n_slabs = N_IN // W_A
    whole = lambda s: pl.BlockSpec(s.shape, lambda i, nd=len(s.shape): (0,) * nd)
    slab_spec = pl.BlockSpec((D_MODEL, W_A), lambda i: (0, i))
    out_rows = pl.BlockSpec((w_out_f32.shape[0] // 2, D_MODEL), lambda i: (jnp.minimum(i, 1), 0))
    cast = lambda a: jax.ShapeDtypeStruct(a.shape, BF16)
    n_plain = len(out_shape)
    out_shape = (tuple(out_shape) + (cast(w_in_f32),) + tuple(cast(a) for a in w_qkv_f32)
                 + (cast(w_out_f32),) + tuple(cast(a) for a in w_tail_f32))
    after_w = list(w_qkv_f32) + [w_if]
    return pl.pallas_call(
        _sample_pre_kernel,
        grid=(n_slabs,),
        in_specs=[_resident(a.shape) for a in before_w] + [slab_spec] + [_resident(a.shape) for a in after_w]
        + [out_rows] + [_resident(a.shape) for a in w_tail_f32],
        out_specs=tuple(whole(s) for s in out_shape[:n_plain]) + (slab_spec,)
        + tuple(whole(cast(a)) for a in w_qkv_f32) + (out_rows,) + tuple(whole(cast(a)) for a in w_tail_f32),
        out_shape=out_shape,
        scratch_shapes=[pltpu.VMEM((nb, 3 * W_B), BF16), pltpu.VMEM((nb, D_MODEL), BF16),
                        pltpu.VMEM((n_slabs, nb, W_A), F32)],
        compiler_params=pltpu.CompilerParams(
            dimension_semantics=("arbitrary",), vmem_limit_bytes=SAMPLE_VMEM_BYTES),
        name="sample_pre",
    )(*before_w, w_in_f32, *after_w, w_out_f32, *w_tail_f32)


def _single_step_call(body, inputs, out_shape, scratch, name):
    return pl.pallas_call(
        body,
        grid=(1,),
        in_specs=[_resident(a.shape) for a in inputs],
        out_specs=tuple(pl.BlockSpec(s.shape, lambda *_, nd=len(s.shape): (0,) * nd) for s in out_shape),
        out_shape=out_shape,
        scratch_shapes=scratch,
        compiler_params=pltpu.CompilerParams(
            dimension_semantics=("arbitrary",), vmem_limit_bytes=SAMPLE_VMEM_BYTES),
        name=name,
    )(*inputs)


def kernel(x_prompt, x_sample, state_mlstm_C, state_mlstm_n, state_mlstm_m, state_conv, p_prompt, p_sample,
           norm_in_g, w_in, ln_v_g, ln_v_b, w_spatial, b_spatial, conv_w, conv_b, w_q, w_k, w_v, w_if, b_if,
           gn_g, skip, w_out, w_ple_gate, b_ple_gate, w_ple_proj, ple_norm_g, final_norm_g):
    assert norm_in_g.shape[0] == 1, "single-layer trunk"
    nb = x_sample.shape[0]

    w_sp = w_spatial[0]
    b_sp_full = jnp.repeat(b_spatial[0].T, DH_A, axis=1)
    k_fix = jnp.concatenate([jnp.ones((W_B, 1), F32), jnp.full((W_B, 1), DH_B ** 0.5, F32), jnp.ones((W_B, 1), F32)], 0)
    w_if_p = jnp.pad(w_if[0] * k_fix, ((0, 0), (0, LANES - 2 * H_B))).astype(BF16)
    w00 = jnp.repeat(w_sp[:, 0, 0], DH_A)
    b0 = jnp.repeat(b_spatial[0][:, 0], DH_A)
    vec = _pack_vectors(norm_in_g[0], ln_v_g[0], ln_v_b[0], conv_b[0], gn_g[0], skip[0], b_ple_gate[0],
                        ple_norm_g[0], final_norm_g, conv_w[0], b_if[0], w00, b0)

    full = jax.ShapeDtypeStruct((nb, W_B), F32)
    pre_out = (
        jax.ShapeDtypeStruct((nb, 1, W_A), F32),
        jax.ShapeDtypeStruct((nb, W_A), BF16),
        jax.ShapeDtypeStruct((nb, CONV_W - 1, W_B), F32),
        full, full, full, full, full, full, full, full, full,
        jax.ShapeDtypeStruct((nb, H_B, DH_B), F32),
        jax.ShapeDtypeStruct((nb, H_B), F32),
    )
    (vn_s, ya_s, conv_new, q_s, k_s, vs_s, gfull, numa, denom, osig, skipxc, zsilu, n_new, m_new,
     w_in_b, wq_b, wk_b, wv_b, w_out_b, w_pg_b, w_pp_b) = _sample_pre_call(
        [x_sample, state_conv[0], state_mlstm_n[0], state_mlstm_m[0], vec], w_in[0],
        [w_q[0], w_k[0], w_v[0]], w_if_p, w_out[0], [w_ple_gate[0], w_ple_proj[0]], pre_out)

    prompt_weights = [vec, w_in_b, w_sp, b_sp_full, wq_b, wk_b, wv_b, w_if_p, w_out_b, w_pg_b, w_pp_b]
    y_p, c_p, n_p, m_p, conv_p, c_new, cq = _prompt_call(
        x_prompt, p_prompt[0], prompt_weights, state_mlstm_C[0], [q_s, k_s, vs_s, gfull])

    (y_s,) = _single_step_call(
        _sample_post_kernel,
        [x_sample, p_sample[0], ya_s, cq, gfull, numa, denom, osig, skipxc, zsilu, vec, w_out_b, w_pg_b, w_pp_b],
        (jax.ShapeDtypeStruct((nb, 1, D_MODEL), F32),), [pltpu.VMEM((nb, W_A + W_B), BF16)], "sample_post")

    return (y_p, y_s, c_p[None], n_p[None], m_p[:, :, 0, 0][None], conv_p[None],
            c_new[None], n_new[None], m_new[None], conv_new[None], vn_s[None])
```

```python
import jax
import jax.numpy as jnp
from jax import lax
from jax.experimental import pallas as pl
from jax.experimental.pallas import tpu as pltpu

D_MODEL = 1024
W_A = 1024
H_A = 8
DH_A = 128
W_B = 1024
H_B = 4
DH_B = 256
CHUNK = 128
CONV_W = 4
D_PLE = 256
N_IN = 3 * W_A + 3 * W_B
EPS = 1e-6

LANES = 128
SUBLANES = 8
PROMPT_TILE = 256
SEQS_PER_STEP = 1
STEP_ORDER = (
    "ple", "norm", "conv", "u", "v", "za", "qkv", "heads_lo", "gate_o", "heads_hi", "gate_z",
    "unit", "unit", "unit", "unit", "unit", "merge_a", "unit", "unit", "unit",
    "merge_b", "pair", "pair", "pair", "out", "pair", "pair", "pair", "pair", "pair",
)
PROMPT_VMEM_BYTES = 56 * 1024 * 1024
SAMPLE_VMEM_BYTES = 58 * 1024 * 1024

F32 = jnp.float32
BF16 = jnp.bfloat16
NT_DIMS = (((1,), (1,)), ((), ()))
TN_DIMS = (((0,), (0,)), ((), ()))


def _rmsnorm(x, g):
    return x * lax.rsqrt(jnp.mean(x * x, axis=-1, keepdims=True) + EPS) * g


def _layernorm(x, g, b=None):
    mu = jnp.mean(x, axis=-1, keepdims=True)
    d = x - mu
    var = jnp.mean(d * d, axis=-1, keepdims=True)
    y = d * lax.rsqrt(var + EPS) * g
    return y if b is None else y + b


def _gelu(x):
    return 0.5 * x * (1.0 + lax.erf(x * (0.5 ** 0.5)))


def _silu(x):
    return x * jax.nn.sigmoid(x)


def _dot(a, b):
    return jnp.dot(a, b, preferred_element_type=F32)


def _cumsum_rows(tril_b, x):
    x1 = x.astype(BF16)
    r1 = x - x1.astype(F32)
    x2 = r1.astype(BF16)
    x3 = (r1 - x2.astype(F32)).astype(BF16)
    return _dot(tril_b, x1) + _dot(tril_b, x2) + _dot(tril_b, x3)


def _vector_params(vec_ref):
    row = lambda i: vec_ref.at[i:i + 1]
    return dict(norm_g=row(0), lnv_g=row(1), lnv_b=row(2), conv_b=row(3), gn_g=row(4), skip=row(5), b_pg=row(6),
                ple_g=row(7), fin_g=row(8), conv_w=vec_ref.at[9:9 + CONV_W], b_if=vec_ref.at[13:14, 0:LANES],
                w00=row(14), b0=row(15))


def _pack_vectors(norm_g, lnv_g, lnv_b, conv_b, gn_g, skip, b_pg, ple_g, fin_g, conv_w, b_if, w00, b0):
    wide = lambda a: jnp.pad(a.reshape(1, -1).astype(F32), ((0, 0), (0, D_MODEL - a.size)))
    rows = [wide(a) for a in (norm_g, lnv_g, lnv_b, conv_b, gn_g, skip, b_pg, ple_g, fin_g)]
    return jnp.concatenate(rows + [conv_w.astype(F32), wide(b_if), wide(w00), wide(b0)], axis=0)


def _prompt_kernel(x_ref, p_ref, vec_ref, w_in_ref, w_sp_ref, b_sp_ref, w_q_ref, w_k_ref, w_v_ref, w_if_ref,
                   w_out_ref, w_pg_ref, w_pp_ref,
                   sc_ref, sq_ref, sk_ref, svs_ref, sg_ref,
                   y_ref, c_ref, n_ref, m_ref, conv_ref, scnew_ref, scq_ref,
                   xpad_ref, qkv_ref, ycat_ref):
    vp = _vector_params(vec_ref)
    norm_g_ref, lnv_g_ref, lnv_b_ref, conv_w_ref, conv_b_ref = (
        vp["norm_g"], vp["lnv_g"], vp["lnv_b"], vp["conv_w"], vp["conv_b"])
    b_if_ref, gn_g_ref, skip_ref, b_pg_ref, ple_g_ref, fin_g_ref = (
        vp["b_if"], vp["gn_g"], vp["skip"], vp["b_pg"], vp["ple_g"], vp["fin_g"])
    n_seq, tt = x_ref.shape[0], x_ref.shape[1]
    n_chunks = tt // CHUNK
    step = pl.program_id(0) * pl.num_programs(1) + pl.program_id(1)

    @pl.when(pl.program_id(1) == 0)
    def _():
        c_ref[...] = jnp.zeros_like(c_ref)
        n_ref[...] = jnp.zeros_like(n_ref)
        m_ref[...] = jnp.zeros_like(m_ref)
        conv_ref[...] = jnp.zeros_like(conv_ref)

    @pl.when(step == 0)
    def _():
        scq_ref[...] = jnp.zeros_like(scq_ref)

    rows_per_step = sc_ref.shape[0]
    group_row = lax.broadcasted_iota(jnp.int32, (SUBLANES, DH_B), 0)

    def sample_state_pair(j, hh):
        sl = slice(hh * DH_B, (hh + 1) * DH_B)
        r = step * rows_per_step + j
        base = pl.multiple_of((r // SUBLANES) * SUBLANES, SUBLANES)
        is_row = group_row == (r - base)
        group = pl.ds(base, SUBLANES)
        c_old = sc_ref[j, hh]
        cq = lax.dot_general(sq_ref[group, sl].astype(BF16), c_old.astype(BF16), NT_DIMS, preferred_element_type=F32)
        scq_ref[group, sl] = jnp.where(is_row, cq, scq_ref[group, sl])
        k_row = jnp.where(is_row, sk_ref[group, sl], 0.0).astype(BF16)
        outer = lax.dot_general(svs_ref[group, sl].astype(BF16), k_row, TN_DIMS, preferred_element_type=F32)
        scnew_ref[j, hh] = sg_ref[pl.ds(r, 1), sl] * c_old + outer

    row = lax.broadcasted_iota(jnp.int32, (CHUNK, CHUNK), 0)
    col = lax.broadcasted_iota(jnp.int32, (CHUNK, CHUNK), 1)
    causal = row >= col
    tril = causal.astype(F32)
    tril_b = causal.astype(BF16)

    class Seq:
        pass

    seqs = [Seq() for _ in range(n_seq)]

    def proj(sq, i):
        return _dot(sq.hb, w_in_ref[:, i * W_A:(i + 1) * W_A])

    def stage_norm(sq, bi):
        sq.hb = _rmsnorm(x_ref[bi], norm_g_ref[...]).astype(BF16)

    def stage_u(sq, bi):
        sq.u = _gelu(proj(sq, 0))

    def stage_v(sq, bi):
        sq.v = _gelu(proj(sq, 1))

    def stage_za(sq, bi):
        sq.z_a = proj(sq, 2)

    def gmlp_heads(sq, bi, heads):
        for hh in heads:
            sl = slice(hh * DH_A, (hh + 1) * DH_A)
            vn = _layernorm(sq.v[:, sl], lnv_g_ref[:, sl], lnv_b_ref[:, sl]).astype(BF16)
            wm = (w_sp_ref[hh] * tril).astype(BF16)
            vn_wide = jnp.concatenate([vn[c * CHUNK:(c + 1) * CHUNK] for c in range(n_chunks)], axis=1)
            sp_wide = _dot(wm, vn_wide)
            for c in range(n_chunks):
                rows = slice(c * CHUNK, (c + 1) * CHUNK)
                sp = sp_wide[:, c * DH_A:(c + 1) * DH_A] + b_sp_ref[:, sl]
                ycat_ref[bi, rows, sl] = (sq.u[rows, sl] * sp * _silu(sq.z_a[rows, sl])).astype(BF16)

    def stage_conv(sq, bi):
        x_b = proj(sq, 3)
        xpad_ref[bi, SUBLANES - 3:SUBLANES, :] = conv_ref[bi]
        xpad_ref[bi, SUBLANES:SUBLANES + tt, :] = x_b
        conv_ref[bi] = x_b[tt - 3:tt, :]
        xc = xpad_ref[bi, SUBLANES - 3:SUBLANES - 3 + tt, :] * conv_w_ref[0:1, :]
        xc = xc + xpad_ref[bi, SUBLANES - 2:SUBLANES - 2 + tt, :] * conv_w_ref[1:2, :]
        xc = xc + xpad_ref[bi, SUBLANES - 1:SUBLANES - 1 + tt, :] * conv_w_ref[2:3, :]
        xc = xc + x_b * conv_w_ref[3:4, :]
        sq.xc = _silu(xc + conv_b_ref[...])
        sq.xcb = sq.xc.astype(BF16)
        sq.xbb = x_b.astype(BF16)

    def stage_gate_o(sq, bi):
        sq.o_sig = jax.nn.sigmoid(proj(sq, 4))

    def stage_gate_z(sq, bi):
        sq.z_silu = _silu(proj(sq, 5))

    def stage_qkv(sq, bi):
        for hh in range(H_B):
            sl = slice(hh * DH_B, (hh + 1) * DH_B)
            qkv_ref[bi, :, hh * DH_B:(hh + 1) * DH_B] = _dot(sq.xcb[:, sl], w_q_ref[hh]).astype(BF16)
            qkv_ref[bi, :, W_B + hh * DH_B:W_B + (hh + 1) * DH_B] = _dot(sq.xcb[:, sl], w_k_ref[hh]).astype(BF16)
            qkv_ref[bi, :, 2 * W_B + hh * DH_B:2 * W_B + (hh + 1) * DH_B] = _dot(sq.xbb[:, sl], w_v_ref[hh]).astype(BF16)
        gates = _dot(qkv_ref[bi], w_if_ref[...]) + b_if_ref[...]
        logf = jax.nn.log_sigmoid(gates)
        sq.decay = []
        for c in range(n_chunks):
            rows = slice(c * CHUNK, (c + 1) * CHUNK)
            bcum = pltpu.roll(_cumsum_rows(tril_b, logf[rows]), LANES - H_B, 1)
            a_all = gates[rows] - bcum
            sq.decay.append((bcum, a_all, a_all.T))

    def mlstm_unit(sq, bi, c, hh):
        rows = slice(c * CHUNK, (c + 1) * CHUNK)
        sl = slice(hh * DH_B, (hh + 1) * DH_B)
        bcum, a_all, a_t = sq.decay[c]
        qh = qkv_ref[bi, rows, hh * DH_B:(hh + 1) * DH_B]
        kh = qkv_ref[bi, rows, W_B + hh * DH_B:W_B + (hh + 1) * DH_B]
        vh = qkv_ref[bi, rows, 2 * W_B + hh * DH_B:2 * W_B + (hh + 1) * DH_B]
        m_prev = m_ref[bi, hh, 0:1, 0:1]
        a_mat = jnp.where(causal, a_t[hh:hh + 1, :], -jnp.inf)
        m_run = jnp.maximum(jnp.max(a_mat, axis=-1, keepdims=True), m_prev)
        s = jnp.exp(a_mat - m_run) * lax.dot_general(qh, kh, NT_DIMS, preferred_element_type=F32)
        g = jnp.exp(m_prev - m_run)
        c_old = c_ref[bi, hh]
        inter = lax.dot_general(qh, c_old.astype(BF16), NT_DIMS, preferred_element_type=F32)
        num = _dot(s.astype(BF16), vh) + g * inter
        n_old = n_ref[bi, hh:hh + 1, :]
        den = jnp.sum(s, axis=-1, keepdims=True) + g * jnp.sum(qh.astype(F32) * n_old, axis=-1, keepdims=True)
        m_t = bcum[:, hh:hh + 1] + m_run
        hcell = num / jnp.maximum(jnp.abs(den), jnp.exp(-m_t))
        m_last = m_run[CHUNK - 1:CHUNK, :]
        w_end = jnp.exp(a_all[:, hh:hh + 1] - m_last)
        g_end = jnp.exp(m_prev - m_last)
        kw = kh.astype(F32) * w_end
        c_ref[bi, hh] = g_end * c_old + lax.dot_general(vh, kw.astype(BF16), TN_DIMS, preferred_element_type=F32)
        n_ref[bi, hh:hh + 1, :] = g_end * n_old + jnp.sum(kw, axis=0, keepdims=True)
        m_ref[bi, hh] = jnp.broadcast_to(m_t[CHUNK - 1:CHUNK, :], (SUBLANES, LANES))
        hn = _layernorm(hcell * sq.o_sig[rows, sl], gn_g_ref[:, sl])
        yb = (hn + skip_ref[:, sl] * sq.xc[rows, sl]) * sq.z_silu[rows, sl]
        ycat_ref[bi, rows, W_A + hh * DH_B:W_A + (hh + 1) * DH_B] = yb.astype(BF16)

    def stage_ple(sq, bi):
        sq.e = _rmsnorm(_dot(p_ref[bi].astype(BF16), w_pp_ref[...]), ple_g_ref[...])

    def stage_merge_a(sq, bi):
        sq.x1 = x_ref[bi] + _dot(ycat_ref[bi, :, :W_A], w_out_ref[:W_A, :])

    def stage_merge_b(sq, bi):
        sq.x1 = sq.x1 + _dot(ycat_ref[bi, :, W_A:], w_out_ref[W_A:, :])

    def stage_out(sq, bi):
        gate = jax.nn.sigmoid(_dot(sq.x1.astype(BF16), w_pg_ref[...]) + b_pg_ref[...])
        x2 = sq.x1 + gate * sq.e
        y_ref[bi] = _rmsnorm(x2, fin_g_ref[...])

    units = iter([(c, hh) for c in range(n_chunks) for hh in range(H_B)])
    pairs = iter([(j, hh) for j in range(rows_per_step) for hh in range(H_B)])
    stages = {
        "norm": stage_norm, "conv": stage_conv, "u": stage_u, "qkv": stage_qkv, "v": stage_v,
        "gate_o": stage_gate_o, "gate_z": stage_gate_z, "za": stage_za,
        "heads_lo": lambda sq, bi: gmlp_heads(sq, bi, list(range(0, H_A // 2))),
        "heads_hi": lambda sq, bi: gmlp_heads(sq, bi, list(range(H_A // 2, H_A))),
        "unit": lambda sq, bi: mlstm_unit(sq, bi, *next(units)),
        "pair": lambda sq, bi: sample_state_pair(*next(pairs)),
        "ple": stage_ple, "merge_a": stage_merge_a, "merge_b": stage_merge_b, "out": stage_out,
    }
    assert STEP_ORDER.count("unit") == n_chunks * H_B and STEP_ORDER.count("pair") == rows_per_step * H_B
    for name in STEP_ORDER:
        stages[name](seqs[0], 0)


def _resident(shape):
    nd = len(shape)
    return pl.BlockSpec(shape, lambda *_: (0,) * nd, pipeline_mode=pl.Buffered(1))


def _prompt_call(x, p, weights, sample_c, sample_rows):
    batch, seq, _ = x.shape
    tt = PROMPT_TILE
    ns = SEQS_PER_STEP
    grid = (batch // ns, seq // tt)
    n_steps = grid[0] * grid[1]
    n_rows = sample_c.shape[0]
    rows_per_step = n_rows // n_steps
    assert rows_per_step * n_steps == n_rows, "sample rows are spread evenly over the prompt grid steps"

    def state_block(b, t):
        return (b * grid[1] + t, 0, 0, 0)

    in_specs = [
        pl.BlockSpec((ns, tt, D_MODEL), lambda b, t: (b, t, 0)),
        pl.BlockSpec((ns, tt, D_PLE), lambda b, t: (b, t, 0)),
    ] + [_resident(w.shape) for w in weights] + [
        pl.BlockSpec((rows_per_step, H_B, DH_B, DH_B), state_block),
    ] + [_resident(a.shape) for a in sample_rows]
    out_shape = (
        jax.ShapeDtypeStruct((batch, seq, D_MODEL), F32),
        jax.ShapeDtypeStruct((batch, H_B, DH_B, DH_B), F32),
        jax.ShapeDtypeStruct((batch, SUBLANES, DH_B), F32),
        jax.ShapeDtypeStruct((batch, H_B, SUBLANES, LANES), F32),
        jax.ShapeDtypeStruct((batch, CONV_W - 1, W_B), F32),
        jax.ShapeDtypeStruct(sample_c.shape, F32),
        jax.ShapeDtypeStruct((n_rows, W_B), F32),
    )
    out_specs = (
        pl.BlockSpec((ns, tt, D_MODEL), lambda b, t: (b, t, 0)),
        pl.BlockSpec((ns, H_B, DH_B, DH_B), lambda b, t: (b, 0, 0, 0)),
        pl.BlockSpec((ns, SUBLANES, DH_B), lambda b, t: (b, 0, 0)),
        pl.BlockSpec((ns, H_B, SUBLANES, LANES), lambda b, t: (b, 0, 0, 0)),
        pl.BlockSpec((ns, CONV_W - 1, W_B), lambda b, t: (b, 0, 0)),
        pl.BlockSpec((rows_per_step, H_B, DH_B, DH_B), state_block),
        pl.BlockSpec((n_rows, W_B), lambda b, t: (0, 0)),
    )
    scratch = [
        pltpu.VMEM((ns, tt + SUBLANES, W_B), F32),
        pltpu.VMEM((ns, tt, 3 * W_B), BF16),
        pltpu.VMEM((ns, tt, W_A + W_B), BF16),
    ]
    return pl.pallas_call(
        _prompt_kernel,
        grid=grid,
        in_specs=in_specs,
        out_specs=out_specs,
        out_shape=out_shape,
        scratch_shapes=scratch,
        compiler_params=pltpu.CompilerParams(
            dimension_semantics=("arbitrary", "arbitrary"),
            vmem_limit_bytes=PROMPT_VMEM_BYTES),
        name="prompt_fused",
    )(x, p, *weights, sample_c, *sample_rows)


def _sample_pre_kernel(x_ref, conv_ref, n_ref, m_ref, vec_ref, w_in_ref,
                       wq_f32_ref, wk_f32_ref, wv_f32_ref, w_if_ref, w_out_f32_ref, w_pg_f32_ref, w_pp_f32_ref,
                       vn_ref, ya_ref, convnew_ref, q_ref, k_ref, vs_ref, gfull_ref, numa_ref,
                       denom_ref, osig_ref, skipxc_ref, zsilu_ref, nnew_ref, mnew_ref, w_in_bf16_ref,
                       w_q_ref, w_k_ref, w_v_ref, w_out_ref, w_pg_ref, w_pp_ref,
                       qkv_ref, hb_ref, proj_ref):
    slab = pl.program_id(0)
    vp = _vector_params(vec_ref)

    @pl.when(slab == 0)
    def _():
        hb_ref[...] = _rmsnorm(x_ref[:, 0, :], vp["norm_g"][...]).astype(BF16)

    w_slab = w_in_ref[...].astype(BF16)
    w_in_bf16_ref[...] = w_slab
    proj_ref[slab] = _dot(hb_ref[...], w_slab)

    @pl.when(slab < 2)
    def _():
        w_out_ref[...] = w_out_f32_ref[...].astype(BF16)

    @pl.when(slab == 2)
    def _():
        w_pg_ref[...] = w_pg_f32_ref[...].astype(BF16)

    @pl.when(slab == 3)
    def _():
        w_pp_ref[...] = w_pp_f32_ref[...].astype(BF16)
        w_q_ref[...] = wq_f32_ref[...].astype(BF16)

    @pl.when(slab == 4)
    def _():
        w_k_ref[...] = (wk_f32_ref[...] * (DH_B ** -0.5)).astype(BF16)
        w_v_ref[...] = wv_f32_ref[...].astype(BF16)

    @pl.when(slab == pl.num_programs(0) - 1)
    def _():
        _sample_front_end(conv_ref, n_ref, m_ref, vp["lnv_g"], vp["lnv_b"], vp["w00"], vp["b0"], vp["conv_w"],
                          vp["conv_b"], w_q_ref, w_k_ref, w_v_ref, w_if_ref, vp["b_if"], vp["skip"],
                          vn_ref, ya_ref, convnew_ref, q_ref, k_ref, vs_ref, gfull_ref, numa_ref,
                          denom_ref, osig_ref, skipxc_ref, zsilu_ref, nnew_ref, mnew_ref, qkv_ref, proj_ref)


def _sample_front_end(conv_ref, n_ref, m_ref, lnv_g_ref, lnv_b_ref, w00_ref, b0_ref, conv_w_ref, conv_b_ref,
                      w_q_ref, w_k_ref, w_v_ref, w_if_ref, b_if_ref, skip_ref,
                      vn_ref, ya_ref, convnew_ref, q_ref, k_ref, vs_ref, gfull_ref, numa_ref,
                      denom_ref, osig_ref, skipxc_ref, zsilu_ref, nnew_ref, mnew_ref, qkv_ref, proj_ref):
    def proj(i):
        return proj_ref[i]

    u = _gelu(proj(0))
    v = _gelu(proj(1))
    z_a = proj(2)
    for hh in range(H_A):
        sl = slice(hh * DH_A, (hh + 1) * DH_A)
        vn = _layernorm(v[:, sl], lnv_g_ref[:, sl], lnv_b_ref[:, sl])
        vn_ref[:, 0, sl] = vn
        sp = vn * w00_ref[:, sl] + b0_ref[:, sl]
        ya_ref[:, sl] = (u[:, sl] * sp * _silu(z_a[:, sl])).astype(BF16)

    x_b = proj(3)
    osig_ref[...] = jax.nn.sigmoid(proj(4))
    zsilu_ref[...] = _silu(proj(5))
    c0 = conv_ref[:, 0, :]
    c1 = conv_ref[:, 1, :]
    c2 = conv_ref[:, 2, :]
    xc = c0 * conv_w_ref[0:1, :]
    xc = xc + c1 * conv_w_ref[1:2, :]
    xc = xc + c2 * conv_w_ref[2:3, :]
    xc = xc + x_b * conv_w_ref[3:4, :]
    xc = _silu(xc + conv_b_ref[...])
    convnew_ref[:, 0, :] = c1
    convnew_ref[:, 1, :] = c2
    convnew_ref[:, 2, :] = x_b
    skipxc_ref[...] = skip_ref[...] * xc
    xcb = xc.astype(BF16)
    xbb = x_b.astype(BF16)
    for hh in range(H_B):
        sl = slice(hh * DH_B, (hh + 1) * DH_B)
        qkv_ref[:, hh * DH_B:(hh + 1) * DH_B] = _dot(xcb[:, sl], w_q_ref[hh]).astype(BF16)
        qkv_ref[:, W_B + hh * DH_B:W_B + (hh + 1) * DH_B] = _dot(xcb[:, sl], w_k_ref[hh]).astype(BF16)
        qkv_ref[:, 2 * W_B + hh * DH_B:2 * W_B + (hh + 1) * DH_B] = _dot(xbb[:, sl], w_v_ref[hh]).astype(BF16)
    gates = _dot(qkv_ref[...], w_if_ref[...]) + b_if_ref[...]
    logf = pltpu.roll(jax.nn.log_sigmoid(gates), LANES - H_B, 1)
    li = gates[:, 0:H_B]
    inter = logf[:, 0:H_B] + m_ref[...]
    m_t = jnp.maximum(inter, li)
    s_w = jnp.exp(li - m_t)
    g = jnp.exp(inter - m_t)
    e_neg = jnp.exp(-m_t)
    mnew_ref[...] = m_t
    for hh in range(H_B):
        sl = slice(hh * DH_B, (hh + 1) * DH_B)
        qf = qkv_ref[:, hh * DH_B:(hh + 1) * DH_B].astype(F32)
        kf = qkv_ref[:, W_B + hh * DH_B:W_B + (hh + 1) * DH_B].astype(F32)
        vf = qkv_ref[:, 2 * W_B + hh * DH_B:2 * W_B + (hh + 1) * DH_B].astype(F32)
        s_h = s_w[:, hh:hh + 1]
        g_h = g[:, hh:hh + 1]
        n_old = n_ref[:, hh, :]
        s_qk = s_h * jnp.sum(qf * kf, axis=-1, keepdims=True)
        den = s_qk + g_h * jnp.sum(n_old * qf, axis=-1, keepdims=True)
        denom = jnp.maximum(jnp.abs(den), e_neg[:, hh:hh + 1])
        q_ref[:, sl] = qf
        k_ref[:, sl] = kf
        vs_ref[:, sl] = s_h * vf
        gfull_ref[:, sl] = jnp.broadcast_to(g_h, qf.shape)
        numa_ref[:, sl] = s_qk * vf
        denom_ref[:, sl] = jnp.broadcast_to(denom, qf.shape)
        nnew_ref[:, hh, :] = g_h * n_old + s_h * kf


def _sample_post_kernel(x_ref, p_ref, ya_ref, cq_ref, gfull_ref, numa_ref, denom_ref, osig_ref,
                        skipxc_ref, zsilu_ref, vec_ref, w_out_ref, w_pg_ref, w_pp_ref, y_ref, ycat_ref):
    vp = _vector_params(vec_ref)
    gn_g_ref, b_pg_ref, ple_g_ref, fin_g_ref = vp["gn_g"], vp["b_pg"], vp["ple_g"], vp["fin_g"]
    ycat_ref[:, 0:W_A] = ya_ref[...]
    for hh in range(H_B):
        sl = slice(hh * DH_B, (hh + 1) * DH_B)
        hcell = (numa_ref[:, sl] + gfull_ref[:, sl] * cq_ref[:, sl]) / denom_ref[:, sl]
        hn = _layernorm(hcell * osig_ref[:, sl], gn_g_ref[:, sl])
        ycat_ref[:, W_A + hh * DH_B:W_A + (hh + 1) * DH_B] = ((hn + skipxc_ref[:, sl]) * zsilu_ref[:, sl]).astype(BF16)
    x1 = x_ref[:, 0, :] + _dot(ycat_ref[...], w_out_ref[...])
    e = _rmsnorm(_dot(p_ref[:, 0, :].astype(BF16), w_pp_ref[...]), ple_g_ref[...])
    gate = jax.nn.sigmoid(_dot(x1.astype(BF16), w_pg_ref[...]) + b_pg_ref[...])
    x2 = x1 + gate * e
    y_ref[:, 0, :] = _rmsnorm(x2, fin_g_ref[...])


def _sample_pre_call(before_w, w_in_f32, w_qkv_f32, w_if, w_out_f32, w_tail_f32, out_shape):
    nb = before_w[0].shape[0]
    n_slabs = N_IN // W_A
    whole = lambda s: pl.BlockSpec(s.shape, lambda i, nd=len(s.shape): (0,) * nd)
    slab_spec = pl.BlockSpec((D_MODEL, W_A), lambda i: (0, i))
    out_rows = pl.BlockSpec((w_out_f32.shape[0] // 2, D_MODEL), lambda i: (jnp.minimum(i, 1), 0))
    cast = lambda a: jax.ShapeDtypeStruct(a.shape, BF16)
    n_plain = len(out_shape)
    out_shape = (tuple(out_shape) + (cast(w_in_f32),) + tuple(cast(a) for a in w_qkv_f32)
                 + (cast(w_out_f32),) + tuple(cast(a) for a in w_tail_f32))
    after_w = list(w_qkv_f32) + [w_if]
    return pl.pallas_call(
        _sample_pre_kernel,
        grid=(n_slabs,),
        in_specs=[_resident(a.shape) for a in before_w] + [slab_spec] + [_resident(a.shape) for a in after_w]
        + [out_rows] + [_resident(a.shape) for a in w_tail_f32],
        out_specs=tuple(whole(s) for s in out_shape[:n_plain]) + (slab_spec,)
        + tuple(whole(cast(a)) for a in w_qkv_f32) + (out_rows,) + tuple(whole(cast(a)) for a in w_tail_f32),
        out_shape=out_shape,
        scratch_shapes=[pltpu.VMEM((nb, 3 * W_B), BF16), pltpu.VMEM((nb, D_MODEL), BF16),
                        pltpu.VMEM((n_slabs, nb, W_A), F32)],
        compiler_params=pltpu.CompilerParams(
            dimension_semantics=("arbitrary",), vmem_limit_bytes=SAMPLE_VMEM_BYTES),
        name="sample_pre",
    )(*before_w, w_in_f32, *after_w, w_out_f32, *w_tail_f32)


def _single_step_call(body, inputs, out_shape, scratch, name):
    return pl.pallas_call(
        body,
        grid=(1,),
        in_specs=[_resident(a.shape) for a in inputs],
        out_specs=tuple(pl.BlockSpec(s.shape, lambda *_, nd=len(s.shape): (0,) * nd) for s in out_shape),
        out_shape=out_shape,
        scratch_shapes=scratch,
        compiler_params=pltpu.CompilerParams(
            dimension_semantics=("arbitrary",), vmem_limit_bytes=SAMPLE_VMEM_BYTES),
        name=name,
    )(*inputs)


def kernel(x_prompt, x_sample, state_mlstm_C, state_mlstm_n, state_mlstm_m, state_conv, p_prompt, p_sample,
           norm_in_g, w_in, ln_v_g, ln_v_b, w_spatial, b_spatial, conv_w, conv_b, w_q, w_k, w_v, w_if, b_if,
           gn_g, skip, w_out, w_ple_gate, b_ple_gate, w_ple_proj, ple_norm_g, final_norm_g):
    assert norm_in_g.shape[0] == 1, "single-layer trunk"
    nb = x_sample.shape[0]

    w_sp = w_spatial[0]
    b_sp_full = jnp.repeat(b_spatial[0].T, DH_A, axis=1)
    k_fix = jnp.concatenate([jnp.ones((W_B, 1), F32), jnp.full((W_B, 1), DH_B ** 0.5, F32), jnp.ones((W_B, 1), F32)], 0)
    w_if_p = jnp.pad(w_if[0] * k_fix, ((0, 0), (0, LANES - 2 * H_B))).astype(BF16)
    w00 = jnp.repeat(w_sp[:, 0, 0], DH_A)
    b0 = jnp.repeat(b_spatial[0][:, 0], DH_A)
    vec = _pack_vectors(norm_in_g[0], ln_v_g[0], ln_v_b[0], conv_b[0], gn_g[0], skip[0], b_ple_gate[0],
                        ple_norm_g[0], final_norm_g, conv_w[0], b_if[0], w00, b0)

    full = jax.ShapeDtypeStruct((nb, W_B), F32)
    pre_out = (
        jax.ShapeDtypeStruct((nb, 1, W_A), F32),
        jax.ShapeDtypeStruct((nb, W_A), BF16),
        jax.ShapeDtypeStruct((nb, CONV_W - 1, W_B), F32),
        full, full, full, full, full, full, full, full, full,
        jax.ShapeDtypeStruct((nb, H_B, DH_B), F32),
        jax.ShapeDtypeStruct((nb, H_B), F32),
    )
    (vn_s, ya_s, conv_new, q_s, k_s, vs_s, gfull, numa, denom, osig, skipxc, zsilu, n_new, m_new,
     w_in_b, wq_b, wk_b, wv_b, w_out_b, w_pg_b, w_pp_b) = _sample_pre_call(
        [x_sample, state_conv[0], state_mlstm_n[0], state_mlstm_m[0], vec], w_in[0],
        [w_q[0], w_k[0], w_v[0]], w_if_p, w_out[0], [w_ple_gate[0], w_ple_proj[0]], pre_out)

    prompt_weights = [vec, w_in_b, w_sp, b_sp_full, wq_b, wk_b, wv_b, w_if_p, w_out_b, w_pg_b, w_pp_b]
    y_p, c_p, n_p, m_p, conv_p, c_new, cq = _prompt_call(
        x_prompt, p_prompt[0], prompt_weights, state_mlstm_C[0], [q_s, k_s, vs_s, gfull])

    (y_s,) = _single_step_call(
        _sample_post_kernel,
        [x_sample, p_sample[0], ya_s, cq, gfull, numa, denom, osig, skipxc, zsilu, vec, w_out_b, w_pg_b, w_pp_b],
        (jax.ShapeDtypeStruct((nb, 1, D_MODEL), F32),), [pltpu.VMEM((nb, W_A + W_B), BF16)], "sample_post")

    return (y_p, y_s, c_p[None], n_p[:, :H_B][None], m_p[:, :, 0, 0][None], conv_p[None],
            c_new[None], n_new[None], m_new[None], conv_new[None], vn_s[None])
```

```python
import jax
import jax.numpy as jnp
from jax import lax
from jax.experimental import pallas as pl
from jax.experimental.pallas import tpu as pltpu

D_MODEL = 1024
W_A = 1024
H_A = 8
DH_A = 128
W_B = 1024
H_B = 4
DH_B = 256
CHUNK = 128
CONV_W = 4
D_PLE = 256
N_IN = 3 * W_A + 3 * W_B
EPS = 1e-6

LANES = 128
SUBLANES = 8
PROMPT_TILE = 256
SEQS_PER_STEP = 1
STEP_ORDER = (
    "ple", "norm", "conv", "u", "v", "za", "qkv", "heads_lo", "gate_o", "heads_hi", "gate_z",
    "unit", "unit", "unit", "unit", "unit", "merge_a", "unit", "unit", "unit",
    "merge_b", "pair", "pair", "pair", "out", "pair", "pair", "pair", "pair", "pair",
)
PROMPT_VMEM_BYTES = 60 * 1024 * 1024
SAMPLE_VMEM_BYTES = 58 * 1024 * 1024

F32 = jnp.float32
BF16 = jnp.bfloat16
NT_DIMS = (((1,), (1,)), ((), ()))
TN_DIMS = (((0,), (0,)), ((), ()))


def _rmsnorm(x, g):
    return x * lax.rsqrt(jnp.mean(x * x, axis=-1, keepdims=True) + EPS) * g


def _layernorm(x, g, b=None):
    mu = jnp.mean(x, axis=-1, keepdims=True)
    d = x - mu
    var = jnp.mean(d * d, axis=-1, keepdims=True)
    y = d * lax.rsqrt(var + EPS) * g
    return y if b is None else y + b


def _gelu(x):
    return 0.5 * x * (1.0 + lax.erf(x * (0.5 ** 0.5)))


def _silu(x):
    return x * jax.nn.sigmoid(x)


def _dot(a, b):
    return jnp.dot(a, b, preferred_element_type=F32)


def _cumsum_rows(tril_b, x):
    x1 = x.astype(BF16)
    r1 = x - x1.astype(F32)
    x2 = r1.astype(BF16)
    x3 = (r1 - x2.astype(F32)).astype(BF16)
    return _dot(tril_b, x1) + _dot(tril_b, x2) + _dot(tril_b, x3)


def _vector_params(vec_ref):
    row = lambda i: vec_ref.at[i:i + 1]
    return dict(norm_g=row(0), lnv_g=row(1), lnv_b=row(2), conv_b=row(3), gn_g=row(4), skip=row(5), b_pg=row(6),
                ple_g=row(7), fin_g=row(8), conv_w=vec_ref.at[9:9 + CONV_W], b_if=vec_ref.at[13:14, 0:LANES],
                w00=row(14), b0=row(15))


def _pack_vectors(norm_g, lnv_g, lnv_b, conv_b, gn_g, skip, b_pg, ple_g, fin_g, conv_w, b_if, w00, b0):
    wide = lambda a: jnp.pad(a.reshape(1, -1).astype(F32), ((0, 0), (0, D_MODEL - a.size)))
    rows = [wide(a) for a in (norm_g, lnv_g, lnv_b, conv_b, gn_g, skip, b_pg, ple_g, fin_g)]
    return jnp.concatenate(rows + [conv_w.astype(F32), wide(b_if), wide(w00), wide(b0)], axis=0)


def _prompt_kernel(x_ref, p_ref, vec_ref, w_in_ref, w_sp_ref, b_sp_ref, w_q_ref, w_k_ref, w_v_ref, w_if_ref,
                   w_out_ref, w_pg_ref, w_pp_ref,
                   sc_ref, sq_ref, sk_ref, svs_ref, sg_ref,
                   sx_ref, sp_ref, sya_ref, snuma_ref, sdenom_ref, sosig_ref, sskipxc_ref, szsilu_ref,
                   y_ref, c_ref, n_ref, m_ref, conv_ref, scnew_ref, scq_ref, sy_ref,
                   xpad_ref, qkv_ref, ycat_ref):
    vp = _vector_params(vec_ref)
    norm_g_ref, lnv_g_ref, lnv_b_ref, conv_w_ref, conv_b_ref = (
        vp["norm_g"], vp["lnv_g"], vp["lnv_b"], vp["conv_w"], vp["conv_b"])
    b_if_ref, gn_g_ref, skip_ref, b_pg_ref, ple_g_ref, fin_g_ref = (
        vp["b_if"], vp["gn_g"], vp["skip"], vp["b_pg"], vp["ple_g"], vp["fin_g"])
    n_seq, tt = x_ref.shape[0], x_ref.shape[1]
    n_chunks = tt // CHUNK
    step = pl.program_id(0) * pl.num_programs(1) + pl.program_id(1)

    @pl.when(pl.program_id(1) == 0)
    def _():
        c_ref[...] = jnp.zeros_like(c_ref)
        n_ref[...] = jnp.zeros_like(n_ref)
        m_ref[...] = jnp.zeros_like(m_ref)
        conv_ref[...] = jnp.zeros_like(conv_ref)

    @pl.when(step == 0)
    def _():
        scq_ref[...] = jnp.zeros_like(scq_ref)

    rows_per_step = sc_ref.shape[0]
    group_row = lax.broadcasted_iota(jnp.int32, (SUBLANES, DH_B), 0)

    def sample_state_pair(j, hh):
        sl = slice(hh * DH_B, (hh + 1) * DH_B)
        r = step * rows_per_step + j
        base = pl.multiple_of((r // SUBLANES) * SUBLANES, SUBLANES)
        is_row = group_row == (r - base)
        group = pl.ds(base, SUBLANES)
        c_old = sc_ref[j, hh]
        cq = lax.dot_general(sq_ref[group, sl].astype(BF16), c_old.astype(BF16), NT_DIMS, preferred_element_type=F32)
        scq_ref[group, sl] = jnp.where(is_row, cq, scq_ref[group, sl])
        k_row = jnp.where(is_row, sk_ref[group, sl], 0.0).astype(BF16)
        outer = lax.dot_general(svs_ref[group, sl].astype(BF16), k_row, TN_DIMS, preferred_element_type=F32)
        scnew_ref[j, hh] = sg_ref[pl.ds(r, 1), sl] * c_old + outer

    row = lax.broadcasted_iota(jnp.int32, (CHUNK, CHUNK), 0)
    col = lax.broadcasted_iota(jnp.int32, (CHUNK, CHUNK), 1)
    causal = row >= col
    tril = causal.astype(F32)
    tril_b = causal.astype(BF16)

    class Seq:
        pass

    seqs = [Seq() for _ in range(n_seq)]

    def proj(sq, i):
        return _dot(sq.hb, w_in_ref[:, i * W_A:(i + 1) * W_A])

    def stage_norm(sq, bi):
        sq.hb = _rmsnorm(x_ref[bi], norm_g_ref[...]).astype(BF16)

    def stage_u(sq, bi):
        sq.u = _gelu(proj(sq, 0))

    def stage_v(sq, bi):
        sq.v = _gelu(proj(sq, 1))

    def stage_za(sq, bi):
        sq.z_a = proj(sq, 2)

    def gmlp_heads(sq, bi, heads):
        for hh in heads:
            sl = slice(hh * DH_A, (hh + 1) * DH_A)
            vn = _layernorm(sq.v[:, sl], lnv_g_ref[:, sl], lnv_b_ref[:, sl]).astype(BF16)
            wm = (w_sp_ref[hh] * tril).astype(BF16)
            vn_wide = jnp.concatenate([vn[c * CHUNK:(c + 1) * CHUNK] for c in range(n_chunks)], axis=1)
            sp_wide = _dot(wm, vn_wide)
            for c in range(n_chunks):
                rows = slice(c * CHUNK, (c + 1) * CHUNK)
                sp = sp_wide[:, c * DH_A:(c + 1) * DH_A] + b_sp_ref[:, sl]
                ycat_ref[bi, rows, sl] = (sq.u[rows, sl] * sp * _silu(sq.z_a[rows, sl])).astype(BF16)

    def stage_conv(sq, bi):
        x_b = proj(sq, 3)
        xpad_ref[bi, SUBLANES - 3:SUBLANES, :] = conv_ref[bi]
        xpad_ref[bi, SUBLANES:SUBLANES + tt, :] = x_b
        conv_ref[bi] = x_b[tt - 3:tt, :]
        xc = xpad_ref[bi, SUBLANES - 3:SUBLANES - 3 + tt, :] * conv_w_ref[0:1, :]
        xc = xc + xpad_ref[bi, SUBLANES - 2:SUBLANES - 2 + tt, :] * conv_w_ref[1:2, :]
        xc = xc + xpad_ref[bi, SUBLANES - 1:SUBLANES - 1 + tt, :] * conv_w_ref[2:3, :]
        xc = xc + x_b * conv_w_ref[3:4, :]
        sq.xc = _silu(xc + conv_b_ref[...])
        sq.xcb = sq.xc.astype(BF16)
        sq.xbb = x_b.astype(BF16)

    def stage_gate_o(sq, bi):
        sq.o_sig = jax.nn.sigmoid(proj(sq, 4))

    def stage_gate_z(sq, bi):
        sq.z_silu = _silu(proj(sq, 5))

    def stage_qkv(sq, bi):
        for hh in range(H_B):
            sl = slice(hh * DH_B, (hh + 1) * DH_B)
            qkv_ref[bi, :, hh * DH_B:(hh + 1) * DH_B] = _dot(sq.xcb[:, sl], w_q_ref[hh]).astype(BF16)
            qkv_ref[bi, :, W_B + hh * DH_B:W_B + (hh + 1) * DH_B] = _dot(sq.xcb[:, sl], w_k_ref[hh]).astype(BF16)
            qkv_ref[bi, :, 2 * W_B + hh * DH_B:2 * W_B + (hh + 1) * DH_B] = _dot(sq.xbb[:, sl], w_v_ref[hh]).astype(BF16)
        gates = _dot(qkv_ref[bi], w_if_ref[...]) + b_if_ref[...]
        logf = jax.nn.log_sigmoid(gates)
        sq.decay = []
        for c in range(n_chunks):
            rows = slice(c * CHUNK, (c + 1) * CHUNK)
            bcum = pltpu.roll(_cumsum_rows(tril_b, logf[rows]), LANES - H_B, 1)
            a_all = gates[rows] - bcum
            sq.decay.append((bcum, a_all, a_all.T))

    def mlstm_unit(sq, bi, c, hh):
        rows = slice(c * CHUNK, (c + 1) * CHUNK)
        sl = slice(hh * DH_B, (hh + 1) * DH_B)
        bcum, a_all, a_t = sq.decay[c]
        qh = qkv_ref[bi, rows, hh * DH_B:(hh + 1) * DH_B]
        kh = qkv_ref[bi, rows, W_B + hh * DH_B:W_B + (hh + 1) * DH_B]
        vh = qkv_ref[bi, rows, 2 * W_B + hh * DH_B:2 * W_B + (hh + 1) * DH_B]
        m_prev = m_ref[bi, hh, 0:1, 0:1]
        a_mat = jnp.where(causal, a_t[hh:hh + 1, :], -jnp.inf)
        m_run = jnp.maximum(jnp.max(a_mat, axis=-1, keepdims=True), m_prev)
        s = jnp.exp(a_mat - m_run) * lax.dot_general(qh, kh, NT_DIMS, preferred_element_type=F32)
        g = jnp.exp(m_prev - m_run)
        c_old = c_ref[bi, hh]
        inter = lax.dot_general(qh, c_old.astype(BF16), NT_DIMS, preferred_element_type=F32)
        num = _dot(s.astype(BF16), vh) + g * inter
        n_old = n_ref[bi, hh:hh + 1, :]
        den = jnp.sum(s, axis=-1, keepdims=True) + g * jnp.sum(qh.astype(F32) * n_old, axis=-1, keepdims=True)
        m_t = bcum[:, hh:hh + 1] + m_run
        hcell = num / jnp.maximum(jnp.abs(den), jnp.exp(-m_t))
        m_last = m_run[CHUNK - 1:CHUNK, :]
        w_end = jnp.exp(a_all[:, hh:hh + 1] - m_last)
        g_end = jnp.exp(m_prev - m_last)
        kw = kh.astype(F32) * w_end
        c_ref[bi, hh] = g_end * c_old + lax.dot_general(vh, kw.astype(BF16), TN_DIMS, preferred_element_type=F32)
        n_ref[bi, hh:hh + 1, :] = g_end * n_old + jnp.sum(kw, axis=0, keepdims=True)
        m_ref[bi, hh] = jnp.broadcast_to(m_t[CHUNK - 1:CHUNK, :], (SUBLANES, LANES))
        hn = _layernorm(hcell * sq.o_sig[rows, sl], gn_g_ref[:, sl])
        yb = (hn + skip_ref[:, sl] * sq.xc[rows, sl]) * sq.z_silu[rows, sl]
        ycat_ref[bi, rows, W_A + hh * DH_B:W_A + (hh + 1) * DH_B] = yb.astype(BF16)

    def stage_ple(sq, bi):
        sq.e = _rmsnorm(_dot(p_ref[bi].astype(BF16), w_pp_ref[...]), ple_g_ref[...])

    def stage_merge_a(sq, bi):
        sq.x1 = x_ref[bi] + _dot(ycat_ref[bi, :, :W_A], w_out_ref[:W_A, :])

    def stage_merge_b(sq, bi):
        sq.x1 = sq.x1 + _dot(ycat_ref[bi, :, W_A:], w_out_ref[W_A:, :])

    def stage_out(sq, bi):
        gate = jax.nn.sigmoid(_dot(sq.x1.astype(BF16), w_pg_ref[...]) + b_pg_ref[...])
        x2 = sq.x1 + gate * sq.e
        y_ref[bi] = _rmsnorm(x2, fin_g_ref[...])

    units = iter([(c, hh) for c in range(n_chunks) for hh in range(H_B)])
    pairs = iter([(j, hh) for j in range(rows_per_step) for hh in range(H_B)])
    stages = {
        "norm": stage_norm, "conv": stage_conv, "u": stage_u, "qkv": stage_qkv, "v": stage_v,
        "gate_o": stage_gate_o, "gate_z": stage_gate_z, "za": stage_za,
        "heads_lo": lambda sq, bi: gmlp_heads(sq, bi, list(range(0, H_A // 2))),
        "heads_hi": lambda sq, bi: gmlp_heads(sq, bi, list(range(H_A // 2, H_A))),
        "unit": lambda sq, bi: mlstm_unit(sq, bi, *next(units)),
        "pair": lambda sq, bi: sample_state_pair(*next(pairs)),
        "ple": stage_ple, "merge_a": stage_merge_a, "merge_b": stage_merge_b, "out": stage_out,
    }
    assert STEP_ORDER.count("unit") == n_chunks * H_B and STEP_ORDER.count("pair") == rows_per_step * H_B
    for name in STEP_ORDER:
        stages[name](seqs[0], 0)

    @pl.when(step == pl.num_programs(0) * pl.num_programs(1) - 1)
    def _():
        _sample_post_kernel(sx_ref, sp_ref, sya_ref, scq_ref, sg_ref, snuma_ref, sdenom_ref, sosig_ref, sskipxc_ref,
                            szsilu_ref, vec_ref, w_out_ref, w_pg_ref, w_pp_ref, sy_ref,
                            ycat_ref.at[0, 0:sx_ref.shape[0]])


def _resident(shape):
    nd = len(shape)
    return pl.BlockSpec(shape, lambda *_: (0,) * nd, pipeline_mode=pl.Buffered(1))


def _prompt_call(x, p, weights, sample_c, sample_rows, sample_tail):
    batch, seq, _ = x.shape
    tt = PROMPT_TILE
    ns = SEQS_PER_STEP
    grid = (batch // ns, seq // tt)
    n_steps = grid[0] * grid[1]
    n_rows = sample_c.shape[0]
    rows_per_step = n_rows // n_steps
    assert rows_per_step * n_steps == n_rows, "sample rows are spread evenly over the prompt grid steps"

    def state_block(b, t):
        return (b * grid[1] + t, 0, 0, 0)

    in_specs = [
        pl.BlockSpec((ns, tt, D_MODEL), lambda b, t: (b, t, 0)),
        pl.BlockSpec((ns, tt, D_PLE), lambda b, t: (b, t, 0)),
    ] + [_resident(w.shape) for w in weights] + [
        pl.BlockSpec((rows_per_step, H_B, DH_B, DH_B), state_block),
    ] + [_resident(a.shape) for a in sample_rows] + [_resident(a.shape) for a in sample_tail]
    out_shape = (
        jax.ShapeDtypeStruct((batch, seq, D_MODEL), F32),
        jax.ShapeDtypeStruct((batch, H_B, DH_B, DH_B), F32),
        jax.ShapeDtypeStruct((batch, H_B, DH_B), F32),
        jax.ShapeDtypeStruct((batch, H_B, SUBLANES, LANES), F32),
        jax.ShapeDtypeStruct((batch, CONV_W - 1, W_B), F32),
        jax.ShapeDtypeStruct(sample_c.shape, F32),
        jax.ShapeDtypeStruct((n_rows, W_B), F32),
        jax.ShapeDtypeStruct((n_rows, 1, D_MODEL), F32),
    )
    out_specs = (
        pl.BlockSpec((ns, tt, D_MODEL), lambda b, t: (b, t, 0)),
        pl.BlockSpec((ns, H_B, DH_B, DH_B), lambda b, t: (b, 0, 0, 0)),
        pl.BlockSpec((ns, H_B, DH_B), lambda b, t: (b, 0, 0)),
        pl.BlockSpec((ns, H_B, SUBLANES, LANES), lambda b, t: (b, 0, 0, 0)),
        pl.BlockSpec((ns, CONV_W - 1, W_B), lambda b, t: (b, 0, 0)),
        pl.BlockSpec((rows_per_step, H_B, DH_B, DH_B), state_block),
        pl.BlockSpec((n_rows, W_B), lambda b, t: (0, 0)),
        pl.BlockSpec((n_rows, 1, D_MODEL), lambda b, t: (0, 0, 0)),
    )
    scratch = [
        pltpu.VMEM((ns, tt + SUBLANES, W_B), F32),
        pltpu.VMEM((ns, tt, 3 * W_B), BF16),
        pltpu.VMEM((ns, tt, W_A + W_B), BF16),
    ]
    return pl.pallas_call(
        _prompt_kernel,
        grid=grid,
        in_specs=in_specs,
        out_specs=out_specs,
        out_shape=out_shape,
        scratch_shapes=scratch,
        compiler_params=pltpu.CompilerParams(
            dimension_semantics=("arbitrary", "arbitrary"),
            vmem_limit_bytes=PROMPT_VMEM_BYTES),
        name="prompt_fused",
    )(x, p, *weights, sample_c, *sample_rows, *sample_tail)


def _sample_pre_kernel(x_ref, conv_ref, n_ref, m_ref, vec_ref, w_in_ref,
                       wq_f32_ref, wk_f32_ref, wv_f32_ref, w_if_ref, w_out_f32_ref, w_pg_f32_ref, w_pp_f32_ref,
                       vn_ref, ya_ref, convnew_ref, q_ref, k_ref, vs_ref, gfull_ref, numa_ref,
                       denom_ref, osig_ref, skipxc_ref, zsilu_ref, nnew_ref, mnew_ref, w_in_bf16_ref,
                       w_q_ref, w_k_ref, w_v_ref, w_out_ref, w_pg_ref, w_pp_ref,
                       qkv_ref, hb_ref, proj_ref):
    slab = pl.program_id(0)
    vp = _vector_params(vec_ref)

    @pl.when(slab == 0)
    def _():
        hb_ref[...] = _rmsnorm(x_ref[:, 0, :], vp["norm_g"][...]).astype(BF16)

    w_slab = w_in_ref[...].astype(BF16)
    w_in_bf16_ref[...] = w_slab
    proj_ref[slab] = _dot(hb_ref[...], w_slab)

    @pl.when(slab < 2)
    def _():
        w_out_ref[...] = w_out_f32_ref[...].astype(BF16)

    @pl.when(slab == 2)
    def _():
        w_pg_ref[...] = w_pg_f32_ref[...].astype(BF16)

    @pl.when(slab == 3)
    def _():
        w_pp_ref[...] = w_pp_f32_ref[...].astype(BF16)
        w_q_ref[...] = wq_f32_ref[...].astype(BF16)

    @pl.when(slab == 4)
    def _():
        w_k_ref[...] = (wk_f32_ref[...] * (DH_B ** -0.5)).astype(BF16)
        w_v_ref[...] = wv_f32_ref[...].astype(BF16)

    @pl.when(slab == pl.num_programs(0) - 1)
    def _():
        _sample_front_end(conv_ref, n_ref, m_ref, vp["lnv_g"], vp["lnv_b"], vp["w00"], vp["b0"], vp["conv_w"],
                          vp["conv_b"], w_q_ref, w_k_ref, w_v_ref, w_if_ref, vp["b_if"], vp["skip"],
                          vn_ref, ya_ref, convnew_ref, q_ref, k_ref, vs_ref, gfull_ref, numa_ref,
                          denom_ref, osig_ref, skipxc_ref, zsilu_ref, nnew_ref, mnew_ref, qkv_ref, proj_ref)


def _sample_front_end(conv_ref, n_ref, m_ref, lnv_g_ref, lnv_b_ref, w00_ref, b0_ref, conv_w_ref, conv_b_ref,
                      w_q_ref, w_k_ref, w_v_ref, w_if_ref, b_if_ref, skip_ref,
                      vn_ref, ya_ref, convnew_ref, q_ref, k_ref, vs_ref, gfull_ref, numa_ref,
                      denom_ref, osig_ref, skipxc_ref, zsilu_ref, nnew_ref, mnew_ref, qkv_ref, proj_ref):
    def proj(i):
        return proj_ref[i]

    u = _gelu(proj(0))
    v = _gelu(proj(1))
    z_a = proj(2)
    for hh in range(H_A):
        sl = slice(hh * DH_A, (hh + 1) * DH_A)
        vn = _layernorm(v[:, sl], lnv_g_ref[:, sl], lnv_b_ref[:, sl])
        vn_ref[:, 0, sl] = vn
        sp = vn * w00_ref[:, sl] + b0_ref[:, sl]
        ya_ref[:, sl] = (u[:, sl] * sp * _silu(z_a[:, sl])).astype(BF16)

    x_b = proj(3)
    osig_ref[...] = jax.nn.sigmoid(proj(4))
    zsilu_ref[...] = _silu(proj(5))
    c0 = conv_ref[:, 0, :]
    c1 = conv_ref[:, 1, :]
    c2 = conv_ref[:, 2, :]
    xc = c0 * conv_w_ref[0:1, :]
    xc = xc + c1 * conv_w_ref[1:2, :]
    xc = xc + c2 * conv_w_ref[2:3, :]
    xc = xc + x_b * conv_w_ref[3:4, :]
    xc = _silu(xc + conv_b_ref[...])
    convnew_ref[:, 0, :] = c1
    convnew_ref[:, 1, :] = c2
    convnew_ref[:, 2, :] = x_b
    skipxc_ref[...] = skip_ref[...] * xc
    xcb = xc.astype(BF16)
    xbb = x_b.astype(BF16)
    for hh in range(H_B):
        sl = slice(hh * DH_B, (hh + 1) * DH_B)
        qkv_ref[:, hh * DH_B:(hh + 1) * DH_B] = _dot(xcb[:, sl], w_q_ref[hh]).astype(BF16)
        qkv_ref[:, W_B + hh * DH_B:W_B + (hh + 1) * DH_B] = _dot(xcb[:, sl], w_k_ref[hh]).astype(BF16)
        qkv_ref[:, 2 * W_B + hh * DH_B:2 * W_B + (hh + 1) * DH_B] = _dot(xbb[:, sl], w_v_ref[hh]).astype(BF16)
    gates = _dot(qkv_ref[...], w_if_ref[...]) + b_if_ref[...]
    logf = pltpu.roll(jax.nn.log_sigmoid(gates), LANES - H_B, 1)
    li = gates[:, 0:H_B]
    inter = logf[:, 0:H_B] + m_ref[...]
    m_t = jnp.maximum(inter, li)
    s_w = jnp.exp(li - m_t)
    g = jnp.exp(inter - m_t)
    e_neg = jnp.exp(-m_t)
    mnew_ref[...] = m_t
    for hh in range(H_B):
        sl = slice(hh * DH_B, (hh + 1) * DH_B)
        qf = qkv_ref[:, hh * DH_B:(hh + 1) * DH_B].astype(F32)
        kf = qkv_ref[:, W_B + hh * DH_B:W_B + (hh + 1) * DH_B].astype(F32)
        vf = qkv_ref[:, 2 * W_B + hh * DH_B:2 * W_B + (hh + 1) * DH_B].astype(F32)
        s_h = s_w[:, hh:hh + 1]
        g_h = g[:, hh:hh + 1]
        n_old = n_ref[:, hh, :]
        s_qk = s_h * jnp.sum(qf * kf, axis=-1, keepdims=True)
        den = s_qk + g_h * jnp.sum(n_old * qf, axis=-1, keepdims=True)
        denom = jnp.maximum(jnp.abs(den), e_neg[:, hh:hh + 1])
        q_ref[:, sl] = qf
        k_ref[:, sl] = kf
        vs_ref[:, sl] = s_h * vf
        gfull_ref[:, sl] = jnp.broadcast_to(g_h, qf.shape)
        numa_ref[:, sl] = s_qk * vf
        denom_ref[:, sl] = jnp.broadcast_to(denom, qf.shape)
        nnew_ref[:, hh, :] = g_h * n_old + s_h * kf


def _sample_post_kernel(x_ref, p_ref, ya_ref, cq_ref, gfull_ref, numa_ref, denom_ref, osig_ref,
                        skipxc_ref, zsilu_ref, vec_ref, w_out_ref, w_pg_ref, w_pp_ref, y_ref, ycat_ref):
    vp = _vector_params(vec_ref)
    gn_g_ref, b_pg_ref, ple_g_ref, fin_g_ref = vp["gn_g"], vp["b_pg"], vp["ple_g"], vp["fin_g"]
    ycat_ref[:, 0:W_A] = ya_ref[...]
    for hh in range(H_B):
        sl = slice(hh * DH_B, (hh + 1) * DH_B)
        hcell = (numa_ref[:, sl] + gfull_ref[:, sl] * cq_ref[:, sl]) / denom_ref[:, sl]
        hn = _layernorm(hcell * osig_ref[:, sl], gn_g_ref[:, sl])
        ycat_ref[:, W_A + hh * DH_B:W_A + (hh + 1) * DH_B] = ((hn + skipxc_ref[:, sl]) * zsilu_ref[:, sl]).astype(BF16)
    x1 = x_ref[:, 0, :] + _dot(ycat_ref[...], w_out_ref[...])
    e = _rmsnorm(_dot(p_ref[:, 0, :].astype(BF16), w_pp_ref[...]), ple_g_ref[...])
    gate = jax.nn.sigmoid(_dot(x1.astype(BF16), w_pg_ref[...]) + b_pg_ref[...])
    x2 = x1 + gate * e
    y_ref[:, 0, :] = _rmsnorm(x2, fin_g_ref[...])


def _sample_pre_call(before_w, w_in_f32, w_qkv_f32, w_if, w_out_f32, w_tail_f32, out_shape):
    nb = before_w[0].shape[0]
    n_slabs = N_IN // W_A
    whole = lambda s: pl.BlockSpec(s.shape, lambda i, nd=len(s.shape): (0,) * nd)
    slab_spec = pl.BlockSpec((D_MODEL, W_A), lambda i: (0, i))
    out_rows = pl.BlockSpec((w_out_f32.shape[0] // 2, D_MODEL), lambda i: (jnp.minimum(i, 1), 0))
    cast = lambda a: jax.ShapeDtypeStruct(a.shape, BF16)
    n_plain = len(out_shape)
    out_shape = (tuple(out_shape) + (cast(w_in_f32),) + tuple(cast(a) for a in w_qkv_f32)
                 + (cast(w_out_f32),) + tuple(cast(a) for a in w_tail_f32))
    after_w = list(w_qkv_f32) + [w_if]
    return pl.pallas_call(
        _sample_pre_kernel,
        grid=(n_slabs,),
        in_specs=[_resident(a.shape) for a in before_w] + [slab_spec] + [_resident(a.shape) for a in after_w]
        + [out_rows] + [_resident(a.shape) for a in w_tail_f32],
        out_specs=tuple(whole(s) for s in out_shape[:n_plain]) + (slab_spec,)
        + tuple(whole(cast(a)) for a in w_qkv_f32) + (out_rows,) + tuple(whole(cast(a)) for a in w_tail_f32),
        out_shape=out_shape,
        scratch_shapes=[pltpu.VMEM((nb, 3 * W_B), BF16), pltpu.VMEM((nb, D_MODEL), BF16),
                        pltpu.VMEM((n_slabs, nb, W_A), F32)],
        compiler_params=pltpu.CompilerParams(
            dimension_semantics=("arbitrary",), vmem_limit_bytes=SAMPLE_VMEM_BYTES),
        name="sample_pre",
    )(*before_w, w_in_f32, *after_w, w_out_f32, *w_tail_f32)


def _single_step_call(body, inputs, out_shape, scratch, name):
    return pl.pallas_call(
        body,
        grid=(1,),
        in_specs=[_resident(a.shape) for a in inputs],
        out_specs=tuple(pl.BlockSpec(s.shape, lambda *_, nd=len(s.shape): (0,) * nd) for s in out_shape),
        out_shape=out_shape,
        scratch_shapes=scratch,
        compiler_params=pltpu.CompilerParams(
            dimension_semantics=("arbitrary",), vmem_limit_bytes=SAMPLE_VMEM_BYTES),
        name=name,
    )(*inputs)


def kernel(x_prompt, x_sample, state_mlstm_C, state_mlstm_n, state_mlstm_m, state_conv, p_prompt, p_sample,
           norm_in_g, w_in, ln_v_g, ln_v_b, w_spatial, b_spatial, conv_w, conv_b, w_q, w_k, w_v, w_if, b_if,
           gn_g, skip, w_out, w_ple_gate, b_ple_gate, w_ple_proj, ple_norm_g, final_norm_g):
    assert norm_in_g.shape[0] == 1, "single-layer trunk"
    nb = x_sample.shape[0]

    w_sp = w_spatial[0]
    b_sp_full = jnp.repeat(b_spatial[0].T, DH_A, axis=1)
    k_fix = jnp.concatenate([jnp.ones((W_B, 1), F32), jnp.full((W_B, 1), DH_B ** 0.5, F32), jnp.ones((W_B, 1), F32)], 0)
    w_if_p = jnp.pad(w_if[0] * k_fix, ((0, 0), (0, LANES - 2 * H_B))).astype(BF16)
    w00 = jnp.repeat(w_sp[:, 0, 0], DH_A)
    b0 = jnp.repeat(b_spatial[0][:, 0], DH_A)
    vec = _pack_vectors(norm_in_g[0], ln_v_g[0], ln_v_b[0], conv_b[0], gn_g[0], skip[0], b_ple_gate[0],
                        ple_norm_g[0], final_norm_g, conv_w[0], b_if[0], w00, b0)

    full = jax.ShapeDtypeStruct((nb, W_B), F32)
    pre_out = (
        jax.ShapeDtypeStruct((nb, 1, W_A), F32),
        jax.ShapeDtypeStruct((nb, W_A), BF16),
        jax.ShapeDtypeStruct((nb, CONV_W - 1, W_B), F32),
        full, full, full, full, full, full, full, full, full,
        jax.ShapeDtypeStruct((nb, H_B, DH_B), F32),
        jax.ShapeDtypeStruct((nb, H_B), F32),
    )
    (vn_s, ya_s, conv_new, q_s, k_s, vs_s, gfull, numa, denom, osig, skipxc, zsilu, n_new, m_new,
     w_in_b, wq_b, wk_b, wv_b, w_out_b, w_pg_b, w_pp_b) = _sample_pre_call(
        [x_sample, state_conv[0], state_mlstm_n[0], state_mlstm_m[0], vec], w_in[0],
        [w_q[0], w_k[0], w_v[0]], w_if_p, w_out[0], [w_ple_gate[0], w_ple_proj[0]], pre_out)

    prompt_weights = [vec, w_in_b, w_sp, b_sp_full, wq_b, wk_b, wv_b, w_if_p, w_out_b, w_pg_b, w_pp_b]
    y_p, c_p, n_p, m_p, conv_p, c_new, _, y_s = _prompt_call(
        x_prompt, p_prompt[0], prompt_weights, state_mlstm_C[0], [q_s, k_s, vs_s, gfull],
        [x_sample, p_sample[0], ya_s, numa, denom, osig, skipxc, zsilu])

    return (y_p, y_s, c_p[None], n_p[None], m_p[:, :, 0, 0][None], conv_p[None],
            c_new[None], n_new[None], m_new[None], conv_new[None], vn_s[None])
```

```python
import jax
import jax.numpy as jnp
from jax import lax
from jax.experimental import pallas as pl
from jax.experimental.pallas import tpu as pltpu

D_MODEL = 1024
W_A = 1024
H_A = 8
DH_A = 128
W_B = 1024
H_B = 4
DH_B = 256
CHUNK = 128
CONV_W = 4
D_PLE = 256
N_IN = 3 * W_A + 3 * W_B
EPS = 1e-6

LANES = 128
SUBLANES = 8
PROMPT_TILE = 256
SEQS_PER_STEP = 1
STEP_ORDER = (
    "ple", "norm", "conv", "u", "v", "za", "qkv", "heads_lo", "gate_o", "heads_hi", "gate_z",
    "unit", "unit", "unit", "unit", "unit", "merge_a", "unit", "unit", "unit",
    "merge_b", "pair", "pair", "pair", "out", "pair", "pair", "pair", "pair", "pair",
)
PROMPT_VMEM_BYTES = 60 * 1024 * 1024
SAMPLE_VMEM_BYTES = 58 * 1024 * 1024

F32 = jnp.float32
BF16 = jnp.bfloat16
NT_DIMS = (((1,), (1,)), ((), ()))
TN_DIMS = (((0,), (0,)), ((), ()))


def _rmsnorm(x, g):
    return x * lax.rsqrt(jnp.mean(x * x, axis=-1, keepdims=True) + EPS) * g


def _layernorm(x, g, b=None):
    mu = jnp.mean(x, axis=-1, keepdims=True)
    d = x - mu
    var = jnp.mean(d * d, axis=-1, keepdims=True)
    y = d * lax.rsqrt(var + EPS) * g
    return y if b is None else y + b


def _gelu(x):
    return 0.5 * x * (1.0 + lax.erf(x * (0.5 ** 0.5)))


def _silu(x):
    return x * jax.nn.sigmoid(x)


def _dot(a, b):
    return jnp.dot(a, b, preferred_element_type=F32)


def _cumsum_rows(tril_b, x):
    x1 = x.astype(BF16)
    r1 = x - x1.astype(F32)
    x2 = r1.astype(BF16)
    x3 = (r1 - x2.astype(F32)).astype(BF16)
    return _dot(tril_b, x1) + _dot(tril_b, x2) + _dot(tril_b, x3)


def _vector_params(vec_ref):
    row = lambda i: vec_ref.at[i:i + 1]
    return dict(norm_g=row(0), lnv_g=row(1), lnv_b=row(2), conv_b=row(3), gn_g=row(4), skip=row(5), b_pg=row(6),
                ple_g=row(7), fin_g=row(8), conv_w=vec_ref.at[9:9 + CONV_W], b_if=vec_ref.at[13:14, 0:LANES],
                w00=row(14), b0=row(15))


def _pack_vectors(norm_g, lnv_g, lnv_b, conv_b, gn_g, skip, b_pg, ple_g, fin_g, conv_w, b_if, w00, b0):
    wide = lambda a: jnp.pad(a.reshape(1, -1).astype(F32), ((0, 0), (0, D_MODEL - a.size)))
    rows = [wide(a) for a in (norm_g, lnv_g, lnv_b, conv_b, gn_g, skip, b_pg, ple_g, fin_g)]
    return jnp.concatenate(rows + [conv_w.astype(F32), wide(b_if), wide(w00), wide(b0)], axis=0)


def _prompt_kernel(x_ref, p_ref, vec_ref, w_in_ref, w_sp_ref, b_sp_ref, w_q_ref, w_k_ref, w_v_ref, w_if_ref,
                   w_out_ref, w_pg_ref, w_pp_ref,
                   sc_ref, sq_ref, sk_ref, svs_ref, sg_ref,
                   sx_ref, sp_ref, sya_ref, snuma_ref, sdenom_ref, sosig_ref, sskipxc_ref, szsilu_ref,
                   y_ref, c_ref, n_ref, m_ref, conv_ref, scnew_ref, scq_ref, sy_ref,
                   xpad_ref, qkv_ref, ycat_ref):
    vp = _vector_params(vec_ref)
    norm_g_ref, lnv_g_ref, lnv_b_ref, conv_w_ref, conv_b_ref = (
        vp["norm_g"], vp["lnv_g"], vp["lnv_b"], vp["conv_w"], vp["conv_b"])
    b_if_ref, gn_g_ref, skip_ref, b_pg_ref, ple_g_ref, fin_g_ref = (
        vp["b_if"], vp["gn_g"], vp["skip"], vp["b_pg"], vp["ple_g"], vp["fin_g"])
    n_seq, tt = x_ref.shape[0], x_ref.shape[1]
    n_chunks = tt // CHUNK
    step = pl.program_id(0) * pl.num_programs(1) + pl.program_id(1)

    @pl.when(pl.program_id(1) == 0)
    def _():
        c_ref[...] = jnp.zeros_like(c_ref)
        n_ref[...] = jnp.zeros_like(n_ref)
        m_ref[...] = jnp.zeros_like(m_ref)
        conv_ref[...] = jnp.zeros_like(conv_ref)

    @pl.when(step == 0)
    def _():
        scq_ref[...] = jnp.zeros_like(scq_ref)

    rows_per_step = sc_ref.shape[0]
    group_row = lax.broadcasted_iota(jnp.int32, (SUBLANES, DH_B), 0)

    def sample_state_pair(j, hh):
        sl = slice(hh * DH_B, (hh + 1) * DH_B)
        r = step * rows_per_step + j
        base = pl.multiple_of((r // SUBLANES) * SUBLANES, SUBLANES)
        is_row = group_row == (r - base)
        group = pl.ds(base, SUBLANES)
        c_old = sc_ref[j, hh]
        cq = lax.dot_general(sq_ref[group, sl].astype(BF16), c_old.astype(BF16), NT_DIMS, preferred_element_type=F32)
        scq_ref[group, sl] = jnp.where(is_row, cq, scq_ref[group, sl])
        k_row = jnp.where(is_row, sk_ref[group, sl], 0.0).astype(BF16)
        outer = lax.dot_general(svs_ref[group, sl].astype(BF16), k_row, TN_DIMS, preferred_element_type=F32)
        scnew_ref[j, hh] = sg_ref[pl.ds(r, 1), sl] * c_old + outer

    row = lax.broadcasted_iota(jnp.int32, (CHUNK, CHUNK), 0)
    col = lax.broadcasted_iota(jnp.int32, (CHUNK, CHUNK), 1)
    causal = row >= col
    tril = causal.astype(F32)
    tril_b = causal.astype(BF16)

    class Seq:
        pass

    seqs = [Seq() for _ in range(n_seq)]

    def proj(sq, i):
        return _dot(sq.hb, w_in_ref[:, i * W_A:(i + 1) * W_A])

    def stage_norm(sq, bi):
        sq.hb = _rmsnorm(x_ref[bi], norm_g_ref[...]).astype(BF16)

    def stage_u(sq, bi):
        sq.u = _gelu(proj(sq, 0))

    def stage_v(sq, bi):
        sq.v = _gelu(proj(sq, 1))

    def stage_za(sq, bi):
        sq.z_a = proj(sq, 2)

    def gmlp_heads(sq, bi, heads):
        for hh in heads:
            sl = slice(hh * DH_A, (hh + 1) * DH_A)
            vn = _layernorm(sq.v[:, sl], lnv_g_ref[:, sl], lnv_b_ref[:, sl]).astype(BF16)
            wm = (w_sp_ref[hh] * tril).astype(BF16)
            vn_wide = jnp.concatenate([vn[c * CHUNK:(c + 1) * CHUNK] for c in range(n_chunks)], axis=1)
            sp_wide = _dot(wm, vn_wide)
            for c in range(n_chunks):
                rows = slice(c * CHUNK, (c + 1) * CHUNK)
                sp = sp_wide[:, c * DH_A:(c + 1) * DH_A] + b_sp_ref[:, sl]
                ycat_ref[bi, rows, sl] = (sq.u[rows, sl] * sp * _silu(sq.z_a[rows, sl])).astype(BF16)

    def stage_conv(sq, bi):
        x_b = proj(sq, 3)
        xpad_ref[bi, SUBLANES - 3:SUBLANES, :] = conv_ref[bi]
        xpad_ref[bi, SUBLANES:SUBLANES + tt, :] = x_b
        conv_ref[bi] = x_b[tt - 3:tt, :]
        xc = xpad_ref[bi, SUBLANES - 3:SUBLANES - 3 + tt, :] * conv_w_ref[0:1, :]
        xc = xc + xpad_ref[bi, SUBLANES - 2:SUBLANES - 2 + tt, :] * conv_w_ref[1:2, :]
        xc = xc + xpad_ref[bi, SUBLANES - 1:SUBLANES - 1 + tt, :] * conv_w_ref[2:3, :]
        xc = xc + x_b * conv_w_ref[3:4, :]
        sq.xc = _silu(xc + conv_b_ref[...])
        sq.xcb = sq.xc.astype(BF16)
        sq.xbb = x_b.astype(BF16)

    def stage_gate_o(sq, bi):
        sq.o_sig = jax.nn.sigmoid(proj(sq, 4))

    def stage_gate_z(sq, bi):
        sq.z_silu = _silu(proj(sq, 5))

    def stage_qkv(sq, bi):
        for hh in range(H_B):
            sl = slice(hh * DH_B, (hh + 1) * DH_B)
            qkv_ref[bi, :, hh * DH_B:(hh + 1) * DH_B] = _dot(sq.xcb[:, sl], w_q_ref[hh]).astype(BF16)
            qkv_ref[bi, :, W_B + hh * DH_B:W_B + (hh + 1) * DH_B] = _dot(sq.xcb[:, sl], w_k_ref[hh]).astype(BF16)
            qkv_ref[bi, :, 2 * W_B + hh * DH_B:2 * W_B + (hh + 1) * DH_B] = _dot(sq.xbb[:, sl], w_v_ref[hh]).astype(BF16)
        gates = _dot(qkv_ref[bi], w_if_ref[...]) + b_if_ref[...]
        logf = jax.nn.log_sigmoid(gates)
        sq.decay = []
        for c in range(n_chunks):
            rows = slice(c * CHUNK, (c + 1) * CHUNK)
            bcum = pltpu.roll(_cumsum_rows(tril_b, logf[rows]), LANES - H_B, 1)
            a_all = gates[rows] - bcum
            sq.decay.append((bcum, a_all, a_all.T))

    def mlstm_unit(sq, bi, c, hh):
        rows = slice(c * CHUNK, (c + 1) * CHUNK)
        sl = slice(hh * DH_B, (hh + 1) * DH_B)
        bcum, a_all, a_t = sq.decay[c]
        qh = qkv_ref[bi, rows, hh * DH_B:(hh + 1) * DH_B]
        kh = qkv_ref[bi, rows, W_B + hh * DH_B:W_B + (hh + 1) * DH_B]
        vh = qkv_ref[bi, rows, 2 * W_B + hh * DH_B:2 * W_B + (hh + 1) * DH_B]
        m_prev = m_ref[bi, hh, 0:1, 0:1]
        a_mat = jnp.where(causal, a_t[hh:hh + 1, :], -jnp.inf)
        m_run = jnp.maximum(jnp.max(a_mat, axis=-1, keepdims=True), m_prev)
        s = jnp.exp(a_mat - m_run) * lax.dot_general(qh, kh, NT_DIMS, preferred_element_type=F32)
        g = jnp.exp(m_prev - m_run)
        c_old = c_ref[bi, hh]
        inter = lax.dot_general(qh, c_old.astype(BF16), NT_DIMS, preferred_element_type=F32)
        num = _dot(s.astype(BF16), vh) + g * inter
        n_old = n_ref[bi, hh:hh + 1, :]
        den = jnp.sum(s, axis=-1, keepdims=True) + g * jnp.sum(qh.astype(F32) * n_old, axis=-1, keepdims=True)
        m_t = bcum[:, hh:hh + 1] + m_run
        hcell = num / jnp.maximum(jnp.abs(den), jnp.exp(-m_t))
        m_last = m_run[CHUNK - 1:CHUNK, :]
        w_end = jnp.exp(a_all[:, hh:hh + 1] - m_last)
        g_end = jnp.exp(m_prev - m_last)
        kw = kh.astype(F32) * w_end
        c_ref[bi, hh] = g_end * c_old + lax.dot_general(vh, kw.astype(BF16), TN_DIMS, preferred_element_type=F32)
        n_ref[bi, hh:hh + 1, :] = g_end * n_old + jnp.sum(kw, axis=0, keepdims=True)
        m_ref[bi, hh] = jnp.broadcast_to(m_t[CHUNK - 1:CHUNK, :], (SUBLANES, LANES))
        hn = _layernorm(hcell * sq.o_sig[rows, sl], gn_g_ref[:, sl])
        yb = (hn + skip_ref[:, sl] * sq.xc[rows, sl]) * sq.z_silu[rows, sl]
        ycat_ref[bi, rows, W_A + hh * DH_B:W_A + (hh + 1) * DH_B] = yb.astype(BF16)

    def stage_ple(sq, bi):
        sq.e = _rmsnorm(_dot(p_ref[bi].astype(BF16), w_pp_ref[...]), ple_g_ref[...])

    def stage_merge_a(sq, bi):
        sq.x1 = x_ref[bi] + _dot(ycat_ref[bi, :, :W_A], w_out_ref[:W_A, :])

    def stage_merge_b(sq, bi):
        sq.x1 = sq.x1 + _dot(ycat_ref[bi, :, W_A:], w_out_ref[W_A:, :])

    def stage_out(sq, bi):
        gate = jax.nn.sigmoid(_dot(sq.x1.astype(BF16), w_pg_ref[...]) + b_pg_ref[...])
        x2 = sq.x1 + gate * sq.e
        y_ref[bi] = _rmsnorm(x2, fin_g_ref[...])

    units = iter([(c, hh) for c in range(n_chunks) for hh in range(H_B)])
    pairs = iter([(j, hh) for j in range(rows_per_step) for hh in range(H_B)])
    stages = {
        "norm": stage_norm, "conv": stage_conv, "u": stage_u, "qkv": stage_qkv, "v": stage_v,
        "gate_o": stage_gate_o, "gate_z": stage_gate_z, "za": stage_za,
        "heads_lo": lambda sq, bi: gmlp_heads(sq, bi, list(range(0, H_A // 2))),
        "heads_hi": lambda sq, bi: gmlp_heads(sq, bi, list(range(H_A // 2, H_A))),
        "unit": lambda sq, bi: mlstm_unit(sq, bi, *next(units)),
        "pair": lambda sq, bi: sample_state_pair(*next(pairs)),
        "ple": stage_ple, "merge_a": stage_merge_a, "merge_b": stage_merge_b, "out": stage_out,
    }
    assert STEP_ORDER.count("unit") == n_chunks * H_B and STEP_ORDER.count("pair") == rows_per_step * H_B
    for name in STEP_ORDER:
        stages[name](seqs[0], 0)

    @pl.when(step == pl.num_programs(0) * pl.num_programs(1) - 1)
    def _():
        _sample_post_kernel(sx_ref, sp_ref, sya_ref, scq_ref, sg_ref, snuma_ref, sdenom_ref, sosig_ref, sskipxc_ref,
                            szsilu_ref, vec_ref, w_out_ref, w_pg_ref, w_pp_ref, sy_ref,
                            ycat_ref.at[0, 0:sx_ref.shape[0]])


def _resident(shape):
    nd = len(shape)
    return pl.BlockSpec(shape, lambda *_: (0,) * nd, pipeline_mode=pl.Buffered(1))


def _prompt_call(x, p, weights, sample_c, sample_rows, sample_tail):
    batch, seq, _ = x.shape
    tt = PROMPT_TILE
    ns = SEQS_PER_STEP
    grid = (batch // ns, seq // tt)
    n_steps = grid[0] * grid[1]
    n_rows = sample_c.shape[0]
    rows_per_step = n_rows // n_steps
    assert rows_per_step * n_steps == n_rows, "sample rows are spread evenly over the prompt grid steps"

    def state_block(b, t):
        return (b * grid[1] + t, 0, 0, 0)

    in_specs = [
        pl.BlockSpec((ns, tt, D_MODEL), lambda b, t: (b, t, 0)),
        pl.BlockSpec((ns, tt, D_PLE), lambda b, t: (b, t, 0)),
    ] + [_resident(w.shape) for w in weights] + [
        pl.BlockSpec((rows_per_step, H_B, DH_B, DH_B), state_block),
    ] + [_resident(a.shape) for a in sample_rows] + [_resident(a.shape) for a in sample_tail]
    out_shape = (
        jax.ShapeDtypeStruct((batch, seq, D_MODEL), F32),
        jax.ShapeDtypeStruct((batch, H_B, DH_B, DH_B), F32),
        jax.ShapeDtypeStruct((batch, H_B, DH_B), F32),
        jax.ShapeDtypeStruct((batch, H_B, SUBLANES, LANES), F32),
        jax.ShapeDtypeStruct((batch, CONV_W - 1, W_B), F32),
        jax.ShapeDtypeStruct(sample_c.shape, F32),
        jax.ShapeDtypeStruct((n_rows, W_B), F32),
        jax.ShapeDtypeStruct((n_rows, 1, D_MODEL), F32),
    )
    out_specs = (
        pl.BlockSpec((ns, tt, D_MODEL), lambda b, t: (b, t, 0)),
        pl.BlockSpec((ns, H_B, DH_B, DH_B), lambda b, t: (b, 0, 0, 0)),
        pl.BlockSpec((ns, H_B, DH_B), lambda b, t: (b, 0, 0)),
        pl.BlockSpec((ns, H_B, SUBLANES, LANES), lambda b, t: (b, 0, 0, 0)),
        pl.BlockSpec((ns, CONV_W - 1, W_B), lambda b, t: (b, 0, 0)),
        pl.BlockSpec((rows_per_step, H_B, DH_B, DH_B), state_block),
        pl.BlockSpec((n_rows, W_B), lambda b, t: (0, 0)),
        pl.BlockSpec((n_rows, 1, D_MODEL), lambda b, t: (0, 0, 0)),
    )
    scratch = [
        pltpu.VMEM((ns, tt + SUBLANES, W_B), F32),
        pltpu.VMEM((ns, tt, 3 * W_B), BF16),
        pltpu.VMEM((ns, tt, W_A + W_B), BF16),
    ]
    return pl.pallas_call(
        _prompt_kernel,
        grid=grid,
        in_specs=in_specs,
        out_specs=out_specs,
        out_shape=out_shape,
        scratch_shapes=scratch,
        compiler_params=pltpu.CompilerParams(
            dimension_semantics=("arbitrary", "arbitrary"),
            vmem_limit_bytes=PROMPT_VMEM_BYTES),
        name="prompt_fused",
    )(x, p, *weights, sample_c, *sample_rows, *sample_tail)


def _sample_pre_kernel(x_ref, conv_ref, n_ref, m_ref, vec_ref, w_in_ref,
                       wq_f32_ref, wk_f32_ref, wv_f32_ref, w_if_ref, w_out_f32_ref, w_pg_f32_ref, w_pp_f32_ref,
                       vn_ref, ya_ref, convnew_ref, q_ref, k_ref, vs_ref, gfull_ref, numa_ref,
                       denom_ref, osig_ref, skipxc_ref, zsilu_ref, nnew_ref, mnew_ref, w_in_bf16_ref,
                       w_q_ref, w_k_ref, w_v_ref, w_out_ref, w_pg_ref, w_pp_ref,
                       qkv_ref, hb_ref, proj_ref):
    slab = pl.program_id(0)
    vp = _vector_params(vec_ref)

    @pl.when(slab == 0)
    def _():
        hb_ref[...] = _rmsnorm(x_ref[:, 0, :], vp["norm_g"][...]).astype(BF16)

    w_slab = w_in_ref[...].astype(BF16)
    w_in_bf16_ref[...] = w_slab
    proj_ref[slab] = _dot(hb_ref[...], w_slab)

    @pl.when(slab < 2)
    def _():
        w_out_ref[...] = w_out_f32_ref[...].astype(BF16)

    @pl.when(slab == 2)
    def _():
        w_pg_ref[...] = w_pg_f32_ref[...].astype(BF16)

    @pl.when(slab == 3)
    def _():
        w_pp_ref[...] = w_pp_f32_ref[...].astype(BF16)
        w_q_ref[...] = wq_f32_ref[...].astype(BF16)

    @pl.when(slab == 4)
    def _():
        w_k_ref[...] = (wk_f32_ref[...] * (DH_B ** -0.5)).astype(BF16)
        w_v_ref[...] = wv_f32_ref[...].astype(BF16)

    @pl.when(slab == pl.num_programs(0) - 1)
    def _():
        _sample_front_end(conv_ref, n_ref, m_ref, vp["lnv_g"], vp["lnv_b"], vp["w00"], vp["b0"], vp["conv_w"],
                          vp["conv_b"], w_q_ref, w_k_ref, w_v_ref, w_if_ref, vp["b_if"], vp["skip"],
                          vn_ref, ya_ref, convnew_ref, q_ref, k_ref, vs_ref, gfull_ref, numa_ref,
                          denom_ref, osig_ref, skipxc_ref, zsilu_ref, nnew_ref, mnew_ref, qkv_ref, proj_ref)


def _sample_front_end(conv_ref, n_ref, m_ref, lnv_g_ref, lnv_b_ref, w00_ref, b0_ref, conv_w_ref, conv_b_ref,
                      w_q_ref, w_k_ref, w_v_ref, w_if_ref, b_if_ref, skip_ref,
                      vn_ref, ya_ref, convnew_ref, q_ref, k_ref, vs_ref, gfull_ref, numa_ref,
                      denom_ref, osig_ref, skipxc_ref, zsilu_ref, nnew_ref, mnew_ref, qkv_ref, proj_ref):
    def proj(i):
        return proj_ref[i]

    u = _gelu(proj(0))
    v = _gelu(proj(1))
    z_a = proj(2)
    for hh in range(H_A):
        sl = slice(hh * DH_A, (hh + 1) * DH_A)
        vn = _layernorm(v[:, sl], lnv_g_ref[:, sl], lnv_b_ref[:, sl])
        vn_ref[:, 0, sl] = vn
        sp = vn * w00_ref[:, sl] + b0_ref[:, sl]
        ya_ref[:, sl] = (u[:, sl] * sp * _silu(z_a[:, sl])).astype(BF16)

    x_b = proj(3)
    osig_ref[...] = jax.nn.sigmoid(proj(4))
    zsilu_ref[...] = _silu(proj(5))
    c0 = conv_ref[:, 0, :]
    c1 = conv_ref[:, 1, :]
    c2 = conv_ref[:, 2, :]
    xc = c0 * conv_w_ref[0:1, :]
    xc = xc + c1 * conv_w_ref[1:2, :]
    xc = xc + c2 * conv_w_ref[2:3, :]
    xc = xc + x_b * conv_w_ref[3:4, :]
    xc = _silu(xc + conv_b_ref[...])
    convnew_ref[:, 0, :] = c1
    convnew_ref[:, 1, :] = c2
    convnew_ref[:, 2, :] = x_b
    skipxc_ref[...] = skip_ref[...] * xc
    xcb = xc.astype(BF16)
    xbb = x_b.astype(BF16)
    for hh in range(H_B):
        sl = slice(hh * DH_B, (hh + 1) * DH_B)
        qkv_ref[:, hh * DH_B:(hh + 1) * DH_B] = _dot(xcb[:, sl], w_q_ref[hh]).astype(BF16)
        qkv_ref[:, W_B + hh * DH_B:W_B + (hh + 1) * DH_B] = _dot(xcb[:, sl], w_k_ref[hh]).astype(BF16)
        qkv_ref[:, 2 * W_B + hh * DH_B:2 * W_B + (hh + 1) * DH_B] = _dot(xbb[:, sl], w_v_ref[hh]).astype(BF16)
    gates = _dot(qkv_ref[...], w_if_ref[...]) + b_if_ref[...]
    logf = pltpu.roll(jax.nn.log_sigmoid(gates), LANES - H_B, 1)
    li = gates[:, 0:H_B]
    inter = logf[:, 0:H_B] + m_ref[...]
    m_t = jnp.maximum(inter, li)
    s_w = jnp.exp(li - m_t)
    g = jnp.exp(inter - m_t)
    e_neg = jnp.exp(-m_t)
    mnew_ref[...] = m_t
    for hh in range(H_B):
        sl = slice(hh * DH_B, (hh + 1) * DH_B)
        qf = qkv_ref[:, hh * DH_B:(hh + 1) * DH_B].astype(F32)
        kf = qkv_ref[:, W_B + hh * DH_B:W_B + (hh + 1) * DH_B].astype(F32)
        vf = qkv_ref[:, 2 * W_B + hh * DH_B:2 * W_B + (hh + 1) * DH_B].astype(F32)
        s_h = s_w[:, hh:hh + 1]
        g_h = g[:, hh:hh + 1]
        n_old = n_ref[:, hh, :]
        s_qk = s_h * jnp.sum(qf * kf, axis=-1, keepdims=True)
        den = s_qk + g_h * jnp.sum(n_old * qf, axis=-1, keepdims=True)
        denom = jnp.maximum(jnp.abs(den), e_neg[:, hh:hh + 1])
        q_ref[:, sl] = qf
        k_ref[:, sl] = kf
        vs_ref[:, sl] = s_h * vf
        gfull_ref[:, sl] = jnp.broadcast_to(g_h, qf.shape)
        numa_ref[:, sl] = s_qk * vf
        denom_ref[:, sl] = jnp.broadcast_to(denom, qf.shape)
        nnew_ref[:, hh, :] = g_h * n_old + s_h * kf


def _sample_post_kernel(x_ref, p_ref, ya_ref, cq_ref, gfull_ref, numa_ref, denom_ref, osig_ref,
                        skipxc_ref, zsilu_ref, vec_ref, w_out_ref, w_pg_ref, w_pp_ref, y_ref, ycat_ref):
    vp = _vector_params(vec_ref)
    gn_g_ref, b_pg_ref, ple_g_ref, fin_g_ref = vp["gn_g"], vp["b_pg"], vp["ple_g"], vp["fin_g"]
    ycat_ref[:, 0:W_A] = ya_ref[...]
    for hh in range(H_B):
        sl = slice(hh * DH_B, (hh + 1) * DH_B)
        hcell = (numa_ref[:, sl] + gfull_ref[:, sl] * cq_ref[:, sl]) / denom_ref[:, sl]
        hn = _layernorm(hcell * osig_ref[:, sl], gn_g_ref[:, sl])
        ycat_ref[:, W_A + hh * DH_B:W_A + (hh + 1) * DH_B] = ((hn + skipxc_ref[:, sl]) * zsilu_ref[:, sl]).astype(BF16)
    x1 = x_ref[:, 0, :] + _dot(ycat_ref[...], w_out_ref[...])
    e = _rmsnorm(_dot(p_ref[:, 0, :].astype(BF16), w_pp_ref[...]), ple_g_ref[...])
    gate = jax.nn.sigmoid(_dot(x1.astype(BF16), w_pg_ref[...]) + b_pg_ref[...])
    x2 = x1 + gate * e
    y_ref[:, 0, :] = _rmsnorm(x2, fin_g_ref[...])


def _sample_pre_call(before_w, w_in_f32, w_qkv_f32, w_if, w_out_f32, w_tail_f32, out_shape):
    nb = before_w[0].shape[0]
    n_slabs = N_IN // W_A
    whole = lambda s: pl.BlockSpec(s.shape, lambda i, nd=len(s.shape): (0,) * nd)
    slab_spec = pl.BlockSpec((D_MODEL, W_A), lambda i: (0, i))
    out_rows = pl.BlockSpec((w_out_f32.shape[0] // 2, D_MODEL), lambda i: (jnp.minimum(i, 1), 0))
    cast = lambda a: jax.ShapeDtypeStruct(a.shape, BF16)
    n_plain = len(out_shape)
    out_shape = (tuple(out_shape) + (cast(w_in_f32),) + tuple(cast(a) for a in w_qkv_f32)
                 + (cast(w_out_f32),) + tuple(cast(a) for a in w_tail_f32))
    after_w = list(w_qkv_f32) + [w_if]
    return pl.pallas_call(
        _sample_pre_kernel,
        grid=(n_slabs,),
        in_specs=[_resident(a.shape) for a in before_w] + [slab_spec] + [_resident(a.shape) for a in after_w]
        + [out_rows] + [_resident(a.shape) for a in w_tail_f32],
        out_specs=tuple(whole(s) for s in out_shape[:n_plain]) + (slab_spec,)
        + tuple(whole(cast(a)) for a in w_qkv_f32) + (out_rows,) + tuple(whole(cast(a)) for a in w_tail_f32),
        out_shape=out_shape,
        scratch_shapes=[pltpu.VMEM((nb, 3 * W_B), BF16), pltpu.VMEM((nb, D_MODEL), BF16),
                        pltpu.VMEM((n_slabs, nb, W_A), F32)],
        compiler_params=pltpu.CompilerParams(
            dimension_semantics=("arbitrary",), vmem_limit_bytes=SAMPLE_VMEM_BYTES),
        name="sample_pre",
    )(*before_w, w_in_f32, *after_w, w_out_f32, *w_tail_f32)


def kernel(x_prompt, x_sample, state_mlstm_C, state_mlstm_n, state_mlstm_m, state_conv, p_prompt, p_sample,
           norm_in_g, w_in, ln_v_g, ln_v_b, w_spatial, b_spatial, conv_w, conv_b, w_q, w_k, w_v, w_if, b_if,
           gn_g, skip, w_out, w_ple_gate, b_ple_gate, w_ple_proj, ple_norm_g, final_norm_g):
    assert norm_in_g.shape[0] == 1, "single-layer trunk"
    nb = x_sample.shape[0]

    w_sp = w_spatial[0]
    b_sp_full = jnp.repeat(b_spatial[0].T, DH_A, axis=1)
    k_fix = jnp.concatenate([jnp.ones((W_B, 1), F32), jnp.full((W_B, 1), DH_B ** 0.5, F32), jnp.ones((W_B, 1), F32)], 0)
    w_if_p = jnp.pad(w_if[0] * k_fix, ((0, 0), (0, LANES - 2 * H_B))).astype(BF16)
    w00 = jnp.repeat(w_sp[:, 0, 0], DH_A)
    b0 = jnp.repeat(b_spatial[0][:, 0], DH_A)
    vec = _pack_vectors(norm_in_g[0], ln_v_g[0], ln_v_b[0], conv_b[0], gn_g[0], skip[0], b_ple_gate[0],
                        ple_norm_g[0], final_norm_g, conv_w[0], b_if[0], w00, b0)

    full = jax.ShapeDtypeStruct((nb, W_B), F32)
    pre_out = (
        jax.ShapeDtypeStruct((nb, 1, W_A), F32),
        jax.ShapeDtypeStruct((nb, W_A), BF16),
        jax.ShapeDtypeStruct((nb, CONV_W - 1, W_B), F32),
        full, full, full, full, full, full, full, full, full,
        jax.ShapeDtypeStruct((nb, H_B, DH_B), F32),
        jax.ShapeDtypeStruct((nb, H_B), F32),
    )
    (vn_s, ya_s, conv_new, q_s, k_s, vs_s, gfull, numa, denom, osig, skipxc, zsilu, n_new, m_new,
     w_in_b, wq_b, wk_b, wv_b, w_out_b, w_pg_b, w_pp_b) = _sample_pre_call(
        [x_sample, state_conv[0], state_mlstm_n[0], state_mlstm_m[0], vec], w_in[0],
        [w_q[0], w_k[0], w_v[0]], w_if_p, w_out[0], [w_ple_gate[0], w_ple_proj[0]], pre_out)

    prompt_weights = [vec, w_in_b, w_sp, b_sp_full, wq_b, wk_b, wv_b, w_if_p, w_out_b, w_pg_b, w_pp_b]
    y_p, c_p, n_p, m_p, conv_p, c_new, _, y_s = _prompt_call(
        x_prompt, p_prompt[0], prompt_weights, state_mlstm_C[0], [q_s, k_s, vs_s, gfull],
        [x_sample, p_sample[0], ya_s, numa, denom, osig, skipxc, zsilu])

    return (y_p, y_s, c_p[None], n_p[None], m_p[:, :, 0, 0][None], conv_p[None],
            c_new[None], n_new[None], m_new[None], conv_new[None], vn_s[None])
```

```python
import jax
import jax.numpy as jnp
from jax import lax
from jax.experimental import pallas as pl
from jax.experimental.pallas import tpu as pltpu

D_MODEL = 1024
W_A = 1024
H_A = 8
DH_A = 128
W_B = 1024
H_B = 4
DH_B = 256
CHUNK = 128
CONV_W = 4
D_PLE = 256
N_IN = 3 * W_A + 3 * W_B
EPS = 1e-6

LANES = 128
SUBLANES = 8
PROMPT_TILE = 256
SEQS_PER_STEP = 1
STEP_ORDER = (
    "ple", "norm", "conv", "u", "v", "za", "qkv", "heads_lo", "heads_hi", "gate_o", "gate_z",
    "unit", "unit", "unit", "unit", "unit", "merge_a", "unit", "unit", "unit",
    "merge_b", "pair", "pair", "pair", "out", "pair", "pair", "pair", "pair", "pair",
)
PROMPT_VMEM_BYTES = 60 * 1024 * 1024
SAMPLE_VMEM_BYTES = 58 * 1024 * 1024

F32 = jnp.float32
BF16 = jnp.bfloat16
NT_DIMS = (((1,), (1,)), ((), ()))
TN_DIMS = (((0,), (0,)), ((), ()))


def _rmsnorm(x, g):
    return x * lax.rsqrt(jnp.mean(x * x, axis=-1, keepdims=True) + EPS) * g


def _layernorm(x, g, b=None):
    mu = jnp.mean(x, axis=-1, keepdims=True)
    d = x - mu
    var = jnp.mean(d * d, axis=-1, keepdims=True)
    y = d * lax.rsqrt(var + EPS) * g
    return y if b is None else y + b


def _gelu(x):
    return 0.5 * x * (1.0 + lax.erf(x * (0.5 ** 0.5)))


def _silu(x):
    return x * jax.nn.sigmoid(x)


def _dot(a, b):
    return jnp.dot(a, b, preferred_element_type=F32)


def _cumsum_rows(tril_b, x):
    x1 = x.astype(BF16)
    r1 = x - x1.astype(F32)
    x2 = r1.astype(BF16)
    x3 = (r1 - x2.astype(F32)).astype(BF16)
    return _dot(tril_b, x1) + _dot(tril_b, x2) + _dot(tril_b, x3)


def _vector_params(vec_ref):
    row = lambda i: vec_ref.at[i:i + 1]
    return dict(norm_g=row(0), lnv_g=row(1), lnv_b=row(2), conv_b=row(3), gn_g=row(4), skip=row(5), b_pg=row(6),
                ple_g=row(7), fin_g=row(8), conv_w=vec_ref.at[9:9 + CONV_W], b_if=vec_ref.at[13:14, 0:LANES],
                w00=row(14), b0=row(15))


def _pack_vectors(norm_g, lnv_g, lnv_b, conv_b, gn_g, skip, b_pg, ple_g, fin_g, conv_w, b_if, w00, b0):
    wide = lambda a: jnp.pad(a.reshape(1, -1).astype(F32), ((0, 0), (0, D_MODEL - a.size)))
    rows = [wide(a) for a in (norm_g, lnv_g, lnv_b, conv_b, gn_g, skip, b_pg, ple_g, fin_g)]
    return jnp.concatenate(rows + [conv_w.astype(F32), wide(b_if), wide(w00), wide(b0)], axis=0)


def _prompt_kernel(x_ref, p_ref, vec_ref, w_in_ref, w_sp_ref, b_sp_ref, w_q_ref, w_k_ref, w_v_ref, w_if_ref,
                   w_out_ref, w_pg_ref, w_pp_ref,
                   sc_ref, sq_ref, sk_ref, svs_ref, sg_ref,
                   sx_ref, sp_ref, sya_ref, snuma_ref, sdenom_ref, sosig_ref, sskipxc_ref, szsilu_ref,
                   y_ref, c_ref, n_ref, m_ref, conv_ref, scnew_ref, scq_ref, sy_ref,
                   xpad_ref, qkv_ref, ycat_ref):
    vp = _vector_params(vec_ref)
    norm_g_ref, lnv_g_ref, lnv_b_ref, conv_w_ref, conv_b_ref = (
        vp["norm_g"], vp["lnv_g"], vp["lnv_b"], vp["conv_w"], vp["conv_b"])
    b_if_ref, gn_g_ref, skip_ref, b_pg_ref, ple_g_ref, fin_g_ref = (
        vp["b_if"], vp["gn_g"], vp["skip"], vp["b_pg"], vp["ple_g"], vp["fin_g"])
    n_seq, tt = x_ref.shape[0], x_ref.shape[1]
    n_chunks = tt // CHUNK
    step = pl.program_id(0) * pl.num_programs(1) + pl.program_id(1)

    @pl.when(pl.program_id(1) == 0)
    def _():
        c_ref[...] = jnp.zeros_like(c_ref)
        n_ref[...] = jnp.zeros_like(n_ref)
        m_ref[...] = jnp.zeros_like(m_ref)
        conv_ref[...] = jnp.zeros_like(conv_ref)

    @pl.when(step == 0)
    def _():
        scq_ref[...] = jnp.zeros_like(scq_ref)

    rows_per_step = sc_ref.shape[0]
    group_row = lax.broadcasted_iota(jnp.int32, (SUBLANES, DH_B), 0)

    def sample_state_pair(j, hh):
        sl = slice(hh * DH_B, (hh + 1) * DH_B)
        r = step * rows_per_step + j
        base = pl.multiple_of((r // SUBLANES) * SUBLANES, SUBLANES)
        is_row = group_row == (r - base)
        group = pl.ds(base, SUBLANES)
        c_old = sc_ref[j, hh]
        cq = lax.dot_general(sq_ref[group, sl].astype(BF16), c_old.astype(BF16), NT_DIMS, preferred_element_type=F32)
        scq_ref[group, sl] = jnp.where(is_row, cq, scq_ref[group, sl])
        k_row = jnp.where(is_row, sk_ref[group, sl], 0.0).astype(BF16)
        outer = lax.dot_general(svs_ref[group, sl].astype(BF16), k_row, TN_DIMS, preferred_element_type=F32)
        scnew_ref[j, hh] = sg_ref[pl.ds(r, 1), sl] * c_old + outer

    row = lax.broadcasted_iota(jnp.int32, (CHUNK, CHUNK), 0)
    col = lax.broadcasted_iota(jnp.int32, (CHUNK, CHUNK), 1)
    causal = row >= col
    tril = causal.astype(F32)
    tril_b = causal.astype(BF16)

    class Seq:
        pass

    seqs = [Seq() for _ in range(n_seq)]

    def proj(sq, i):
        return _dot(sq.hb, w_in_ref[:, i * W_A:(i + 1) * W_A])

    def stage_norm(sq, bi):
        sq.hb = _rmsnorm(x_ref[bi], norm_g_ref[...]).astype(BF16)

    def stage_u(sq, bi):
        sq.u = _gelu(proj(sq, 0))

    def stage_v(sq, bi):
        sq.v = _gelu(proj(sq, 1))

    def stage_za(sq, bi):
        sq.z_a = proj(sq, 2)

    def gmlp_heads(sq, bi, heads):
        for hh in heads:
            sl = slice(hh * DH_A, (hh + 1) * DH_A)
            vn = _layernorm(sq.v[:, sl], lnv_g_ref[:, sl], lnv_b_ref[:, sl]).astype(BF16)
            wm = (w_sp_ref[hh] * tril).astype(BF16)
            vn_wide = jnp.concatenate([vn[c * CHUNK:(c + 1) * CHUNK] for c in range(n_chunks)], axis=1)
            sp_wide = _dot(wm, vn_wide)
            for c in range(n_chunks):
                rows = slice(c * CHUNK, (c + 1) * CHUNK)
                sp = sp_wide[:, c * DH_A:(c + 1) * DH_A] + b_sp_ref[:, sl]
                ycat_ref[bi, rows, sl] = (sq.u[rows, sl] * sp * _silu(sq.z_a[rows, sl])).astype(BF16)

    def stage_conv(sq, bi):
        x_b = proj(sq, 3)
        xpad_ref[bi, SUBLANES - 3:SUBLANES, :] = conv_ref[bi]
        xpad_ref[bi, SUBLANES:SUBLANES + tt, :] = x_b
        conv_ref[bi] = x_b[tt - 3:tt, :]
        xc = xpad_ref[bi, SUBLANES - 3:SUBLANES - 3 + tt, :] * conv_w_ref[0:1, :]
        xc = xc + xpad_ref[bi, SUBLANES - 2:SUBLANES - 2 + tt, :] * conv_w_ref[1:2, :]
        xc = xc + xpad_ref[bi, SUBLANES - 1:SUBLANES - 1 + tt, :] * conv_w_ref[2:3, :]
        xc = xc + x_b * conv_w_ref[3:4, :]
        sq.xc = _silu(xc + conv_b_ref[...])
        sq.xcb = sq.xc.astype(BF16)
        sq.xbb = x_b.astype(BF16)

    def stage_gate_o(sq, bi):
        sq.o_sig = jax.nn.sigmoid(proj(sq, 4))

    def stage_gate_z(sq, bi):
        sq.z_silu = _silu(proj(sq, 5))

    def stage_qkv(sq, bi):
        for hh in range(H_B):
            sl = slice(hh * DH_B, (hh + 1) * DH_B)
            qkv_ref[bi, :, hh * DH_B:(hh + 1) * DH_B] = _dot(sq.xcb[:, sl], w_q_ref[hh]).astype(BF16)
            qkv_ref[bi, :, W_B + hh * DH_B:W_B + (hh + 1) * DH_B] = _dot(sq.xcb[:, sl], w_k_ref[hh]).astype(BF16)
            qkv_ref[bi, :, 2 * W_B + hh * DH_B:2 * W_B + (hh + 1) * DH_B] = _dot(sq.xbb[:, sl], w_v_ref[hh]).astype(BF16)
        gates = _dot(qkv_ref[bi], w_if_ref[...]) + b_if_ref[...]
        logf = jax.nn.log_sigmoid(gates)
        sq.decay = []
        for c in range(n_chunks):
            rows = slice(c * CHUNK, (c + 1) * CHUNK)
            bcum = pltpu.roll(_cumsum_rows(tril_b, logf[rows]), LANES - H_B, 1)
            a_all = gates[rows] - bcum
            sq.decay.append((bcum, a_all, a_all.T))

    def mlstm_unit(sq, bi, c, hh):
        rows = slice(c * CHUNK, (c + 1) * CHUNK)
        sl = slice(hh * DH_B, (hh + 1) * DH_B)
        bcum, a_all, a_t = sq.decay[c]
        qh = qkv_ref[bi, rows, hh * DH_B:(hh + 1) * DH_B]
        kh = qkv_ref[bi, rows, W_B + hh * DH_B:W_B + (hh + 1) * DH_B]
        vh = qkv_ref[bi, rows, 2 * W_B + hh * DH_B:2 * W_B + (hh + 1) * DH_B]
        m_prev = m_ref[bi, hh, 0:1, 0:1]
        a_mat = jnp.where(causal, a_t[hh:hh + 1, :], -jnp.inf)
        m_run = jnp.maximum(jnp.max(a_mat, axis=-1, keepdims=True), m_prev)
        s = jnp.exp(a_mat - m_run) * lax.dot_general(qh, kh, NT_DIMS, preferred_element_type=F32)
        g = jnp.exp(m_prev - m_run)
        c_old = c_ref[bi, hh]
        inter = lax.dot_general(qh, c_old.astype(BF16), NT_DIMS, preferred_element_type=F32)
        num = _dot(s.astype(BF16), vh) + g * inter
        n_old = n_ref[bi, hh:hh + 1, :]
        den = jnp.sum(s, axis=-1, keepdims=True) + g * jnp.sum(qh.astype(F32) * n_old, axis=-1, keepdims=True)
        m_t = bcum[:, hh:hh + 1] + m_run
        hcell = num / jnp.maximum(jnp.abs(den), jnp.exp(-m_t))
        m_last = m_run[CHUNK - 1:CHUNK, :]
        w_end = jnp.exp(a_all[:, hh:hh + 1] - m_last)
        g_end = jnp.exp(m_prev - m_last)
        kw = kh.astype(F32) * w_end
        c_ref[bi, hh] = g_end * c_old + lax.dot_general(vh, kw.astype(BF16), TN_DIMS, preferred_element_type=F32)
        n_ref[bi, hh:hh + 1, :] = g_end * n_old + jnp.sum(kw, axis=0, keepdims=True)
        m_ref[bi, hh] = jnp.broadcast_to(m_t[CHUNK - 1:CHUNK, :], (SUBLANES, LANES))
        hn = _layernorm(hcell * sq.o_sig[rows, sl], gn_g_ref[:, sl])
        yb = (hn + skip_ref[:, sl] * sq.xc[rows, sl]) * sq.z_silu[rows, sl]
        ycat_ref[bi, rows, W_A + hh * DH_B:W_A + (hh + 1) * DH_B] = yb.astype(BF16)

    def stage_ple(sq, bi):
        sq.e = _rmsnorm(_dot(p_ref[bi].astype(BF16), w_pp_ref[...]), ple_g_ref[...])

    def stage_merge_a(sq, bi):
        sq.x1 = x_ref[bi] + _dot(ycat_ref[bi, :, :W_A], w_out_ref[:W_A, :])

    def stage_merge_b(sq, bi):
        sq.x1 = sq.x1 + _dot(ycat_ref[bi, :, W_A:], w_out_ref[W_A:, :])

    def stage_out(sq, bi):
        gate = jax.nn.sigmoid(_dot(sq.x1.astype(BF16), w_pg_ref[...]) + b_pg_ref[...])
        x2 = sq.x1 + gate * sq.e
        y_ref[bi] = _rmsnorm(x2, fin_g_ref[...])

    units = iter([(c, hh) for c in range(n_chunks) for hh in range(H_B)])
    pairs = iter([(j, hh) for j in range(rows_per_step) for hh in range(H_B)])
    stages = {
        "norm": stage_norm, "conv": stage_conv, "u": stage_u, "qkv": stage_qkv, "v": stage_v,
        "gate_o": stage_gate_o, "gate_z": stage_gate_z, "za": stage_za,
        "heads_lo": lambda sq, bi: gmlp_heads(sq, bi, list(range(0, H_A // 2))),
        "heads_hi": lambda sq, bi: gmlp_heads(sq, bi, list(range(H_A // 2, H_A))),
        "unit": lambda sq, bi: mlstm_unit(sq, bi, *next(units)),
        "pair": lambda sq, bi: sample_state_pair(*next(pairs)),
        "ple": stage_ple, "merge_a": stage_merge_a, "merge_b": stage_merge_b, "out": stage_out,
    }
    assert STEP_ORDER.count("unit") == n_chunks * H_B and STEP_ORDER.count("pair") == rows_per_step * H_B
    for name in STEP_ORDER:
        stages[name](seqs[0], 0)

    @pl.when(step == pl.num_programs(0) * pl.num_programs(1) - 1)
    def _():
        _sample_post_kernel(sx_ref, sp_ref, sya_ref, scq_ref, sg_ref, snuma_ref, sdenom_ref, sosig_ref, sskipxc_ref,
                            szsilu_ref, vec_ref, w_out_ref, w_pg_ref, w_pp_ref, sy_ref,
                            ycat_ref.at[0, 0:sx_ref.shape[0]])


def _resident(shape):
    nd = len(shape)
    return pl.BlockSpec(shape, lambda *_: (0,) * nd, pipeline_mode=pl.Buffered(1))


def _prompt_call(x, p, weights, sample_c, sample_rows, sample_tail):
    batch, seq, _ = x.shape
    tt = PROMPT_TILE
    ns = SEQS_PER_STEP
    grid = (batch // ns, seq // tt)
    n_steps = grid[0] * grid[1]
    n_rows = sample_c.shape[0]
    rows_per_step = n_rows // n_steps
    assert rows_per_step * n_steps == n_rows, "sample rows are spread evenly over the prompt grid steps"

    def state_block(b, t):
        return (b * grid[1] + t, 0, 0, 0)

    in_specs = [
        pl.BlockSpec((ns, tt, D_MODEL), lambda b, t: (b, t, 0)),
        pl.BlockSpec((ns, tt, D_PLE), lambda b, t: (b, t, 0)),
    ] + [_resident(w.shape) for w in weights] + [
        pl.BlockSpec((rows_per_step, H_B, DH_B, DH_B), state_block),
    ] + [_resident(a.shape) for a in sample_rows] + [_resident(a.shape) for a in sample_tail]
    out_shape = (
        jax.ShapeDtypeStruct((batch, seq, D_MODEL), F32),
        jax.ShapeDtypeStruct((batch, H_B, DH_B, DH_B), F32),
        jax.ShapeDtypeStruct((batch, H_B, DH_B), F32),
        jax.ShapeDtypeStruct((batch, H_B, SUBLANES, LANES), F32),
        jax.ShapeDtypeStruct((batch, CONV_W - 1, W_B), F32),
        jax.ShapeDtypeStruct(sample_c.shape, F32),
        jax.ShapeDtypeStruct((n_rows, W_B), F32),
        jax.ShapeDtypeStruct((n_rows, 1, D_MODEL), F32),
    )
    out_specs = (
        pl.BlockSpec((ns, tt, D_MODEL), lambda b, t: (b, t, 0)),
        pl.BlockSpec((ns, H_B, DH_B, DH_B), lambda b, t: (b, 0, 0, 0)),
        pl.BlockSpec((ns, H_B, DH_B), lambda b, t: (b, 0, 0)),
        pl.BlockSpec((ns, H_B, SUBLANES, LANES), lambda b, t: (b, 0, 0, 0)),
        pl.BlockSpec((ns, CONV_W - 1, W_B), lambda b, t: (b, 0, 0)),
        pl.BlockSpec((rows_per_step, H_B, DH_B, DH_B), state_block),
        pl.BlockSpec((n_rows, W_B), lambda b, t: (0, 0)),
        pl.BlockSpec((n_rows, 1, D_MODEL), lambda b, t: (0, 0, 0)),
    )
    scratch = [
        pltpu.VMEM((ns, tt + SUBLANES, W_B), F32),
        pltpu.VMEM((ns, tt, 3 * W_B), BF16),
        pltpu.VMEM((ns, tt, W_A + W_B), BF16),
    ]
    return pl.pallas_call(
        _prompt_kernel,
        grid=grid,
        in_specs=in_specs,
        out_specs=out_specs,
        out_shape=out_shape,
        scratch_shapes=scratch,
        compiler_params=pltpu.CompilerParams(
            dimension_semantics=("arbitrary", "arbitrary"),
            vmem_limit_bytes=PROMPT_VMEM_BYTES),
        name="prompt_fused",
    )(x, p, *weights, sample_c, *sample_rows, *sample_tail)


def _sample_pre_kernel(x_ref, conv_ref, n_ref, m_ref, vec_ref, w_in_ref,
                       wq_f32_ref, wk_f32_ref, wv_f32_ref, w_if_ref, w_out_f32_ref, w_pg_f32_ref, w_pp_f32_ref,
                       vn_ref, ya_ref, convnew_ref, q_ref, k_ref, vs_ref, gfull_ref, numa_ref,
                       denom_ref, osig_ref, skipxc_ref, zsilu_ref, nnew_ref, mnew_ref, w_in_bf16_ref,
                       w_q_ref, w_k_ref, w_v_ref, w_out_ref, w_pg_ref, w_pp_ref,
                       qkv_ref, hb_ref, proj_ref):
    slab = pl.program_id(0)
    vp = _vector_params(vec_ref)

    @pl.when(slab == 0)
    def _():
        hb_ref[...] = _rmsnorm(x_ref[:, 0, :], vp["norm_g"][...]).astype(BF16)

    w_slab = w_in_ref[...].astype(BF16)
    w_in_bf16_ref[...] = w_slab
    proj_ref[slab] = _dot(hb_ref[...], w_slab)

    @pl.when(slab < 2)
    def _():
        w_out_ref[...] = w_out_f32_ref[...].astype(BF16)

    @pl.when(slab == 2)
    def _():
        w_pg_ref[...] = w_pg_f32_ref[...].astype(BF16)

    @pl.when(slab == 3)
    def _():
        w_pp_ref[...] = w_pp_f32_ref[...].astype(BF16)
        w_q_ref[...] = wq_f32_ref[...].astype(BF16)

    @pl.when(slab == 4)
    def _():
        w_k_ref[...] = (wk_f32_ref[...] * (DH_B ** -0.5)).astype(BF16)
        w_v_ref[...] = wv_f32_ref[...].astype(BF16)

    @pl.when(slab == pl.num_programs(0) - 1)
    def _():
        _sample_front_end(conv_ref, n_ref, m_ref, vp["lnv_g"], vp["lnv_b"], vp["w00"], vp["b0"], vp["conv_w"],
                          vp["conv_b"], w_q_ref, w_k_ref, w_v_ref, w_if_ref, vp["b_if"], vp["skip"],
                          vn_ref, ya_ref, convnew_ref, q_ref, k_ref, vs_ref, gfull_ref, numa_ref,
                          denom_ref, osig_ref, skipxc_ref, zsilu_ref, nnew_ref, mnew_ref, qkv_ref, proj_ref)


def _sample_front_end(conv_ref, n_ref, m_ref, lnv_g_ref, lnv_b_ref, w00_ref, b0_ref, conv_w_ref, conv_b_ref,
                      w_q_ref, w_k_ref, w_v_ref, w_if_ref, b_if_ref, skip_ref,
                      vn_ref, ya_ref, convnew_ref, q_ref, k_ref, vs_ref, gfull_ref, numa_ref,
                      denom_ref, osig_ref, skipxc_ref, zsilu_ref, nnew_ref, mnew_ref, qkv_ref, proj_ref):
    def proj(i):
        return proj_ref[i]

    u = _gelu(proj(0))
    v = _gelu(proj(1))
    z_a = proj(2)
    for hh in range(H_A):
        sl = slice(hh * DH_A, (hh + 1) * DH_A)
        vn = _layernorm(v[:, sl], lnv_g_ref[:, sl], lnv_b_ref[:, sl])
        vn_ref[:, 0, sl] = vn
        sp = vn * w00_ref[:, sl] + b0_ref[:, sl]
        ya_ref[:, sl] = (u[:, sl] * sp * _silu(z_a[:, sl])).astype(BF16)

    x_b = proj(3)
    osig_ref[...] = jax.nn.sigmoid(proj(4))
    zsilu_ref[...] = _silu(proj(5))
    c0 = conv_ref[:, 0, :]
    c1 = conv_ref[:, 1, :]
    c2 = conv_ref[:, 2, :]
    xc = c0 * conv_w_ref[0:1, :]
    xc = xc + c1 * conv_w_ref[1:2, :]
    xc = xc + c2 * conv_w_ref[2:3, :]
    xc = xc + x_b * conv_w_ref[3:4, :]
    xc = _silu(xc + conv_b_ref[...])
    convnew_ref[:, 0, :] = c1
    convnew_ref[:, 1, :] = c2
    convnew_ref[:, 2, :] = x_b
    skipxc_ref[...] = skip_ref[...] * xc
    xcb = xc.astype(BF16)
    xbb = x_b.astype(BF16)
    for hh in range(H_B):
        sl = slice(hh * DH_B, (hh + 1) * DH_B)
        qkv_ref[:, hh * DH_B:(hh + 1) * DH_B] = _dot(xcb[:, sl], w_q_ref[hh]).astype(BF16)
        qkv_ref[:, W_B + hh * DH_B:W_B + (hh + 1) * DH_B] = _dot(xcb[:, sl], w_k_ref[hh]).astype(BF16)
        qkv_ref[:, 2 * W_B + hh * DH_B:2 * W_B + (hh + 1) * DH_B] = _dot(xbb[:, sl], w_v_ref[hh]).astype(BF16)
    gates = _dot(qkv_ref[...], w_if_ref[...]) + b_if_ref[...]
    logf = pltpu.roll(jax.nn.log_sigmoid(gates), LANES - H_B, 1)
    li = gates[:, 0:H_B]
    inter = logf[:, 0:H_B] + m_ref[...]
    m_t = jnp.maximum(inter, li)
    s_w = jnp.exp(li - m_t)
    g = jnp.exp(inter - m_t)
    e_neg = jnp.exp(-m_t)
    mnew_ref[...] = m_t
    for hh in range(H_B):
        sl = slice(hh * DH_B, (hh + 1) * DH_B)
        qf = qkv_ref[:, hh * DH_B:(hh + 1) * DH_B].astype(F32)
        kf = qkv_ref[:, W_B + hh * DH_B:W_B + (hh + 1) * DH_B].astype(F32)
        vf = qkv_ref[:, 2 * W_B + hh * DH_B:2 * W_B + (hh + 1) * DH_B].astype(F32)
        s_h = s_w[:, hh:hh + 1]
        g_h = g[:, hh:hh + 1]
        n_old = n_ref[:, hh, :]
        s_qk = s_h * jnp.sum(qf * kf, axis=-1, keepdims=True)
        den = s_qk + g_h * jnp.sum(n_old * qf, axis=-1, keepdims=True)
        denom = jnp.maximum(jnp.abs(den), e_neg[:, hh:hh + 1])
        q_ref[:, sl] = qf
        k_ref[:, sl] = kf
        vs_ref[:, sl] = s_h * vf
        gfull_ref[:, sl] = jnp.broadcast_to(g_h, qf.shape)
        numa_ref[:, sl] = s_qk * vf
        denom_ref[:, sl] = jnp.broadcast_to(denom, qf.shape)
        nnew_ref[:, hh, :] = g_h * n_old + s_h * kf


def _sample_post_kernel(x_ref, p_ref, ya_ref, cq_ref, gfull_ref, numa_ref, denom_ref, osig_ref,
                        skipxc_ref, zsilu_ref, vec_ref, w_out_ref, w_pg_ref, w_pp_ref, y_ref, ycat_ref):
    vp = _vector_params(vec_ref)
    gn_g_ref, b_pg_ref, ple_g_ref, fin_g_ref = vp["gn_g"], vp["b_pg"], vp["ple_g"], vp["fin_g"]
    ycat_ref[:, 0:W_A] = ya_ref[...]
    for hh in range(H_B):
        sl = slice(hh * DH_B, (hh + 1) * DH_B)
        hcell = (numa_ref[:, sl] + gfull_ref[:, sl] * cq_ref[:, sl]) / denom_ref[:, sl]
        hn = _layernorm(hcell * osig_ref[:, sl], gn_g_ref[:, sl])
        ycat_ref[:, W_A + hh * DH_B:W_A + (hh + 1) * DH_B] = ((hn + skipxc_ref[:, sl]) * zsilu_ref[:, sl]).astype(BF16)
    x1 = x_ref[:, 0, :] + _dot(ycat_ref[...], w_out_ref[...])
    e = _rmsnorm(_dot(p_ref[:, 0, :].astype(BF16), w_pp_ref[...]), ple_g_ref[...])
    gate = jax.nn.sigmoid(_dot(x1.astype(BF16), w_pg_ref[...]) + b_pg_ref[...])
    x2 = x1 + gate * e
    y_ref[:, 0, :] = _rmsnorm(x2, fin_g_ref[...])


def _sample_pre_call(before_w, w_in_f32, w_qkv_f32, w_if, w_out_f32, w_tail_f32, out_shape):
    nb = before_w[0].shape[0]
    n_slabs = N_IN // W_A
    whole = lambda s: pl.BlockSpec(s.shape, lambda i, nd=len(s.shape): (0,) * nd)
    slab_spec = pl.BlockSpec((D_MODEL, W_A), lambda i: (0, i))
    out_rows = pl.BlockSpec((w_out_f32.shape[0] // 2, D_MODEL), lambda i: (jnp.minimum(i, 1), 0))
    cast = lambda a: jax.ShapeDtypeStruct(a.shape, BF16)
    n_plain = len(out_shape)
    out_shape = (tuple(out_shape) + (cast(w_in_f32),) + tuple(cast(a) for a in w_qkv_f32)
                 + (cast(w_out_f32),) + tuple(cast(a) for a in w_tail_f32))
    after_w = list(w_qkv_f32) + [w_if]
    return pl.pallas_call(
        _sample_pre_kernel,
        grid=(n_slabs,),
        in_specs=[_resident(a.shape) for a in before_w] + [slab_spec] + [_resident(a.shape) for a in after_w]
        + [out_rows] + [_resident(a.shape) for a in w_tail_f32],
        out_specs=tuple(whole(s) for s in out_shape[:n_plain]) + (slab_spec,)
        + tuple(whole(cast(a)) for a in w_qkv_f32) + (out_rows,) + tuple(whole(cast(a)) for a in w_tail_f32),
        out_shape=out_shape,
        scratch_shapes=[pltpu.VMEM((nb, 3 * W_B), BF16), pltpu.VMEM((nb, D_MODEL), BF16),
                        pltpu.VMEM((n_slabs, nb, W_A), F32)],
        compiler_params=pltpu.CompilerParams(
            dimension_semantics=("arbitrary",), vmem_limit_bytes=SAMPLE_VMEM_BYTES),
        name="sample_pre",
    )(*before_w, w_in_f32, *after_w, w_out_f32, *w_tail_f32)


def kernel(x_prompt, x_sample, state_mlstm_C, state_mlstm_n, state_mlstm_m, state_conv, p_prompt, p_sample,
           norm_in_g, w_in, ln_v_g, ln_v_b, w_spatial, b_spatial, conv_w, conv_b, w_q, w_k, w_v, w_if, b_if,
           gn_g, skip, w_out, w_ple_gate, b_ple_gate, w_ple_proj, ple_norm_g, final_norm_g):
    assert norm_in_g.shape[0] == 1, "single-layer trunk"
    nb = x_sample.shape[0]

    w_sp = w_spatial[0]
    b_sp_full = jnp.repeat(b_spatial[0].T, DH_A, axis=1)
    k_fix = jnp.concatenate([jnp.ones((W_B, 1), F32), jnp.full((W_B, 1), DH_B ** 0.5, F32), jnp.ones((W_B, 1), F32)], 0)
    w_if_p = jnp.pad(w_if[0] * k_fix, ((0, 0), (0, LANES - 2 * H_B))).astype(BF16)
    w00 = jnp.repeat(w_sp[:, 0, 0], DH_A)
    b0 = jnp.repeat(b_spatial[0][:, 0], DH_A)
    vec = _pack_vectors(norm_in_g[0], ln_v_g[0], ln_v_b[0], conv_b[0], gn_g[0], skip[0], b_ple_gate[0],
                        ple_norm_g[0], final_norm_g, conv_w[0], b_if[0], w00, b0)

    full = jax.ShapeDtypeStruct((nb, W_B), F32)
    pre_out = (
        jax.ShapeDtypeStruct((nb, 1, W_A), F32),
        jax.ShapeDtypeStruct((nb, W_A), BF16),
        jax.ShapeDtypeStruct((nb, CONV_W - 1, W_B), F32),
        full, full, full, full, full, full, full, full, full,
        jax.ShapeDtypeStruct((nb, H_B, DH_B), F32),
        jax.ShapeDtypeStruct((nb, H_B), F32),
    )
    (vn_s, ya_s, conv_new, q_s, k_s, vs_s, gfull, numa, denom, osig, skipxc, zsilu, n_new, m_new,
     w_in_b, wq_b, wk_b, wv_b, w_out_b, w_pg_b, w_pp_b) = _sample_pre_call(
        [x_sample, state_conv[0], state_mlstm_n[0], state_mlstm_m[0], vec], w_in[0],
        [w_q[0], w_k[0], w_v[0]], w_if_p, w_out[0], [w_ple_gate[0], w_ple_proj[0]], pre_out)

    prompt_weights = [vec, w_in_b, w_sp, b_sp_full, wq_b, wk_b, wv_b, w_if_p, w_out_b, w_pg_b, w_pp_b]
    y_p, c_p, n_p, m_p, conv_p, c_new, _, y_s = _prompt_call(
        x_prompt, p_prompt[0], prompt_weights, state_mlstm_C[0], [q_s, k_s, vs_s, gfull],
        [x_sample, p_sample[0], ya_s, numa, denom, osig, skipxc, zsilu])

    return (y_p, y_s, c_p[None], n_p[None], m_p[:, :, 0, 0][None], conv_p[None],
            c_new[None], n_new[None], m_new[None], conv_new[None], vn_s[None])
```
